```python
import math
import jax, jax.numpy as jnp
from jax import lax
import numpy as np

D_MODEL = 2048
BATCH = 4
SEQ = 2048
DEPTH = 2

N_MIXERS = 2
N_A = (DEPTH + 1) // 2
N_B = DEPTH // 2
D_FF = 5632
PLE_DIM = 256
CHUNK = 128
GMLP_HALF = 3 * D_MODEL
GMLP_GROUPS = 16
GMLP_GROUP_DIM = GMLP_HALF // GMLP_GROUPS
SSM_WIDTH = D_MODEL
SSM_GROUP_DIM = 16
SSM_GROUPS = SSM_WIDTH // SSM_GROUP_DIM
SSM_STATE = 64
DT_MIN = 0.001
DT_MAX = 0.1
EPS = 1e-6

kernel_name = "hybrid_gmlp_s5_macaron_ple"


def rms_norm(x, g):
    xf = x.astype(jnp.float32)
    y = xf * lax.rsqrt(jnp.mean(xf * xf, axis=-1, keepdims=True) + EPS)
    return (y * g.astype(jnp.float32)).astype(x.dtype)


def swiglu(h, w_gate, w_up, w_down):
    return (jax.nn.silu(h @ w_gate) * (h @ w_up)) @ w_down


def chunked_gmlp(h, w_in, ln_g, ln_b, w_s, b_s, w_out):
    B, L, _ = h.shape
    z = jax.nn.gelu(h @ w_in)
    u, v = jnp.split(z, 2, axis=-1)
    v = v.reshape(B, L, GMLP_GROUPS, GMLP_GROUP_DIM)
    vf = v.astype(jnp.float32)
    mu = jnp.mean(vf, axis=-1, keepdims=True)
    var = jnp.mean(jnp.square(vf - mu), axis=-1, keepdims=True)
    v = ((vf - mu) * lax.rsqrt(var + EPS)).astype(h.dtype)
    v = v * ln_g.reshape(GMLP_GROUPS, GMLP_GROUP_DIM) + ln_b.reshape(GMLP_GROUPS, GMLP_GROUP_DIM)
    v = v.reshape(B, L // CHUNK, CHUNK, GMLP_GROUPS, GMLP_GROUP_DIM)
    causal = jnp.tril(jnp.ones((CHUNK, CHUNK), dtype=bool))
    ws = jnp.where(causal[None], w_s, jnp.zeros_like(w_s))
    sv = jnp.einsum('gts,bnsgd->bntgd', ws, v) + b_s.T[None, None, :, :, None]
    gated = u * sv.reshape(B, L, GMLP_HALF)
    return gated @ w_out


def _diag_scan_op(left, right):
    ar1, ai1, br1, bi1 = left
    ar2, ai2, br2, bi2 = right
    ar = ar2 * ar1 - ai2 * ai1
    ai = ar2 * ai1 + ai2 * ar1
    br = ar2 * br1 - ai2 * bi1 + br2
    bi = ar2 * bi1 + ai2 * br1 + bi2
    return (ar, ai, br, bi)


def s5_mixer(h, w_in, lam_re, lam_im, log_dt, b_re, b_im, c_re, c_im, d_skip, w_out):
    B, L, _ = h.shape
    u = (h @ w_in).reshape(B, L, SSM_GROUPS, SSM_GROUP_DIM).astype(jnp.float32)
    dt = jnp.exp(log_dt.astype(jnp.float32))[:, None]
    lr = lam_re.astype(jnp.float32)
    li = lam_im.astype(jnp.float32)
    mag = jnp.exp(lr * dt)
    ang = li * dt
    abar_r = mag * jnp.cos(ang)
    abar_i = mag * jnp.sin(ang)
    den = lr * lr + li * li
    nr = abar_r - 1.0
    ni = abar_i
    z_r = (nr * lr + ni * li) / den
    z_i = (ni * lr - nr * li) / den
    br_ = b_re.astype(jnp.float32)
    bi_ = b_im.astype(jnp.float32)
    bb_r = z_r[..., None] * br_ - z_i[..., None] * bi_
    bb_i = z_r[..., None] * bi_ + z_i[..., None] * br_
    bu_r = jnp.einsum('blgh,gph->blgp', u, bb_r)
    bu_i = jnp.einsum('blgh,gph->blgp', u, bb_i)
    a_r = jnp.broadcast_to(abar_r[None, None], (1, L, SSM_GROUPS, SSM_STATE))
    a_i = jnp.broadcast_to(abar_i[None, None], (1, L, SSM_GROUPS, SSM_STATE))
    _, _, hr, hi = lax.associative_scan(_diag_scan_op, (a_r, a_i, bu_r, bu_i), axis=1)
    y = (jnp.einsum('blgp,ghp->blgh', hr, c_re.astype(jnp.float32))
         - jnp.einsum('blgp,ghp->blgh', hi, c_im.astype(jnp.float32))
         + d_skip.astype(jnp.float32).reshape(SSM_GROUPS, SSM_GROUP_DIM) * u)
    act = jax.nn.gelu(y).reshape(B, L, SSM_WIDTH).astype(h.dtype)
    val, gate = jnp.split(act @ w_out, 2, axis=-1)
    return val * jax.nn.sigmoid(gate)


def setup_inputs(seed: int = 0) -> dict:
    key = jax.random.key(seed)
    ks = jax.random.split(key, 32)
    f32 = jnp.float32
    nrm = lambda k, shape, scale: jax.random.normal(k, shape, f32) * scale
    x = jax.random.normal(ks[0], (BATCH, SEQ, D_MODEL), f32)
    p = jax.random.normal(ks[1], (DEPTH, BATCH, SEQ, PLE_DIM), f32)
    norm_g = 1.0 + nrm(ks[2], (DEPTH, 4, D_MODEL), 0.02)
    final_norm_g = 1.0 + nrm(ks[3], (D_MODEL,), 0.02)
    ffn_w_gate = nrm(ks[4], (DEPTH, 2, D_MODEL, D_FF), D_MODEL ** -0.5)
    ffn_w_up = nrm(ks[5], (DEPTH, 2, D_MODEL, D_FF), D_MODEL ** -0.5)
    ffn_w_down = nrm(ks[6], (DEPTH, 2, D_FF, D_MODEL), D_FF ** -0.5)
    gmlp_w_in = nrm(ks[7], (N_A, D_MODEL, 2 * GMLP_HALF), D_MODEL ** -0.5)
    gmlp_ln_g = 1.0 + nrm(ks[8], (N_A, GMLP_HALF), 0.02)
    gmlp_ln_b = nrm(ks[9], (N_A, GMLP_HALF), 0.02)
    gmlp_w_s = nrm(ks[10], (N_A, GMLP_GROUPS, CHUNK, CHUNK), CHUNK ** -0.5)
    gmlp_b_s = 1.0 + nrm(ks[11], (N_A, GMLP_GROUPS, CHUNK), 0.1)
    gmlp_w_out = nrm(ks[12], (N_A, GMLP_HALF, D_MODEL), GMLP_HALF ** -0.5)
    s5_w_in = nrm(ks[13], (N_B, D_MODEL, SSM_WIDTH), D_MODEL ** -0.5)
    s5_lam_re = -0.5 + nrm(ks[14], (N_B, SSM_GROUPS, SSM_STATE), 0.01)
    s5_lam_im = (math.pi * jnp.arange(SSM_STATE, dtype=f32))[None, None, :] + nrm(ks[15], (N_B, SSM_GROUPS, SSM_STATE), 0.01)
    s5_log_dt = jax.random.uniform(ks[16], (N_B, SSM_GROUPS), f32, math.log(DT_MIN), math.log(DT_MAX))
    s5_b_re = nrm(ks[17], (N_B, SSM_GROUPS, SSM_STATE, SSM_GROUP_DIM), (2 * SSM_GROUP_DIM) ** -0.5)
    s5_b_im = nrm(ks[18], (N_B, SSM_GROUPS, SSM_STATE, SSM_GROUP_DIM), (2 * SSM_GROUP_DIM) ** -0.5)
    s5_c_re = nrm(ks[19], (N_B, SSM_GROUPS, SSM_GROUP_DIM, SSM_STATE), (2 * SSM_STATE) ** -0.5)
    s5_c_im = nrm(ks[20], (N_B, SSM_GROUPS, SSM_GROUP_DIM, SSM_STATE), (2 * SSM_STATE) ** -0.5)
    s5_d = nrm(ks[21], (N_B, SSM_WIDTH), 1.0)
    s5_w_out = nrm(ks[22], (N_B, SSM_WIDTH, 2 * D_MODEL), SSM_WIDTH ** -0.5)
    ple_w_gate = nrm(ks[23], (DEPTH, D_MODEL, D_MODEL), D_MODEL ** -0.5)
    ple_w_proj = nrm(ks[24], (DEPTH, PLE_DIM, D_MODEL), PLE_DIM ** -0.5)
    return {"x": x, "p": p, "norm_g": norm_g, "final_norm_g": final_norm_g,
            "ffn_w_gate": ffn_w_gate, "ffn_w_up": ffn_w_up, "ffn_w_down": ffn_w_down,
            "gmlp_w_in": gmlp_w_in, "gmlp_ln_g": gmlp_ln_g, "gmlp_ln_b": gmlp_ln_b,
            "gmlp_w_s": gmlp_w_s, "gmlp_b_s": gmlp_b_s, "gmlp_w_out": gmlp_w_out,
            "s5_w_in": s5_w_in, "s5_lam_re": s5_lam_re, "s5_lam_im": s5_lam_im,
            "s5_log_dt": s5_log_dt, "s5_b_re": s5_b_re, "s5_b_im": s5_b_im,
            "s5_c_re": s5_c_re, "s5_c_im": s5_c_im, "s5_d": s5_d, "s5_w_out": s5_w_out,
            "ple_w_gate": ple_w_gate, "ple_w_proj": ple_w_proj}


def reference(x, p, norm_g, final_norm_g, ffn_w_gate, ffn_w_up, ffn_w_down,
              gmlp_w_in, gmlp_ln_g, gmlp_ln_b, gmlp_w_s, gmlp_b_s, gmlp_w_out,
              s5_w_in, s5_lam_re, s5_lam_im, s5_log_dt, s5_b_re, s5_b_im,
              s5_c_re, s5_c_im, s5_d, s5_w_out, ple_w_gate, ple_w_proj):
    for i in range(DEPTH):
        x = x + 0.5 * swiglu(rms_norm(x, norm_g[i, 0]), ffn_w_gate[i, 0], ffn_w_up[i, 0], ffn_w_down[i, 0])
        h = rms_norm(x, norm_g[i, 1])
        j = i // N_MIXERS
        if i % N_MIXERS == 0:
            x = x + chunked_gmlp(h, gmlp_w_in[j], gmlp_ln_g[j], gmlp_ln_b[j],
                                 gmlp_w_s[j], gmlp_b_s[j], gmlp_w_out[j])
        else:
            x = x + s5_mixer(h, s5_w_in[j], s5_lam_re[j], s5_lam_im[j], s5_log_dt[j],
                             s5_b_re[j], s5_b_im[j], s5_c_re[j], s5_c_im[j], s5_d[j], s5_w_out[j])
        x = x + 0.5 * swiglu(rms_norm(x, norm_g[i, 2]), ffn_w_gate[i, 1], ffn_w_up[i, 1], ffn_w_down[i, 1])
        gate = jax.nn.sigmoid(rms_norm(x, norm_g[i, 3]) @ ple_w_gate[i])
        x = x + gate * (p[i] @ ple_w_proj[i])
    return rms_norm(x, final_norm_g)
```

```python
import functools
import math

import jax
import jax.numpy as jnp
from jax import lax
from jax.experimental import pallas as pl
from jax.experimental.pallas import tpu as pltpu

F32 = jnp.float32
BF16 = jnp.bfloat16

D_MODEL = 2048
BATCH = 4
SEQ = 2048
TOKENS = BATCH * SEQ
DEPTH = 2
D_FF = 5632
PLE_DIM = 256
CHUNK = 128
GMLP_HALF = 3 * D_MODEL
GMLP_GROUPS = 16
GMLP_GROUP_DIM = GMLP_HALF // GMLP_GROUPS
SSM_GROUP_DIM = 16
SSM_GROUPS = D_MODEL // SSM_GROUP_DIM
SSM_STATE = 64
EPS = 1e-6

LANES = 128
SUBLANES = 8
VMEM_LIMIT = 56 * 1024 * 1024

SLAB_GROUPS = LANES // SSM_GROUP_DIM
N_SLABS = D_MODEL // LANES
SLAB_STATES = SLAB_GROUPS * SSM_STATE
HALF_STATES = SLAB_STATES // 2
HALF_CH = LANES // 2
SCAN_SLABS = 2
SCAN_T = 128


def _rms(x, g):
    ms = jnp.mean(x * x, axis=-1, keepdims=True)
    return x * lax.rsqrt(ms + EPS) * g


def _gelu(x):
    c = math.sqrt(2.0 / math.pi)
    return x * (0.5 * (1.0 + jnp.tanh(c * (x + 0.044715 * (x * x * x)))))


def _sigmoid(x):
    return 1.0 / (1.0 + jnp.exp(-x))


def _params(sem):
    return pltpu.CompilerParams(dimension_semantics=sem, vmem_limit_bytes=VMEM_LIMIT)


def _ffn_body(x_ref, g_ref, wg_ref, wu_ref, wd_ref, o_ref, h_ref):
    @pl.when(pl.program_id(1) == 0)
    def _():
        x = x_ref[...]
        h_ref[...] = _rms(x, g_ref[...]).astype(BF16)
        o_ref[...] = x

    h = h_ref[...]
    gate = jnp.dot(h, wg_ref[...], preferred_element_type=F32)
    up = jnp.dot(h, wu_ref[...], preferred_element_type=F32)
    a = (gate * _sigmoid(gate)) * (0.5 * up)
    o_ref[...] += jnp.dot(a.astype(BF16), wd_ref[...], preferred_element_type=F32)


def _ffn(x, g, wg, wu, wd, *, tm=1024, tf=256):
    return pl.pallas_call(
        _ffn_body,
        grid=(TOKENS // tm, D_FF // tf),
        in_specs=[
            pl.BlockSpec((tm, D_MODEL), lambda i, j: (i, 0)),
            pl.BlockSpec((1, D_MODEL), lambda i, j: (0, 0)),
            pl.BlockSpec((D_MODEL, tf), lambda i, j: (0, j)),
            pl.BlockSpec((D_MODEL, tf), lambda i, j: (0, j)),
            pl.BlockSpec((tf, D_MODEL), lambda i, j: (j, 0)),
        ],
        out_specs=pl.BlockSpec((tm, D_MODEL), lambda i, j: (i, 0)),
        out_shape=jax.ShapeDtypeStruct((TOKENS, D_MODEL), F32),
        scratch_shapes=[pltpu.VMEM((tm, D_MODEL), BF16)],
        compiler_params=_params(("parallel", "arbitrary")),
    )(x, g, wg, wu, wd)


GMLP_STEP_GROUPS = 2
GMLP_STEP_W = GMLP_STEP_GROUPS * GMLP_GROUP_DIM


def _gmlp_body(x_ref, g_ref, wu_ref, wv_ref, lng_ref, lnb_ref, ws_ref, bs_ref, wo_ref,
               o_ref, h_ref, *, tm):
    @pl.when(pl.program_id(1) == 0)
    def _():
        x = x_ref[...]
        h_ref[...] = _rms(x, g_ref[...]).astype(BF16)
        o_ref[...] = x

    h = h_ref[...]
    zu = _gelu(jnp.dot(h, wu_ref[...], preferred_element_type=F32))
    zv = _gelu(jnp.dot(h, wv_ref[...], preferred_element_type=F32))
    row = lax.broadcasted_iota(jnp.int32, (CHUNK, CHUNK), 0)
    col = lax.broadcasted_iota(jnp.int32, (CHUNK, CHUNK), 1)
    parts = []
    for gi in range(GMLP_STEP_GROUPS):
        lo, hi = gi * GMLP_GROUP_DIM, (gi + 1) * GMLP_GROUP_DIM
        v = zv[:, lo:hi]
        d = v - jnp.mean(v, axis=-1, keepdims=True)
        var = jnp.mean(d * d, axis=-1, keepdims=True)
        vn = ((d * lax.rsqrt(var + EPS)) * lng_ref[:, lo:hi] + lnb_ref[:, lo:hi]).astype(BF16)
        ws = jnp.where(row >= col, ws_ref[gi], 0.0).astype(BF16)
        bias = bs_ref[gi]
        sv = [jnp.dot(ws, vn[c * CHUNK:(c + 1) * CHUNK], preferred_element_type=F32) + bias
              for c in range(tm // CHUNK)]
        parts.append(zu[:, lo:hi] * jnp.concatenate(sv, axis=0))
    gated = jnp.concatenate(parts, axis=-1).astype(BF16)
    o_ref[...] += jnp.dot(gated, wo_ref[...], preferred_element_type=F32)


def _gmlp(x, g, w_in, ln_g, ln_b, w_s, b_s, w_out, *, tm=512):
    nj = GMLP_GROUPS // GMLP_STEP_GROUPS
    return pl.pallas_call(
        functools.partial(_gmlp_body, tm=tm),
        grid=(TOKENS // tm, nj),
        in_specs=[
            pl.BlockSpec((tm, D_MODEL), lambda i, j: (i, 0)),
            pl.BlockSpec((1, D_MODEL), lambda i, j: (0, 0)),
            pl.BlockSpec((D_MODEL, GMLP_STEP_W), lambda i, j: (0, j)),
            pl.BlockSpec((D_MODEL, GMLP_STEP_W), lambda i, j: (0, j + nj)),
            pl.BlockSpec((1, GMLP_STEP_W), lambda i, j: (0, j)),
            pl.BlockSpec((1, GMLP_STEP_W), lambda i, j: (0, j)),
            pl.BlockSpec((GMLP_STEP_GROUPS, CHUNK, CHUNK), lambda i, j: (j, 0, 0)),
            pl.BlockSpec((GMLP_STEP_GROUPS, CHUNK, 1), lambda i, j: (j, 0, 0)),
            pl.BlockSpec((GMLP_STEP_W, D_MODEL), lambda i, j: (j, 0)),
        ],
        out_specs=pl.BlockSpec((tm, D_MODEL), lambda i, j: (i, 0)),
        out_shape=jax.ShapeDtypeStruct((TOKENS, D_MODEL), F32),
        scratch_shapes=[pltpu.VMEM((tm, D_MODEL), BF16)],
        compiler_params=_params(("parallel", "arbitrary")),
    )(x, g, w_in, w_in, ln_g, ln_b, w_s, b_s, w_out)


def _s5_disc_body(lr_ref, li_ref, ldt_ref, br_ref, bi_ref, ar_ref, ai_ref, bbr_ref, bbi_ref):
    lr, li = lr_ref[...], li_ref[...]
    dt = jnp.exp(ldt_ref[...])
    mag = jnp.exp(lr * dt)
    ang = li * dt
    a_r = mag * jnp.cos(ang)
    a_i = mag * jnp.sin(ang)
    den = lr * lr + li * li
    nr = a_r - 1.0
    z_r = (nr * lr + a_i * li) / den
    z_i = (a_i * lr - nr * li) / den
    ar_ref[...] = a_r
    ai_ref[...] = a_i
    bbr_ref[...] = z_r * br_ref[...] - z_i * bi_ref[...]
    bbi_ref[...] = z_r * bi_ref[...] + z_i * br_ref[...]


def _s5_disc(lam_re, lam_im, log_dt, b_re, b_im):
    n = SSM_GROUPS * SSM_STATE
    row = jax.ShapeDtypeStruct((1, n), F32)
    mat = jax.ShapeDtypeStruct((SSM_GROUP_DIM, n), F32)
    to_mat = lambda b: jnp.transpose(b, (2, 0, 1)).reshape(SSM_GROUP_DIM, n)
    ldt = jnp.broadcast_to(log_dt[:, None], (SSM_GROUPS, SSM_STATE)).reshape(1, n)
    return pl.pallas_call(_s5_disc_body, out_shape=(row, row, mat, mat))(
        lam_re.reshape(1, n), lam_im.reshape(1, n), ldt, to_mat(b_re), to_mat(b_im))


def _s5_in_body(x_ref, g_ref, w_ref, o_ref):
    h = _rms(x_ref[...], g_ref[...]).astype(BF16)
    o_ref[...] = jnp.dot(h, w_ref[...], preferred_element_type=F32)


def _s5_in(x, g, w, *, tm=512):
    return pl.pallas_call(
        _s5_in_body,
        grid=(TOKENS // tm,),
        in_specs=[
            pl.BlockSpec((tm, D_MODEL), lambda i: (i, 0)),
            pl.BlockSpec((1, D_MODEL), lambda i: (0, 0)),
            pl.BlockSpec((D_MODEL, D_MODEL), lambda i: (0, 0)),
        ],
        out_specs=pl.BlockSpec((tm, D_MODEL), lambda i: (i, 0)),
        out_shape=jax.ShapeDtypeStruct((TOKENS, D_MODEL), F32),
        compiler_params=_params(("parallel",)),
    )(x, g, w)


def _s5_scan_body(u_ref, bmat_ref, cmat_ref, ar_ref, ai_ref, d_ref, o_ref,
                  lhs_ref, bu_ref, hs_ref, yp_ref, carry_ref):
    tile = SUBLANES
    lane = lax.broadcasted_iota(jnp.int32, (tile, LANES), 1)
    low = lane < HALF_CH

    @pl.when(pl.program_id(1) == 0)
    def _():
        carry_ref[...] = jnp.zeros_like(carry_ref)

    for s in range(SCAN_SLABS):
        for b in range(BATCH):
            for t0 in range(0, SCAN_T, tile):
                v = u_ref[b, t0:t0 + tile, s * LANES:(s + 1) * LANES]
                lhs_ref[s, pl.ds(t0 * tile + b, tile, stride=tile), :] = jnp.where(low, v, 0.0)
                lhs_ref[s, pl.ds(t0 * tile + BATCH + b, tile, stride=tile), :] = jnp.where(low, 0.0, v)

    for s in range(SCAN_SLABS):
        bu_ref[s] = jnp.dot(lhs_ref[s].astype(BF16), bmat_ref[s], preferred_element_type=F32)

    a_r = [ar_ref[s] for s in range(SCAN_SLABS)]
    a_i = [ai_ref[s] for s in range(SCAN_SLABS)]

    def step(t, carry):
        r0 = pl.multiple_of(t * tile, tile)
        out = []
        for s in range(SCAN_SLABS):
            hr, hi = carry[2 * s], carry[2 * s + 1]
            bur = bu_ref[s, pl.ds(r0, tile), 0:HALF_STATES]
            bui = bu_ref[s, pl.ds(r0, tile), HALF_STATES:2 * HALF_STATES]
            nhr = (a_r[s] * hr - a_i[s] * hi) + bur
            nhi = (a_r[s] * hi + a_i[s] * hr) + bui
            hs_ref[s, pl.ds(r0, tile), 0:HALF_STATES] = nhr
            hs_ref[s, pl.ds(r0, tile), HALF_STATES:2 * HALF_STATES] = nhi
            out += [nhr, nhi]
        return tuple(out)

    init = []
    for s in range(SCAN_SLABS):
        init += [carry_ref[s, :, 0:HALF_STATES], carry_ref[s, :, HALF_STATES:2 * HALF_STATES]]
    fin = lax.fori_loop(0, SCAN_T, step, tuple(init), unroll=4)
    for s in range(SCAN_SLABS):
        carry_ref[s, :, 0:HALF_STATES] = fin[2 * s]
        carry_ref[s, :, HALF_STATES:2 * HALF_STATES] = fin[2 * s + 1]

    for s in range(SCAN_SLABS):
        yp_ref[s] = jnp.dot(hs_ref[s].astype(BF16), cmat_ref[s], preferred_element_type=F32)

    for s in range(SCAN_SLABS):
        sl = slice(s * LANES, (s + 1) * LANES)
        dsk = d_ref[:, sl]
        for b in range(BATCH):
            for t0 in range(0, SCAN_T, 2 * tile):
                ys = []
                for t1 in (t0, t0 + tile):
                    y0 = yp_ref[s, pl.ds(t1 * tile + b, tile, stride=tile), :]
                    y1 = yp_ref[s, pl.ds(t1 * tile + BATCH + b, tile, stride=tile), :]
                    ys.append(jnp.where(low, y0, y1) + dsk * u_ref[b, t1:t1 + tile, sl])
                o_ref[b, t0:t0 + 2 * tile, sl] = _gelu(jnp.concatenate(ys, axis=0)).astype(BF16)


def _s5_scan(u, bmat, cmat, a_r, a_i, d):
    width = SCAN_SLABS * LANES
    rows = SCAN_T * SUBLANES
    return pl.pallas_call(
        _s5_scan_body,
        grid=(N_SLABS // SCAN_SLABS, SEQ // SCAN_T),
        in_specs=[
            pl.BlockSpec((BATCH, SCAN_T, width), lambda s, c: (0, c, s)),
            pl.BlockSpec((SCAN_SLABS, LANES, 2 * HALF_STATES), lambda s, c: (s, 0, 0)),
            pl.BlockSpec((SCAN_SLABS, 2 * HALF_STATES, LANES), lambda s, c: (s, 0, 0)),
            pl.BlockSpec((SCAN_SLABS, SUBLANES, HALF_STATES), lambda s, c: (s, 0, 0)),
            pl.BlockSpec((SCAN_SLABS, SUBLANES, HALF_STATES), lambda s, c: (s, 0, 0)),
            pl.BlockSpec((1, width), lambda s, c: (0, s)),
        ],
        out_specs=pl.BlockSpec((BATCH, SCAN_T, width), lambda s, c: (0, c, s)),
        out_shape=jax.ShapeDtypeStruct((BATCH, SEQ, D_MODEL), BF16),
        scratch_shapes=[
            pltpu.VMEM((SCAN_SLABS, rows, LANES), F32),
            pltpu.VMEM((SCAN_SLABS, rows, 2 * HALF_STATES), F32),
            pltpu.VMEM((SCAN_SLABS, rows, 2 * HALF_STATES), F32),
            pltpu.VMEM((SCAN_SLABS, rows, LANES), F32),
            pltpu.VMEM((SCAN_SLABS, SUBLANES, 2 * HALF_STATES), F32),
        ],
        compiler_params=_params(("parallel", "arbitrary")),
    )(u, bmat, cmat, a_r, a_i, d)


def _s5_scan_operands(a_r, a_i, bb_r, bb_i, c_re, c_im):
    halves = SLAB_GROUPS // 2
    eye = jnp.eye(halves, dtype=F32)

    def b_blocks(bb):
        bb = bb.reshape(SSM_GROUP_DIM, N_SLABS, 2, halves, SSM_STATE)
        blk = jnp.einsum('hsqgp,gk->sqghkp', bb, eye)
        return blk.reshape(N_SLABS, LANES, HALF_STATES)

    def c_blocks(c):
        c = c.reshape(N_SLABS, 2, halves, SSM_GROUP_DIM, SSM_STATE)
        blk = jnp.einsum('sqghp,gk->skpqgh', c, eye)
        return blk.reshape(N_SLABS, HALF_STATES, LANES)

    bmat = jnp.concatenate([b_blocks(bb_r), b_blocks(bb_i)], axis=-1).astype(BF16)
    cmat = jnp.concatenate([c_blocks(c_re), -c_blocks(c_im)], axis=1).astype(BF16)

    def a_rows(a):
        a = a.reshape(N_SLABS, 2, 1, HALF_STATES)
        return jnp.broadcast_to(a, (N_SLABS, 2, BATCH, HALF_STATES)).reshape(
            N_SLABS, SUBLANES, HALF_STATES)

    return bmat, cmat, a_rows(a_r), a_rows(a_i)


def _s5_out_body(x_ref, a_ref, wv_ref, wg_ref, o_ref):
    a = a_ref[...]
    val = jnp.dot(a, wv_ref[...], preferred_element_type=F32)
    gate = jnp.dot(a, wg_ref[...], preferred_element_type=F32)
    o_ref[...] = x_ref[...] + val * _sigmoid(gate)


def _s5_out(x, act, w, *, tm=1024, tn=512):
    nj = D_MODEL // tn
    return pl.pallas_call(
        _s5_out_body,
        grid=(TOKENS // tm, nj),
        in_specs=[
            pl.BlockSpec((tm, tn), lambda i, j: (i, j)),
            pl.BlockSpec((tm, D_MODEL), lambda i, j: (i, 0)),
            pl.BlockSpec((D_MODEL, tn), lambda i, j: (0, j)),
            pl.BlockSpec((D_MODEL, tn), lambda i, j: (0, j + nj)),
        ],
        out_specs=pl.BlockSpec((tm, tn), lambda i, j: (i, j)),
        out_shape=jax.ShapeDtypeStruct((TOKENS, D_MODEL), F32),
        compiler_params=_params(("parallel", "arbitrary")),
    )(x, act, w, w)


def _s5(x, g, w_in, lam_re, lam_im, log_dt, b_re, b_im, c_re, c_im, d_skip, w_out):
    a_r, a_i, bb_r, bb_i = _s5_disc(lam_re, lam_im, log_dt, b_re, b_im)
    bmat, cmat, ar_rows, ai_rows = _s5_scan_operands(a_r, a_i, bb_r, bb_i, c_re, c_im)
    u = _s5_in(x, g, w_in).reshape(BATCH, SEQ, D_MODEL)
    act = _s5_scan(u, bmat, cmat, ar_rows, ai_rows, d_skip.reshape(1, D_MODEL))
    return _s5_out(x, act.reshape(TOKENS, D_MODEL), w_out)


def _ple_body(x_ref, g_ref, wg_ref, p_ref, wp_ref, fg_ref, o_ref, *, final):
    x = x_ref[...]
    h = _rms(x, g_ref[...]).astype(BF16)
    gate = _sigmoid(jnp.dot(h, wg_ref[...], preferred_element_type=F32))
    proj = jnp.dot(p_ref[...].astype(BF16), wp_ref[...], preferred_element_type=F32)
    y = x + gate * proj
    o_ref[...] = _rms(y, fg_ref[...]) if final else y


def _ple(x, g, wg, p, wp, fg, *, final, tm=512):
    return pl.pallas_call(
        functools.partial(_ple_body, final=final),
        grid=(TOKENS // tm,),
        in_specs=[
            pl.BlockSpec((tm, D_MODEL), lambda i: (i, 0)),
            pl.BlockSpec((1, D_MODEL), lambda i: (0, 0)),
            pl.BlockSpec((D_MODEL, D_MODEL), lambda i: (0, 0)),
            pl.BlockSpec((tm, PLE_DIM), lambda i: (i, 0)),
            pl.BlockSpec((PLE_DIM, D_MODEL), lambda i: (0, 0)),
            pl.BlockSpec((1, D_MODEL), lambda i: (0, 0)),
        ],
        out_specs=pl.BlockSpec((tm, D_MODEL), lambda i: (i, 0)),
        out_shape=jax.ShapeDtypeStruct((TOKENS, D_MODEL), F32),
        compiler_params=_params(("parallel",)),
    )(x, g, wg, p, wp, fg)


def kernel(x, p, norm_g, final_norm_g, ffn_w_gate, ffn_w_up, ffn_w_down, gmlp_w_in, gmlp_ln_g, gmlp_ln_b, gmlp_w_s, gmlp_b_s, gmlp_w_out, s5_w_in, s5_lam_re, s5_lam_im, s5_log_dt, s5_b_re, s5_b_im, s5_c_re, s5_c_im, s5_d, s5_w_out, ple_w_gate, ple_w_proj):
    bf = lambda w: w.astype(BF16)
    x = x.reshape(TOKENS, D_MODEL)
    p = p.reshape(DEPTH, TOKENS, PLE_DIM)
    norm_g = norm_g.reshape(DEPTH, 4, 1, D_MODEL)
    fg = final_norm_g.reshape(1, D_MODEL)
    for i in range(DEPTH):
        x = _ffn(x, norm_g[i, 0], bf(ffn_w_gate[i, 0]), bf(ffn_w_up[i, 0]), bf(ffn_w_down[i, 0]))
        j = i // 2
        if i % 2 == 0:
            x = _gmlp(x, norm_g[i, 1], bf(gmlp_w_in[j]), gmlp_ln_g[j].reshape(1, GMLP_HALF),
                      gmlp_ln_b[j].reshape(1, GMLP_HALF), gmlp_w_s[j],
                      gmlp_b_s[j].reshape(GMLP_GROUPS, CHUNK, 1), bf(gmlp_w_out[j]))
        else:
            x = _s5(x, norm_g[i, 1], bf(s5_w_in[j]), s5_lam_re[j], s5_lam_im[j], s5_log_dt[j],
                    s5_b_re[j], s5_b_im[j], s5_c_re[j], s5_c_im[j], s5_d[j], bf(s5_w_out[j]))
        x = _ffn(x, norm_g[i, 2], bf(ffn_w_gate[i, 1]), bf(ffn_w_up[i, 1]), bf(ffn_w_down[i, 1]))
        x = _ple(x, norm_g[i, 3], bf(ple_w_gate[i]), p[i], bf(ple_w_proj[i]), fg,
                 final=(i == DEPTH - 1))
    return x.reshape(BATCH, SEQ, D_MODEL)
```

```python
import functools
import math

import jax
import jax.numpy as jnp
from jax import lax
from jax.experimental import pallas as pl
from jax.experimental.pallas import tpu as pltpu

F32 = jnp.float32
BF16 = jnp.bfloat16

D_MODEL = 2048
BATCH = 4
SEQ = 2048
TOKENS = BATCH * SEQ
DEPTH = 2
D_FF = 5632
PLE_DIM = 256
CHUNK = 128
GMLP_HALF = 3 * D_MODEL
GMLP_GROUPS = 16
GMLP_GROUP_DIM = GMLP_HALF // GMLP_GROUPS
SSM_GROUP_DIM = 16
SSM_GROUPS = D_MODEL // SSM_GROUP_DIM
SSM_STATE = 64
EPS = 1e-6

LANES = 128
SUBLANES = 8
VMEM_LIMIT = 60 * 1024 * 1024

SLAB_GROUPS = LANES // SSM_GROUP_DIM
N_SLABS = D_MODEL // LANES
SLAB_STATES = SLAB_GROUPS * SSM_STATE
HALF_STATES = SLAB_STATES // 2
HALF_CH = LANES // 2
SCAN_SLABS = 2
SCAN_T = 128


def _rms(x, g):
    ms = jnp.mean(x * x, axis=-1, keepdims=True)
    return x * lax.rsqrt(ms + EPS) * g


def _gelu(x):
    c = math.sqrt(2.0 / math.pi)
    return x * (0.5 * (1.0 + jnp.tanh(c * (x + 0.044715 * (x * x * x)))))


def _sigmoid(x):
    return 1.0 / (1.0 + jnp.exp(-x))


def _params(sem):
    return pltpu.CompilerParams(dimension_semantics=sem, vmem_limit_bytes=VMEM_LIMIT)


def _gain_spec(layer, slot):
    return pl.BlockSpec((None, None, 1, D_MODEL), lambda i, j: (layer, slot, 0, 0))


def _ffn_body(x_ref, g_ref, wg_ref, wu_ref, wd_ref, o_ref, h_ref):
    @pl.when(pl.program_id(1) == 0)
    def _():
        x = x_ref[...]
        h_ref[...] = _rms(x, g_ref[...]).astype(BF16)
        o_ref[...] = x

    h = h_ref[...]
    gate = jnp.dot(h, wg_ref[...].astype(BF16), preferred_element_type=F32)
    up = jnp.dot(h, wu_ref[...].astype(BF16), preferred_element_type=F32)
    a = (gate * _sigmoid(gate)) * (0.5 * up)
    o_ref[...] += jnp.dot(a.astype(BF16), wd_ref[...].astype(BF16), preferred_element_type=F32)


def _ffn(x, norm_g, wg, wu, wd, layer, which, *, tm=1024, tf=256):
    return pl.pallas_call(
        _ffn_body,
        grid=(TOKENS // tm, D_FF // tf),
        in_specs=[
            pl.BlockSpec((tm, D_MODEL), lambda i, j: (i, 0)),
            _gain_spec(layer, 2 * which),
            pl.BlockSpec((None, None, D_MODEL, tf), lambda i, j: (layer, which, 0, j)),
            pl.BlockSpec((None, None, D_MODEL, tf), lambda i, j: (layer, which, 0, j)),
            pl.BlockSpec((None, None, tf, D_MODEL), lambda i, j: (layer, which, j, 0)),
        ],
        out_specs=pl.BlockSpec((tm, D_MODEL), lambda i, j: (i, 0)),
        out_shape=jax.ShapeDtypeStruct((TOKENS, D_MODEL), F32),
        scratch_shapes=[pltpu.VMEM((tm, D_MODEL), BF16)],
        compiler_params=_params(("parallel", "arbitrary")),
        name="ffn",
    )(x, norm_g, wg, wu, wd)


GMLP_STEP_GROUPS = 2
GMLP_STEP_W = GMLP_STEP_GROUPS * GMLP_GROUP_DIM


def _gmlp_body(x_ref, g_ref, wu_ref, wv_ref, lng_ref, lnb_ref, ws_ref, bs_ref, wo_ref,
               o_ref, h_ref, *, tm):
    @pl.when(pl.program_id(1) == 0)
    def _():
        x = x_ref[...]
        h_ref[...] = _rms(x, g_ref[...]).astype(BF16)
        o_ref[...] = x

    h = h_ref[...]
    zu = _gelu(jnp.dot(h, wu_ref[...], preferred_element_type=F32))
    zv = _gelu(jnp.dot(h, wv_ref[...], preferred_element_type=F32))
    row = lax.broadcasted_iota(jnp.int32, (CHUNK, CHUNK), 0)
    col = lax.broadcasted_iota(jnp.int32, (CHUNK, CHUNK), 1)
    parts = []
    for gi in range(GMLP_STEP_GROUPS):
        lo, hi = gi * GMLP_GROUP_DIM, (gi + 1) * GMLP_GROUP_DIM
        v = zv[:, lo:hi]
        d = v - jnp.mean(v, axis=-1, keepdims=True)
        var = jnp.mean(d * d, axis=-1, keepdims=True)
        vn = ((d * lax.rsqrt(var + EPS)) * lng_ref[:, lo:hi] + lnb_ref[:, lo:hi]).astype(BF16)
        ws = jnp.where(row >= col, ws_ref[gi], 0.0).astype(BF16)
        bias = bs_ref[gi]
        sv = [jnp.dot(ws, vn[c * CHUNK:(c + 1) * CHUNK], preferred_element_type=F32) + bias
              for c in range(tm // CHUNK)]
        parts.append(zu[:, lo:hi] * jnp.concatenate(sv, axis=0))
    gated = jnp.concatenate(parts, axis=-1).astype(BF16)
    o_ref[...] += jnp.dot(gated, wo_ref[...], preferred_element_type=F32)


def _gmlp(x, norm_g, w_in, ln_g, ln_b, w_s, b_s, w_out, layer, *, tm=512):
    nj = GMLP_GROUPS // GMLP_STEP_GROUPS
    return pl.pallas_call(
        functools.partial(_gmlp_body, tm=tm),
        grid=(TOKENS // tm, nj),
        in_specs=[
            pl.BlockSpec((tm, D_MODEL), lambda i, j: (i, 0)),
            _gain_spec(layer, 1),
            pl.BlockSpec((D_MODEL, GMLP_STEP_W), lambda i, j: (0, j)),
            pl.BlockSpec((D_MODEL, GMLP_STEP_W), lambda i, j: (0, j + nj)),
            pl.BlockSpec((1, GMLP_STEP_W), lambda i, j: (0, j)),
            pl.BlockSpec((1, GMLP_STEP_W), lambda i, j: (0, j)),
            pl.BlockSpec((GMLP_STEP_GROUPS, CHUNK, CHUNK), lambda i, j: (j, 0, 0)),
            pl.BlockSpec((GMLP_STEP_GROUPS, CHUNK, 1), lambda i, j: (j, 0, 0)),
            pl.BlockSpec((GMLP_STEP_W, D_MODEL), lambda i, j: (j, 0)),
        ],
        out_specs=pl.BlockSpec((tm, D_MODEL), lambda i, j: (i, 0)),
        out_shape=jax.ShapeDtypeStruct((TOKENS, D_MODEL), F32),
        scratch_shapes=[pltpu.VMEM((tm, D_MODEL), BF16)],
        compiler_params=_params(("parallel", "arbitrary")),
        name="gmlp",
    )(x, norm_g, w_in, w_in, ln_g, ln_b, w_s, b_s, w_out)


def _s5_disc_body(lr_ref, li_ref, ldt_ref, br_ref, bi_ref, ar_ref, ai_ref, bbr_ref, bbi_ref):
    lr, li = lr_ref[...], li_ref[...]
    dt = jnp.exp(ldt_ref[...])
    mag = jnp.exp(lr * dt)
    ang = li * dt
    a_r = mag * jnp.cos(ang)
    a_i = mag * jnp.sin(ang)
    den = lr * lr + li * li
    nr = a_r - 1.0
    z_r = (nr * lr + a_i * li) / den
    z_i = (a_i * lr - nr * li) / den
    ar_ref[...] = a_r
    ai_ref[...] = a_i
    bbr_ref[...] = z_r * br_ref[...] - z_i * bi_ref[...]
    bbi_ref[...] = z_r * bi_ref[...] + z_i * br_ref[...]


def _s5_disc(lam_re, lam_im, log_dt, b_re, b_im):
    n = SSM_GROUPS * SSM_STATE
    row = jax.ShapeDtypeStruct((1, n), F32)
    mat = jax.ShapeDtypeStruct((SSM_GROUP_DIM, n), F32)
    to_mat = lambda b: jnp.transpose(b, (2, 0, 1)).reshape(SSM_GROUP_DIM, n)
    ldt = jnp.broadcast_to(log_dt[:, None], (SSM_GROUPS, SSM_STATE)).reshape(1, n)
    return pl.pallas_call(_s5_disc_body, out_shape=(row, row, mat, mat), name="s5_disc")(
        lam_re.reshape(1, n), lam_im.reshape(1, n), ldt, to_mat(b_re), to_mat(b_im))


def _s5_in_body(x_ref, g_ref, w_ref, o_ref, h_ref):
    @pl.when(pl.program_id(1) == 0)
    def _():
        h_ref[...] = _rms(x_ref[...], g_ref[...]).astype(BF16)

    o_ref[...] = jnp.dot(h_ref[...], w_ref[...].astype(BF16), preferred_element_type=F32)


def _s5_in(x, norm_g, w, layer, *, tm=1024, tn=512):
    return pl.pallas_call(
        _s5_in_body,
        grid=(TOKENS // tm, D_MODEL // tn),
        in_specs=[
            pl.BlockSpec((tm, D_MODEL), lambda i, j: (i, 0)),
            _gain_spec(layer, 1),
            pl.BlockSpec((D_MODEL, tn), lambda i, j: (0, j)),
        ],
        out_specs=pl.BlockSpec((tm, tn), lambda i, j: (i, j)),
        out_shape=jax.ShapeDtypeStruct((TOKENS, D_MODEL), F32),
        scratch_shapes=[pltpu.VMEM((tm, D_MODEL), BF16)],
        compiler_params=_params(("parallel", "arbitrary")),
        name="s5_in",
    )(x, norm_g, w)


def _s5_scan_body(u_ref, bmat_ref, cmat_ref, ar_ref, ai_ref, d_ref, o_ref,
                  lhs_ref, bu_ref, hs_ref, yp_ref, carry_ref):
    tile = SUBLANES
    lane = lax.broadcasted_iota(jnp.int32, (tile, LANES), 1)
    low = lane < HALF_CH

    @pl.when(pl.program_id(1) == 0)
    def _():
        carry_ref[...] = jnp.zeros_like(carry_ref)

    for s in range(SCAN_SLABS):
        for b in range(BATCH):
            for t0 in range(0, SCAN_T, tile):
                v = u_ref[b, t0:t0 + tile, s * LANES:(s + 1) * LANES]
                lhs_ref[s, pl.ds(t0 * tile + b, tile, stride=tile), :] = jnp.where(low, v, 0.0)
                lhs_ref[s, pl.ds(t0 * tile + BATCH + b, tile, stride=tile), :] = jnp.where(low, 0.0, v)

    for s in range(SCAN_SLABS):
        bu_ref[s] = jnp.dot(lhs_ref[s].astype(BF16), bmat_ref[s], preferred_element_type=F32)

    a_r = [ar_ref[s] for s in range(SCAN_SLABS)]
    a_i = [ai_ref[s] for s in range(SCAN_SLABS)]

    def step(t, carry):
        r0 = pl.multiple_of(t * tile, tile)
        out = []
        for s in range(SCAN_SLABS):
            hr, hi = carry[2 * s], carry[2 * s + 1]
            bur = bu_ref[s, pl.ds(r0, tile), 0:HALF_STATES]
            bui = bu_ref[s, pl.ds(r0, tile), HALF_STATES:2 * HALF_STATES]
            nhr = (a_r[s] * hr - a_i[s] * hi) + bur
            nhi = (a_r[s] * hi + a_i[s] * hr) + bui
            hs_ref[s, pl.ds(r0, tile), 0:HALF_STATES] = nhr
            hs_ref[s, pl.ds(r0, tile), HALF_STATES:2 * HALF_STATES] = nhi
            out += [nhr, nhi]
        return tuple(out)

    init = []
    for s in range(SCAN_SLABS):
        init += [carry_ref[s, :, 0:HALF_STATES], carry_ref[s, :, HALF_STATES:2 * HALF_STATES]]
    fin = lax.fori_loop(0, SCAN_T, step, tuple(init), unroll=4)
    for s in range(SCAN_SLABS):
        carry_ref[s, :, 0:HALF_STATES] = fin[2 * s]
        carry_ref[s, :, HALF_STATES:2 * HALF_STATES] = fin[2 * s + 1]

    for s in range(SCAN_SLABS):
        yp_ref[s] = jnp.dot(hs_ref[s].astype(BF16), cmat_ref[s], preferred_element_type=F32)

    for s in range(SCAN_SLABS):
        sl = slice(s * LANES, (s + 1) * LANES)
        dsk = d_ref[:, sl]
        for b in range(BATCH):
            for t0 in range(0, SCAN_T, 2 * tile):
                ys = []
                for t1 in (t0, t0 + tile):
                    y0 = yp_ref[s, pl.ds(t1 * tile + b, tile, stride=tile), :]
                    y1 = yp_ref[s, pl.ds(t1 * tile + BATCH + b, tile, stride=tile), :]
                    ys.append(jnp.where(low, y0, y1) + dsk * u_ref[b, t1:t1 + tile, sl])
                o_ref[b, t0:t0 + 2 * tile, sl] = _gelu(jnp.concatenate(ys, axis=0)).astype(BF16)


def _s5_scan(u, bmat, cmat, a_r, a_i, d):
    width = SCAN_SLABS * LANES
    rows = SCAN_T * SUBLANES
    return pl.pallas_call(
        _s5_scan_body,
        grid=(N_SLABS // SCAN_SLABS, SEQ // SCAN_T),
        in_specs=[
            pl.BlockSpec((BATCH, SCAN_T, width), lambda s, c: (0, c, s)),
            pl.BlockSpec((SCAN_SLABS, LANES, 2 * HALF_STATES), lambda s, c: (s, 0, 0)),
            pl.BlockSpec((SCAN_SLABS, 2 * HALF_STATES, LANES), lambda s, c: (s, 0, 0)),
            pl.BlockSpec((SCAN_SLABS, SUBLANES, HALF_STATES), lambda s, c: (s, 0, 0)),
            pl.BlockSpec((SCAN_SLABS, SUBLANES, HALF_STATES), lambda s, c: (s, 0, 0)),
            pl.BlockSpec((1, width), lambda s, c: (0, s)),
        ],
        out_specs=pl.BlockSpec((BATCH, SCAN_T, width), lambda s, c: (0, c, s)),
        out_shape=jax.ShapeDtypeStruct((BATCH, SEQ, D_MODEL), BF16),
        scratch_shapes=[
            pltpu.VMEM((SCAN_SLABS, rows, LANES), F32),
            pltpu.VMEM((SCAN_SLABS, rows, 2 * HALF_STATES), F32),
            pltpu.VMEM((SCAN_SLABS, rows, 2 * HALF_STATES), F32),
            pltpu.VMEM((SCAN_SLABS, rows, LANES), F32),
            pltpu.VMEM((SCAN_SLABS, SUBLANES, 2 * HALF_STATES), F32),
        ],
        compiler_params=_params(("parallel", "arbitrary")),
        name="s5_scan",
    )(u, bmat, cmat, a_r, a_i, d)


def _s5_scan_operands(a_r, a_i, bb_r, bb_i, c_re, c_im):
    halves = SLAB_GROUPS // 2
    eye = jnp.eye(halves, dtype=F32)

    def b_blocks(bb):
        bb = bb.reshape(SSM_GROUP_DIM, N_SLABS, 2, halves, SSM_STATE)
        blk = jnp.einsum('hsqgp,gk->sqghkp', bb, eye)
        return blk.reshape(N_SLABS, LANES, HALF_STATES)

    def c_blocks(c):
        c = c.reshape(N_SLABS, 2, halves, SSM_GROUP_DIM, SSM_STATE)
        blk = jnp.einsum('sqghp,gk->skpqgh', c, eye)
        return blk.reshape(N_SLABS, HALF_STATES, LANES)

    bmat = jnp.concatenate([b_blocks(bb_r), b_blocks(bb_i)], axis=-1).astype(BF16)
    cmat = jnp.concatenate([c_blocks(c_re), -c_blocks(c_im)], axis=1).astype(BF16)

    def a_rows(a):
        a = a.reshape(N_SLABS, 2, 1, HALF_STATES)
        return jnp.broadcast_to(a, (N_SLABS, 2, BATCH, HALF_STATES)).reshape(
            N_SLABS, SUBLANES, HALF_STATES)

    return bmat, cmat, a_rows(a_r), a_rows(a_i)


def _s5_out_body(x_ref, a_ref, wv_ref, wg_ref, o_ref):
    a = a_ref[...]
    val = jnp.dot(a, wv_ref[...].astype(BF16), preferred_element_type=F32)
    gate = jnp.dot(a, wg_ref[...].astype(BF16), preferred_element_type=F32)
    o_ref[...] = x_ref[...] + val * _sigmoid(gate)


def _s5_out(x, act, w, *, tm=1024, tn=512):
    nj = D_MODEL // tn
    return pl.pallas_call(
        _s5_out_body,
        grid=(TOKENS // tm, nj),
        in_specs=[
            pl.BlockSpec((tm, tn), lambda i, j: (i, j)),
            pl.BlockSpec((tm, D_MODEL), lambda i, j: (i, 0)),
            pl.BlockSpec((D_MODEL, tn), lambda i, j: (0, j)),
            pl.BlockSpec((D_MODEL, tn), lambda i, j: (0, j + nj)),
        ],
        out_specs=pl.BlockSpec((tm, tn), lambda i, j: (i, j)),
        out_shape=jax.ShapeDtypeStruct((TOKENS, D_MODEL), F32),
        compiler_params=_params(("parallel", "arbitrary")),
        name="s5_out",
    )(x, act, w, w)


def _s5(x, norm_g, layer, w_in, lam_re, lam_im, log_dt, b_re, b_im, c_re, c_im, d_skip, w_out):
    a_r, a_i, bb_r, bb_i = _s5_disc(lam_re, lam_im, log_dt, b_re, b_im)
    bmat, cmat, ar_rows, ai_rows = _s5_scan_operands(a_r, a_i, bb_r, bb_i, c_re, c_im)
    u = _s5_in(x, norm_g, w_in, layer).reshape(BATCH, SEQ, D_MODEL)
    act = _s5_scan(u, bmat, cmat, ar_rows, ai_rows, d_skip.reshape(1, D_MODEL))
    return _s5_out(x, act.reshape(TOKENS, D_MODEL), w_out)


def _ple_body(x_ref, g_ref, wg_ref, p_ref, wp_ref, fg_ref, o_ref, h_ref, *, final, tn):
    j = pl.program_id(1)

    @pl.when(j == 0)
    def _():
        h_ref[...] = _rms(x_ref[...], g_ref[...]).astype(BF16)

    cols = pl.ds(pl.multiple_of(j * tn, tn), tn)
    gate = _sigmoid(jnp.dot(h_ref[...], wg_ref[...].astype(BF16), preferred_element_type=F32))
    proj = jnp.dot(p_ref[...].astype(BF16), wp_ref[...].astype(BF16),
                   preferred_element_type=F32)
    o_ref[:, cols] = x_ref[:, cols] + gate * proj
    if final:
        @pl.when(j == pl.num_programs(1) - 1)
        def _():
            o_ref[...] = _rms(o_ref[...], fg_ref[...])


def _ple(x, norm_g, wg, p, wp, fg, layer, *, final, tm=1024, tn=512):
    return pl.pallas_call(
        functools.partial(_ple_body, final=final, tn=tn),
        grid=(TOKENS // tm, D_MODEL // tn),
        in_specs=[
            pl.BlockSpec((tm, D_MODEL), lambda i, j: (i, 0)),
            _gain_spec(layer, 3),
            pl.BlockSpec((None, D_MODEL, tn), lambda i, j: (layer, 0, j)),
            pl.BlockSpec((None, tm, PLE_DIM), lambda i, j: (layer, i, 0)),
            pl.BlockSpec((None, PLE_DIM, tn), lambda i, j: (layer, 0, j)),
            pl.BlockSpec((1, D_MODEL), lambda i, j: (0, 0)),
        ],
        out_specs=pl.BlockSpec((tm, D_MODEL), lambda i, j: (i, 0)),
        out_shape=jax.ShapeDtypeStruct((TOKENS, D_MODEL), F32),
        scratch_shapes=[pltpu.VMEM((tm, D_MODEL), BF16)],
        compiler_params=_params(("parallel", "arbitrary")),
        name="ple",
    )(x, norm_g, wg, p, wp, fg)


def kernel(x, p, norm_g, final_norm_g, ffn_w_gate, ffn_w_up, ffn_w_down, gmlp_w_in, gmlp_ln_g, gmlp_ln_b, gmlp_w_s, gmlp_b_s, gmlp_w_out, s5_w_in, s5_lam_re, s5_lam_im, s5_log_dt, s5_b_re, s5_b_im, s5_c_re, s5_c_im, s5_d, s5_w_out, ple_w_gate, ple_w_proj):
    bf = lambda w: w.astype(BF16)
    x = x.reshape(TOKENS, D_MODEL)
    p = p.reshape(DEPTH, TOKENS, PLE_DIM)
    norm_g = norm_g.reshape(DEPTH, 4, 1, D_MODEL)
    fg = final_norm_g.reshape(1, D_MODEL)
    for i in range(DEPTH):
        x = _ffn(x, norm_g, ffn_w_gate, ffn_w_up, ffn_w_down, i, 0)
        j = i // 2
        if i % 2 == 0:
            x = _gmlp(x, norm_g, bf(gmlp_w_in[j]), gmlp_ln_g[j].reshape(1, GMLP_HALF),
                      gmlp_ln_b[j].reshape(1, GMLP_HALF), gmlp_w_s[j],
                      gmlp_b_s[j].reshape(GMLP_GROUPS, CHUNK, 1), bf(gmlp_w_out[j]), i)
        else:
            x = _s5(x, norm_g, i, s5_w_in[j], s5_lam_re[j], s5_lam_im[j], s5_log_dt[j],
                    s5_b_re[j], s5_b_im[j], s5_c_re[j], s5_c_im[j], s5_d[j], s5_w_out[j])
        x = _ffn(x, norm_g, ffn_w_gate, ffn_w_up, ffn_w_down, i, 1)
        x = _ple(x, norm_g, ple_w_gate, p, ple_w_proj, fg, i, final=(i == DEPTH - 1))
    return x.reshape(BATCH, SEQ, D_MODEL)
```

```python
import functools
import math

import jax
import jax.numpy as jnp
from jax import lax
from jax.experimental import pallas as pl
from jax.experimental.pallas import tpu as pltpu

F32 = jnp.float32
BF16 = jnp.bfloat16

D_MODEL = 2048
BATCH = 4
SEQ = 2048
TOKENS = BATCH * SEQ
DEPTH = 2
D_FF = 5632
PLE_DIM = 256
CHUNK = 128
GMLP_HALF = 3 * D_MODEL
GMLP_GROUPS = 16
GMLP_GROUP_DIM = GMLP_HALF // GMLP_GROUPS
SSM_GROUP_DIM = 16
SSM_GROUPS = D_MODEL // SSM_GROUP_DIM
SSM_STATE = 64
EPS = 1e-6

LANES = 128
SUBLANES = 8
VMEM_LIMIT = 60 * 1024 * 1024

SLAB_GROUPS = LANES // SSM_GROUP_DIM
N_SLABS = D_MODEL // LANES
SLAB_STATES = SLAB_GROUPS * SSM_STATE
HALF_STATES = SLAB_STATES // 2
HALF_CH = LANES // 2
SCAN_SLABS = 2
SCAN_T = 128


def _rms(x, g):
    ms = jnp.mean(x * x, axis=-1, keepdims=True)
    return x * lax.rsqrt(ms + EPS) * g


def _gelu(x):
    c = math.sqrt(2.0 / math.pi)
    return x * (0.5 * (1.0 + jnp.tanh(c * (x + 0.044715 * (x * x * x)))))


def _sigmoid(x):
    return 1.0 / (1.0 + jnp.exp(-x))


def _params(sem):
    return pltpu.CompilerParams(dimension_semantics=sem, vmem_limit_bytes=VMEM_LIMIT)


def _gain_spec(layer, slot):
    return pl.BlockSpec((None, None, 1, D_MODEL), lambda i, j: (layer, slot, 0, 0))


def _ffn_body(x_ref, g_ref, wg_ref, wu_ref, wd_ref, o_ref, h_ref):
    @pl.when(pl.program_id(1) == 0)
    def _():
        x = x_ref[...]
        h_ref[...] = _rms(x, g_ref[...]).astype(BF16)
        o_ref[...] = x

    h = h_ref[...]
    gate = jnp.dot(h, wg_ref[...].astype(BF16), preferred_element_type=F32)
    up = jnp.dot(h, wu_ref[...].astype(BF16), preferred_element_type=F32)
    a = (gate * _sigmoid(gate)) * (0.5 * up)
    o_ref[...] += jnp.dot(a.astype(BF16), wd_ref[...].astype(BF16), preferred_element_type=F32)


def _ffn(x, norm_g, wg, wu, wd, layer, which, *, tm=1024, tf=512):
    return pl.pallas_call(
        _ffn_body,
        grid=(TOKENS // tm, D_FF // tf),
        in_specs=[
            pl.BlockSpec((tm, D_MODEL), lambda i, j: (i, 0), pipeline_mode=pl.Buffered(1)),
            _gain_spec(layer, 2 * which),
            pl.BlockSpec((None, None, D_MODEL, tf), lambda i, j: (layer, which, 0, j)),
            pl.BlockSpec((None, None, D_MODEL, tf), lambda i, j: (layer, which, 0, j)),
            pl.BlockSpec((None, None, tf, D_MODEL), lambda i, j: (layer, which, j, 0)),
        ],
        out_specs=pl.BlockSpec((tm, D_MODEL), lambda i, j: (i, 0)),
        out_shape=jax.ShapeDtypeStruct((TOKENS, D_MODEL), F32),
        scratch_shapes=[pltpu.VMEM((tm, D_MODEL), BF16)],
        compiler_params=_params(("parallel", "arbitrary")),
        name="ffn",
    )(x, norm_g, wg, wu, wd)


GMLP_STEP_GROUPS = 2
GMLP_STEP_W = GMLP_STEP_GROUPS * GMLP_GROUP_DIM


def _gmlp_body(x_ref, g_ref, wu_ref, wv_ref, lng_ref, lnb_ref, ws_ref, bs_ref, wo_ref,
               o_ref, h_ref, *, tm):
    @pl.when(pl.program_id(1) == 0)
    def _():
        x = x_ref[...]
        h_ref[...] = _rms(x, g_ref[...]).astype(BF16)
        o_ref[...] = x

    h = h_ref[...]
    zu = _gelu(jnp.dot(h, wu_ref[...], preferred_element_type=F32))
    zv = _gelu(jnp.dot(h, wv_ref[...], preferred_element_type=F32))
    row = lax.broadcasted_iota(jnp.int32, (CHUNK, CHUNK), 0)
    col = lax.broadcasted_iota(jnp.int32, (CHUNK, CHUNK), 1)
    parts = []
    for gi in range(GMLP_STEP_GROUPS):
        lo, hi = gi * GMLP_GROUP_DIM, (gi + 1) * GMLP_GROUP_DIM
        v = zv[:, lo:hi]
        d = v - jnp.mean(v, axis=-1, keepdims=True)
        var = jnp.mean(d * d, axis=-1, keepdims=True)
        vn = ((d * lax.rsqrt(var + EPS)) * lng_ref[:, lo:hi] + lnb_ref[:, lo:hi]).astype(BF16)
        ws = jnp.where(row >= col, ws_ref[gi], 0.0).astype(BF16)
        bias = bs_ref[gi]
        sv = [jnp.dot(ws, vn[c * CHUNK:(c + 1) * CHUNK], preferred_element_type=F32) + bias
              for c in range(tm // CHUNK)]
        parts.append(zu[:, lo:hi] * jnp.concatenate(sv, axis=0))
    gated = jnp.concatenate(parts, axis=-1).astype(BF16)
    o_ref[...] += jnp.dot(gated, wo_ref[...], preferred_element_type=F32)


def _gmlp(x, norm_g, w_in, ln_g, ln_b, w_s, b_s, w_out, layer, *, tm=512):
    nj = GMLP_GROUPS // GMLP_STEP_GROUPS
    return pl.pallas_call(
        functools.partial(_gmlp_body, tm=tm),
        grid=(TOKENS // tm, nj),
        in_specs=[
            pl.BlockSpec((tm, D_MODEL), lambda i, j: (i, 0)),
            _gain_spec(layer, 1),
            pl.BlockSpec((D_MODEL, GMLP_STEP_W), lambda i, j: (0, j)),
            pl.BlockSpec((D_MODEL, GMLP_STEP_W), lambda i, j: (0, j + nj)),
            pl.BlockSpec((1, GMLP_STEP_W), lambda i, j: (0, j)),
            pl.BlockSpec((1, GMLP_STEP_W), lambda i, j: (0, j)),
            pl.BlockSpec((GMLP_STEP_GROUPS, CHUNK, CHUNK), lambda i, j: (j, 0, 0)),
            pl.BlockSpec((GMLP_STEP_GROUPS, CHUNK, 1), lambda i, j: (j, 0, 0)),
            pl.BlockSpec((GMLP_STEP_W, D_MODEL), lambda i, j: (j, 0)),
        ],
        out_specs=pl.BlockSpec((tm, D_MODEL), lambda i, j: (i, 0)),
        out_shape=jax.ShapeDtypeStruct((TOKENS, D_MODEL), F32),
        scratch_shapes=[pltpu.VMEM((tm, D_MODEL), BF16)],
        compiler_params=_params(("parallel", "arbitrary")),
        name="gmlp",
    )(x, norm_g, w_in, w_in, ln_g, ln_b, w_s, b_s, w_out)


def _s5_disc_body(lr_ref, li_ref, ldt_ref, br_ref, bi_ref, ar_ref, ai_ref, bbr_ref, bbi_ref):
    lr, li = lr_ref[...], li_ref[...]
    dt = jnp.exp(ldt_ref[...])
    mag = jnp.exp(lr * dt)
    ang = li * dt
    a_r = mag * jnp.cos(ang)
    a_i = mag * jnp.sin(ang)
    den = lr * lr + li * li
    nr = a_r - 1.0
    z_r = (nr * lr + a_i * li) / den
    z_i = (a_i * lr - nr * li) / den
    ar_ref[...] = a_r
    ai_ref[...] = a_i
    bbr_ref[...] = z_r * br_ref[...] - z_i * bi_ref[...]
    bbi_ref[...] = z_r * bi_ref[...] + z_i * br_ref[...]


def _s5_disc(lam_re, lam_im, log_dt, b_re, b_im):
    n = SSM_GROUPS * SSM_STATE
    row = jax.ShapeDtypeStruct((1, n), F32)
    mat = jax.ShapeDtypeStruct((SSM_GROUP_DIM, n), F32)
    to_mat = lambda b: jnp.transpose(b, (2, 0, 1)).reshape(SSM_GROUP_DIM, n)
    ldt = jnp.broadcast_to(log_dt[:, None], (SSM_GROUPS, SSM_STATE)).reshape(1, n)
    return pl.pallas_call(_s5_disc_body, out_shape=(row, row, mat, mat), name="s5_disc")(
        lam_re.reshape(1, n), lam_im.reshape(1, n), ldt, to_mat(b_re), to_mat(b_im))


def _resident(shape):
    zeros = (0,) * len(shape)
    return pl.BlockSpec(shape, lambda i: zeros, pipeline_mode=pl.Buffered(1))


def _s5_in_body(x_ref, g_ref, w_ref, o_ref, wb_ref):
    @pl.when(pl.program_id(0) == 0)
    def _():
        wb_ref[...] = w_ref[...].astype(BF16)

    h = _rms(x_ref[...], g_ref[...]).astype(BF16)
    o_ref[...] = jnp.dot(h, wb_ref[...], preferred_element_type=F32)


def _s5_in(x, norm_g, w, layer, *, tm=512):
    return pl.pallas_call(
        _s5_in_body,
        grid=(TOKENS // tm,),
        in_specs=[
            pl.BlockSpec((tm, D_MODEL), lambda i: (i, 0)),
            pl.BlockSpec((None, None, 1, D_MODEL), lambda i: (layer, 1, 0, 0)),
            _resident((D_MODEL, D_MODEL)),
        ],
        out_specs=pl.BlockSpec((tm, D_MODEL), lambda i: (i, 0)),
        out_shape=jax.ShapeDtypeStruct((TOKENS, D_MODEL), F32),
        scratch_shapes=[pltpu.VMEM((D_MODEL, D_MODEL), BF16)],
        compiler_params=_params(("arbitrary",)),
        name="s5_in",
    )(x, norm_g, w)


def _s5_scan_body(u_ref, bmat_ref, cmat_ref, ar_ref, ai_ref, d_ref, o_ref,
                  lhs_ref, bu_ref, hs_ref, yp_ref, carry_ref):
    tile = SUBLANES
    lane = lax.broadcasted_iota(jnp.int32, (tile, LANES), 1)
    low = lane < HALF_CH

    @pl.when(pl.program_id(1) == 0)
    def _():
        carry_ref[...] = jnp.zeros_like(carry_ref)

    for s in range(SCAN_SLABS):
        for b in range(BATCH):
            for t0 in range(0, SCAN_T, tile):
                v = u_ref[b, t0:t0 + tile, s * LANES:(s + 1) * LANES]
                lhs_ref[s, pl.ds(t0 * tile + b, tile, stride=tile), :] = jnp.where(low, v, 0.0)
                lhs_ref[s, pl.ds(t0 * tile + BATCH + b, tile, stride=tile), :] = jnp.where(low, 0.0, v)

    for s in range(SCAN_SLABS):
        bu_ref[s] = jnp.dot(lhs_ref[s].astype(BF16), bmat_ref[s], preferred_element_type=F32)

    a_r = [ar_ref[s] for s in range(SCAN_SLABS)]
    a_i = [ai_ref[s] for s in range(SCAN_SLABS)]

    def step(t, carry):
        r0 = pl.multiple_of(t * tile, tile)
        out = []
        for s in range(SCAN_SLABS):
            hr, hi = carry[2 * s], carry[2 * s + 1]
            bur = bu_ref[s, pl.ds(r0, tile), 0:HALF_STATES]
            bui = bu_ref[s, pl.ds(r0, tile), HALF_STATES:2 * HALF_STATES]
            nhr = (a_r[s] * hr - a_i[s] * hi) + bur
            nhi = (a_r[s] * hi + a_i[s] * hr) + bui
            hs_ref[s, pl.ds(r0, tile), 0:HALF_STATES] = nhr
            hs_ref[s, pl.ds(r0, tile), HALF_STATES:2 * HALF_STATES] = nhi
            out += [nhr, nhi]
        return tuple(out)

    init = []
    for s in range(SCAN_SLABS):
        init += [carry_ref[s, :, 0:HALF_STATES], carry_ref[s, :, HALF_STATES:2 * HALF_STATES]]
    fin = lax.fori_loop(0, SCAN_T, step, tuple(init), unroll=4)
    for s in range(SCAN_SLABS):
        carry_ref[s, :, 0:HALF_STATES] = fin[2 * s]
        carry_ref[s, :, HALF_STATES:2 * HALF_STATES] = fin[2 * s + 1]

    for s in range(SCAN_SLABS):
        yp_ref[s] = jnp.dot(hs_ref[s].astype(BF16), cmat_ref[s], preferred_element_type=F32)

    for s in range(SCAN_SLABS):
        sl = slice(s * LANES, (s + 1) * LANES)
        dsk = d_ref[:, sl]
        for b in range(BATCH):
            for t0 in range(0, SCAN_T, 2 * tile):
                ys = []
                for t1 in (t0, t0 + tile):
                    y0 = yp_ref[s, pl.ds(t1 * tile + b, tile, stride=tile), :]
                    y1 = yp_ref[s, pl.ds(t1 * tile + BATCH + b, tile, stride=tile), :]
                    ys.append(jnp.where(low, y0, y1) + dsk * u_ref[b, t1:t1 + tile, sl])
                o_ref[b, t0:t0 + 2 * tile, sl] = _gelu(jnp.concatenate(ys, axis=0)).astype(BF16)


def _s5_scan(u, bmat, cmat, a_r, a_i, d):
    width = SCAN_SLABS * LANES
    rows = SCAN_T * SUBLANES
    return pl.pallas_call(
        _s5_scan_body,
        grid=(N_SLABS // SCAN_SLABS, SEQ // SCAN_T),
        in_specs=[
            pl.BlockSpec((BATCH, SCAN_T, width), lambda s, c: (0, c, s)),
            pl.BlockSpec((SCAN_SLABS, LANES, 2 * HALF_STATES), lambda s, c: (s, 0, 0)),
            pl.BlockSpec((SCAN_SLABS, 2 * HALF_STATES, LANES), lambda s, c: (s, 0, 0)),
            pl.BlockSpec((SCAN_SLABS, SUBLANES, HALF_STATES), lambda s, c: (s, 0, 0)),
            pl.BlockSpec((SCAN_SLABS, SUBLANES, HALF_STATES), lambda s, c: (s, 0, 0)),
            pl.BlockSpec((1, width), lambda s, c: (0, s)),
        ],
        out_specs=pl.BlockSpec((BATCH, SCAN_T, width), lambda s, c: (0, c, s)),
        out_shape=jax.ShapeDtypeStruct((BATCH, SEQ, D_MODEL), BF16),
        scratch_shapes=[
            pltpu.VMEM((SCAN_SLABS, rows, LANES), F32),
            pltpu.VMEM((SCAN_SLABS, rows, 2 * HALF_STATES), F32),
            pltpu.VMEM((SCAN_SLABS, rows, 2 * HALF_STATES), F32),
            pltpu.VMEM((SCAN_SLABS, rows, LANES), F32),
            pltpu.VMEM((SCAN_SLABS, SUBLANES, 2 * HALF_STATES), F32),
        ],
        compiler_params=_params(("parallel", "arbitrary")),
        name="s5_scan",
    )(u, bmat, cmat, a_r, a_i, d)


def _s5_scan_operands(a_r, a_i, bb_r, bb_i, c_re, c_im):
    halves = SLAB_GROUPS // 2
    eye = jnp.eye(halves, dtype=F32)

    def b_blocks(bb):
        bb = bb.reshape(SSM_GROUP_DIM, N_SLABS, 2, halves, SSM_STATE)
        blk = jnp.einsum('hsqgp,gk->sqghkp', bb, eye)
        return blk.reshape(N_SLABS, LANES, HALF_STATES)

    def c_blocks(c):
        c = c.reshape(N_SLABS, 2, halves, SSM_GROUP_DIM, SSM_STATE)
        blk = jnp.einsum('sqghp,gk->skpqgh', c, eye)
        return blk.reshape(N_SLABS, HALF_STATES, LANES)

    bmat = jnp.concatenate([b_blocks(bb_r), b_blocks(bb_i)], axis=-1).astype(BF16)
    cmat = jnp.concatenate([c_blocks(c_re), -c_blocks(c_im)], axis=1).astype(BF16)

    def a_rows(a):
        a = a.reshape(N_SLABS, 2, 1, HALF_STATES)
        return jnp.broadcast_to(a, (N_SLABS, 2, BATCH, HALF_STATES)).reshape(
            N_SLABS, SUBLANES, HALF_STATES)

    return bmat, cmat, a_rows(a_r), a_rows(a_i)


def _s5_out_body(x_ref, a_ref, w_ref, o_ref, *, tn):
    a = a_ref[...]
    for c in range(0, D_MODEL, tn):
        val = jnp.dot(a, w_ref[:, c:c + tn], preferred_element_type=F32)
        gate = jnp.dot(a, w_ref[:, D_MODEL + c:D_MODEL + c + tn], preferred_element_type=F32)
        o_ref[:, c:c + tn] = x_ref[:, c:c + tn] + val * _sigmoid(gate)


def _s5_out(x, act, w, *, tm=512, tn=512):
    return pl.pallas_call(
        functools.partial(_s5_out_body, tn=tn),
        grid=(TOKENS // tm,),
        in_specs=[
            pl.BlockSpec((tm, D_MODEL), lambda i: (i, 0)),
            pl.BlockSpec((tm, D_MODEL), lambda i: (i, 0)),
            _resident((D_MODEL, 2 * D_MODEL)),
        ],
        out_specs=pl.BlockSpec((tm, D_MODEL), lambda i: (i, 0)),
        out_shape=jax.ShapeDtypeStruct((TOKENS, D_MODEL), F32),
        compiler_params=_params(("parallel",)),
        name="s5_out",
    )(x, act, w)


def _s5(x, norm_g, layer, w_in, lam_re, lam_im, log_dt, b_re, b_im, c_re, c_im, d_skip, w_out):
    a_r, a_i, bb_r, bb_i = _s5_disc(lam_re, lam_im, log_dt, b_re, b_im)
    bmat, cmat, ar_rows, ai_rows = _s5_scan_operands(a_r, a_i, bb_r, bb_i, c_re, c_im)
    u = _s5_in(x, norm_g, w_in, layer).reshape(BATCH, SEQ, D_MODEL)
    act = _s5_scan(u, bmat, cmat, ar_rows, ai_rows, d_skip.reshape(1, D_MODEL))
    return _s5_out(x, act.reshape(TOKENS, D_MODEL), w_out)


def _ple_body(x_ref, g_ref, wg_ref, p_ref, wp_ref, fg_ref, o_ref, wgb_ref, wpb_ref, *, final):
    @pl.when(pl.program_id(0) == 0)
    def _():
        wgb_ref[...] = wg_ref[...].astype(BF16)
        wpb_ref[...] = wp_ref[...].astype(BF16)

    x = x_ref[...]
    h = _rms(x, g_ref[...]).astype(BF16)
    gate = _sigmoid(jnp.dot(h, wgb_ref[...], preferred_element_type=F32))
    proj = jnp.dot(p_ref[...].astype(BF16), wpb_ref[...], preferred_element_type=F32)
    y = x + gate * proj
    o_ref[...] = _rms(y, fg_ref[...]) if final else y


def _ple(x, norm_g, wg, p, wp, fg, layer, *, final, tm=512):
    once = pl.Buffered(1)
    return pl.pallas_call(
        functools.partial(_ple_body, final=final),
        grid=(TOKENS // tm,),
        in_specs=[
            pl.BlockSpec((tm, D_MODEL), lambda i: (i, 0)),
            pl.BlockSpec((None, None, 1, D_MODEL), lambda i: (layer, 3, 0, 0)),
            pl.BlockSpec((None, D_MODEL, D_MODEL), lambda i: (layer, 0, 0), pipeline_mode=once),
            pl.BlockSpec((None, tm, PLE_DIM), lambda i: (layer, i, 0)),
            pl.BlockSpec((None, PLE_DIM, D_MODEL), lambda i: (layer, 0, 0), pipeline_mode=once),
            pl.BlockSpec((1, D_MODEL), lambda i: (0, 0)),
        ],
        out_specs=pl.BlockSpec((tm, D_MODEL), lambda i: (i, 0)),
        out_shape=jax.ShapeDtypeStruct((TOKENS, D_MODEL), F32),
        scratch_shapes=[pltpu.VMEM((D_MODEL, D_MODEL), BF16), pltpu.VMEM((PLE_DIM, D_MODEL), BF16)],
        compiler_params=_params(("arbitrary",)),
        name="ple",
    )(x, norm_g, wg, p, wp, fg)


def kernel(x, p, norm_g, final_norm_g, ffn_w_gate, ffn_w_up, ffn_w_down, gmlp_w_in, gmlp_ln_g, gmlp_ln_b, gmlp_w_s, gmlp_b_s, gmlp_w_out, s5_w_in, s5_lam_re, s5_lam_im, s5_log_dt, s5_b_re, s5_b_im, s5_c_re, s5_c_im, s5_d, s5_w_out, ple_w_gate, ple_w_proj):
    bf = lambda w: w.astype(BF16)
    x = x.reshape(TOKENS, D_MODEL)
    p = p.reshape(DEPTH, TOKENS, PLE_DIM)
    norm_g = norm_g.reshape(DEPTH, 4, 1, D_MODEL)
    fg = final_norm_g.reshape(1, D_MODEL)
    for i in range(DEPTH):
        x = _ffn(x, norm_g, ffn_w_gate, ffn_w_up, ffn_w_down, i, 0)
        j = i // 2
        if i % 2 == 0:
            x = _gmlp(x, norm_g, bf(gmlp_w_in[j]), gmlp_ln_g[j].reshape(1, GMLP_HALF),
                      gmlp_ln_b[j].reshape(1, GMLP_HALF), gmlp_w_s[j],
                      gmlp_b_s[j].reshape(GMLP_GROUPS, CHUNK, 1), bf(gmlp_w_out[j]), i)
        else:
            x = _s5(x, norm_g, i, s5_w_in[j], s5_lam_re[j], s5_lam_im[j], s5_log_dt[j],
                    s5_b_re[j], s5_b_im[j], s5_c_re[j], s5_c_im[j], s5_d[j], bf(s5_w_out[j]))
        x = _ffn(x, norm_g, ffn_w_gate, ffn_w_up, ffn_w_down, i, 1)
        x = _ple(x, norm_g, ple_w_gate, p, ple_w_proj, fg, i, final=(i == DEPTH - 1))
    return x.reshape(BATCH, SEQ, D_MODEL)
```

```python
import functools
import math

import jax
import jax.numpy as jnp
from jax import lax
from jax.experimental import pallas as pl
from jax.experimental.pallas import tpu as pltpu

F32 = jnp.float32
BF16 = jnp.bfloat16

D_MODEL = 2048
BATCH = 4
SEQ = 2048
TOKENS = BATCH * SEQ
DEPTH = 2
D_FF = 5632
PLE_DIM = 256
CHUNK = 128
GMLP_HALF = 3 * D_MODEL
GMLP_GROUPS = 16
GMLP_GROUP_DIM = GMLP_HALF // GMLP_GROUPS
SSM_GROUP_DIM = 16
SSM_GROUPS = D_MODEL // SSM_GROUP_DIM
SSM_STATE = 64
EPS = 1e-6

LANES = 128
SUBLANES = 8
VMEM_LIMIT = 60 * 1024 * 1024

SLAB_GROUPS = LANES // SSM_GROUP_DIM
N_SLABS = D_MODEL // LANES
SLAB_STATES = SLAB_GROUPS * SSM_STATE
HALF_STATES = SLAB_STATES // 2
HALF_CH = LANES // 2
SCAN_PAIR = 2
SCAN_SLABS = 2 * SCAN_PAIR
SCAN_T = 128


def _rms(x, g):
    ms = jnp.mean(x * x, axis=-1, keepdims=True)
    return x * lax.rsqrt(ms + EPS) * g


def _gelu(x):
    c = math.sqrt(2.0 / math.pi)
    return x * (0.5 * (1.0 + jnp.tanh(c * (x + 0.044715 * (x * x * x)))))


def _sigmoid(x):
    return 1.0 / (1.0 + jnp.exp(-x))


def _params(sem):
    return pltpu.CompilerParams(dimension_semantics=sem, vmem_limit_bytes=VMEM_LIMIT)


def _gain_spec(layer, slot):
    return pl.BlockSpec((None, None, 1, D_MODEL), lambda i, j: (layer, slot, 0, 0))


def _ffn_body(x_ref, g_ref, wg_ref, wu_ref, wd_ref, o_ref, h_ref):
    @pl.when(pl.program_id(1) == 0)
    def _():
        x = x_ref[...]
        h_ref[...] = _rms(x, g_ref[...]).astype(BF16)
        o_ref[...] = x

    h = h_ref[...]
    gate = jnp.dot(h, wg_ref[...].astype(BF16), preferred_element_type=F32)
    up = jnp.dot(h, wu_ref[...].astype(BF16), preferred_element_type=F32)
    a = (gate * _sigmoid(gate)) * (0.5 * up)
    o_ref[...] += jnp.dot(a.astype(BF16), wd_ref[...].astype(BF16), preferred_element_type=F32)


def _ffn(x, norm_g, wg, wu, wd, layer, which, *, tm=1024, tf=512):
    return pl.pallas_call(
        _ffn_body,
        grid=(TOKENS // tm, D_FF // tf),
        in_specs=[
            pl.BlockSpec((tm, D_MODEL), lambda i, j: (i, 0), pipeline_mode=pl.Buffered(1)),
            _gain_spec(layer, 2 * which),
            pl.BlockSpec((None, None, D_MODEL, tf), lambda i, j: (layer, which, 0, j)),
            pl.BlockSpec((None, None, D_MODEL, tf), lambda i, j: (layer, which, 0, j)),
            pl.BlockSpec((None, None, tf, D_MODEL), lambda i, j: (layer, which, j, 0)),
        ],
        out_specs=pl.BlockSpec((tm, D_MODEL), lambda i, j: (i, 0)),
        out_shape=jax.ShapeDtypeStruct((TOKENS, D_MODEL), F32),
        scratch_shapes=[pltpu.VMEM((tm, D_MODEL), BF16)],
        compiler_params=_params(("parallel", "arbitrary")),
        name="ffn",
    )(x, norm_g, wg, wu, wd)


GMLP_STEP_GROUPS = 2
GMLP_STEP_W = GMLP_STEP_GROUPS * GMLP_GROUP_DIM


def _gmlp_body(x_ref, g_ref, wu_ref, wv_ref, lng_ref, lnb_ref, ws_ref, bs_ref, wo_ref,
               o_ref, h_ref, *, tm):
    @pl.when(pl.program_id(1) == 0)
    def _():
        x = x_ref[...]
        h_ref[...] = _rms(x, g_ref[...]).astype(BF16)
        o_ref[...] = x

    h = h_ref[...]
    zu = _gelu(jnp.dot(h, wu_ref[...], preferred_element_type=F32))
    zv = _gelu(jnp.dot(h, wv_ref[...], preferred_element_type=F32))
    row = lax.broadcasted_iota(jnp.int32, (CHUNK, CHUNK), 0)
    col = lax.broadcasted_iota(jnp.int32, (CHUNK, CHUNK), 1)
    parts = []
    for gi in range(GMLP_STEP_GROUPS):
        lo, hi = gi * GMLP_GROUP_DIM, (gi + 1) * GMLP_GROUP_DIM
        v = zv[:, lo:hi]
        d = v - jnp.mean(v, axis=-1, keepdims=True)
        var = jnp.mean(d * d, axis=-1, keepdims=True)
        vn = ((d * lax.rsqrt(var + EPS)) * lng_ref[:, lo:hi] + lnb_ref[:, lo:hi]).astype(BF16)
        ws = jnp.where(row >= col, ws_ref[gi], 0.0).astype(BF16)
        bias = bs_ref[gi]
        sv = [jnp.dot(ws, vn[c * CHUNK:(c + 1) * CHUNK], preferred_element_type=F32) + bias
              for c in range(tm // CHUNK)]
        parts.append(zu[:, lo:hi] * jnp.concatenate(sv, axis=0))
    gated = jnp.concatenate(parts, axis=-1).astype(BF16)
    o_ref[...] += jnp.dot(gated, wo_ref[...], preferred_element_type=F32)


def _gmlp(x, norm_g, w_in, ln_g, ln_b, w_s, b_s, w_out, layer, *, tm=512):
    nj = GMLP_GROUPS // GMLP_STEP_GROUPS
    return pl.pallas_call(
        functools.partial(_gmlp_body, tm=tm),
        grid=(TOKENS // tm, nj),
        in_specs=[
            pl.BlockSpec((tm, D_MODEL), lambda i, j: (i, 0)),
            _gain_spec(layer, 1),
            pl.BlockSpec((D_MODEL, GMLP_STEP_W), lambda i, j: (0, j)),
            pl.BlockSpec((D_MODEL, GMLP_STEP_W), lambda i, j: (0, j + nj)),
            pl.BlockSpec((1, GMLP_STEP_W), lambda i, j: (0, j)),
            pl.BlockSpec((1, GMLP_STEP_W), lambda i, j: (0, j)),
            pl.BlockSpec((GMLP_STEP_GROUPS, CHUNK, CHUNK), lambda i, j: (j, 0, 0)),
            pl.BlockSpec((GMLP_STEP_GROUPS, CHUNK, 1), lambda i, j: (j, 0, 0)),
            pl.BlockSpec((GMLP_STEP_W, D_MODEL), lambda i, j: (j, 0)),
        ],
        out_specs=pl.BlockSpec((tm, D_MODEL), lambda i, j: (i, 0)),
        out_shape=jax.ShapeDtypeStruct((TOKENS, D_MODEL), F32),
        scratch_shapes=[pltpu.VMEM((tm, D_MODEL), BF16)],
        compiler_params=_params(("parallel", "arbitrary")),
        name="gmlp",
    )(x, norm_g, w_in, w_in, ln_g, ln_b, w_s, b_s, w_out)


def _s5_disc_body(lr_ref, li_ref, ldt_ref, br_ref, bi_ref, ar_ref, ai_ref, bbr_ref, bbi_ref):
    lr, li = lr_ref[...], li_ref[...]
    dt = jnp.exp(ldt_ref[...])
    mag = jnp.exp(lr * dt)
    ang = li * dt
    a_r = mag * jnp.cos(ang)
    a_i = mag * jnp.sin(ang)
    den = lr * lr + li * li
    nr = a_r - 1.0
    z_r = (nr * lr + a_i * li) / den
    z_i = (a_i * lr - nr * li) / den
    ar_ref[...] = a_r
    ai_ref[...] = a_i
    bbr_ref[...] = z_r * br_ref[...] - z_i * bi_ref[...]
    bbi_ref[...] = z_r * bi_ref[...] + z_i * br_ref[...]


def _s5_disc(lam_re, lam_im, log_dt, b_re, b_im):
    n = SSM_GROUPS * SSM_STATE
    row = jax.ShapeDtypeStruct((1, n), F32)
    mat = jax.ShapeDtypeStruct((SSM_GROUP_DIM, n), F32)
    to_mat = lambda b: jnp.transpose(b, (2, 0, 1)).reshape(SSM_GROUP_DIM, n)
    ldt = jnp.broadcast_to(log_dt[:, None], (SSM_GROUPS, SSM_STATE)).reshape(1, n)
    return pl.pallas_call(_s5_disc_body, out_shape=(row, row, mat, mat), name="s5_disc")(
        lam_re.reshape(1, n), lam_im.reshape(1, n), ldt, to_mat(b_re), to_mat(b_im))


def _resident(shape):
    zeros = (0,) * len(shape)
    return pl.BlockSpec(shape, lambda i: zeros, pipeline_mode=pl.Buffered(1))


def _s5_in_body(x_ref, g_ref, w_ref, o_ref, wb_ref):
    @pl.when(pl.program_id(0) == 0)
    def _():
        wb_ref[...] = w_ref[...].astype(BF16)

    h = _rms(x_ref[...], g_ref[...]).astype(BF16)
    o_ref[...] = jnp.dot(h, wb_ref[...], preferred_element_type=F32)


def _s5_in(x, norm_g, w, layer, *, tm=512):
    return pl.pallas_call(
        _s5_in_body,
        grid=(TOKENS // tm,),
        in_specs=[
            pl.BlockSpec((tm, D_MODEL), lambda i: (i, 0)),
            pl.BlockSpec((None, None, 1, D_MODEL), lambda i: (layer, 1, 0, 0)),
            _resident((D_MODEL, D_MODEL)),
        ],
        out_specs=pl.BlockSpec((tm, D_MODEL), lambda i: (i, 0)),
        out_shape=jax.ShapeDtypeStruct((TOKENS, D_MODEL), F32),
        scratch_shapes=[pltpu.VMEM((D_MODEL, D_MODEL), BF16)],
        compiler_params=_params(("arbitrary",)),
        name="s5_in",
    )(x, norm_g, w)


def _s5_scan_body(u_ref, bmat_ref, cmat_ref, ar_ref, ai_ref, d_ref, o_ref,
                  lhs_ref, bu0_ref, bu1_ref, hs0_ref, hs1_ref, yp_ref, carry_ref):
    tile = SUBLANES
    lane = lax.broadcasted_iota(jnp.int32, (tile, LANES), 1)
    low = lane < HALF_CH
    bu_refs = (bu0_ref, bu1_ref)
    hs_refs = (hs0_ref, hs1_ref)

    @pl.when(pl.program_id(1) == 0)
    def _():
        carry_ref[...] = jnp.zeros_like(carry_ref)

    for s in range(SCAN_SLABS):
        for b in range(BATCH):
            for t0 in range(0, SCAN_T, tile):
                v = u_ref[b, t0:t0 + tile, s * LANES:(s + 1) * LANES]
                lhs_ref[s, pl.ds(t0 * tile + b, tile, stride=tile), :] = jnp.where(low, v, 0.0)
                lhs_ref[s, pl.ds(t0 * tile + BATCH + b, tile, stride=tile), :] = jnp.where(low, 0.0, v)

    def to_states(pair):
        for k in range(SCAN_PAIR):
            s = pair * SCAN_PAIR + k
            bu_refs[pair][k] = jnp.dot(lhs_ref[s].astype(BF16), bmat_ref[s],
                                       preferred_element_type=F32)

    def scan(pair):
        bu_ref, hs_ref = bu_refs[pair], hs_refs[pair]
        re, im = slice(0, HALF_STATES), slice(HALF_STATES, 2 * HALF_STATES)
        hs = []
        for k in range(SCAN_PAIR):
            s = pair * SCAN_PAIR + k
            hs.append((ar_ref[s], ai_ref[s], carry_ref[s, :, re], carry_ref[s, :, im]))
        for t in range(SCAN_T):
            rows = slice(t * tile, (t + 1) * tile)
            for k in range(SCAN_PAIR):
                a_r, a_i, hr, hi = hs[k]
                hr, hi = ((a_r * hr - a_i * hi) + bu_ref[k, rows, re],
                          (a_r * hi + a_i * hr) + bu_ref[k, rows, im])
                hs_ref[k, rows, re] = hr
                hs_ref[k, rows, im] = hi
                hs[k] = (a_r, a_i, hr, hi)
        for k in range(SCAN_PAIR):
            s = pair * SCAN_PAIR + k
            carry_ref[s, :, re] = hs[k][2]
            carry_ref[s, :, im] = hs[k][3]

    def from_states(pair):
        for k in range(SCAN_PAIR):
            s = pair * SCAN_PAIR + k
            yp_ref[s] = jnp.dot(hs_refs[pair][k].astype(BF16), cmat_ref[s],
                                preferred_element_type=F32)

    to_states(0)
    to_states(1)
    scan(0)
    from_states(0)
    scan(1)
    from_states(1)

    for s in range(SCAN_SLABS):
        sl = slice(s * LANES, (s + 1) * LANES)
        dsk = d_ref[:, sl]
        for b in range(BATCH):
            for t0 in range(0, SCAN_T, 2 * tile):
                ys = []
                for t1 in (t0, t0 + tile):
                    y0 = yp_ref[s, pl.ds(t1 * tile + b, tile, stride=tile), :]
                    y1 = yp_ref[s, pl.ds(t1 * tile + BATCH + b, tile, stride=tile), :]
                    ys.append(jnp.where(low, y0, y1) + dsk * u_ref[b, t1:t1 + tile, sl])
                o_ref[b, t0:t0 + 2 * tile, sl] = _gelu(jnp.concatenate(ys, axis=0)).astype(BF16)


def _s5_scan(u, bmat, cmat, a_r, a_i, d):
    width = SCAN_SLABS * LANES
    rows = SCAN_T * SUBLANES
    pair_buf = pltpu.VMEM((SCAN_PAIR, rows, 2 * HALF_STATES), F32)
    return pl.pallas_call(
        _s5_scan_body,
        grid=(N_SLABS // SCAN_SLABS, SEQ // SCAN_T),
        in_specs=[
            pl.BlockSpec((BATCH, SCAN_T, width), lambda s, c: (0, c, s)),
            pl.BlockSpec((SCAN_SLABS, LANES, 2 * HALF_STATES), lambda s, c: (s, 0, 0)),
            pl.BlockSpec((SCAN_SLABS, 2 * HALF_STATES, LANES), lambda s, c: (s, 0, 0)),
            pl.BlockSpec((SCAN_SLABS, SUBLANES, HALF_STATES), lambda s, c: (s, 0, 0)),
            pl.BlockSpec((SCAN_SLABS, SUBLANES, HALF_STATES), lambda s, c: (s, 0, 0)),
            pl.BlockSpec((1, width), lambda s, c: (0, s)),
        ],
        out_specs=pl.BlockSpec((BATCH, SCAN_T, width), lambda s, c: (0, c, s)),
        out_shape=jax.ShapeDtypeStruct((BATCH, SEQ, D_MODEL), BF16),
        scratch_shapes=[
            pltpu.VMEM((SCAN_SLABS, rows, LANES), F32),
            pair_buf, pair_buf, pair_buf, pair_buf,
            pltpu.VMEM((SCAN_SLABS, rows, LANES), F32),
            pltpu.VMEM((SCAN_SLABS, SUBLANES, 2 * HALF_STATES), F32),
        ],
        compiler_params=_params(("parallel", "arbitrary")),
        name="s5_scan",
    )(u, bmat, cmat, a_r, a_i, d)


def _s5_scan_operands(a_r, a_i, bb_r, bb_i, c_re, c_im):
    halves = SLAB_GROUPS // 2
    eye = jnp.eye(halves, dtype=F32)

    def b_blocks(bb):
        bb = bb.reshape(SSM_GROUP_DIM, N_SLABS, 2, halves, SSM_STATE)
        blk = jnp.einsum('hsqgp,gk->sqghkp', bb, eye)
        return blk.reshape(N_SLABS, LANES, HALF_STATES)

    def c_blocks(c):
        c = c.reshape(N_SLABS, 2, halves, SSM_GROUP_DIM, SSM_STATE)
        blk = jnp.einsum('sqghp,gk->skpqgh', c, eye)
        return blk.reshape(N_SLABS, HALF_STATES, LANES)

    bmat = jnp.concatenate([b_blocks(bb_r), b_blocks(bb_i)], axis=-1).astype(BF16)
    cmat = jnp.concatenate([c_blocks(c_re), -c_blocks(c_im)], axis=1).astype(BF16)

    def a_rows(a):
        a = a.reshape(N_SLABS, 2, 1, HALF_STATES)
        return jnp.broadcast_to(a, (N_SLABS, 2, BATCH, HALF_STATES)).reshape(
            N_SLABS, SUBLANES, HALF_STATES)

    return bmat, cmat, a_rows(a_r), a_rows(a_i)


def _s5_out_body(x_ref, a_ref, w_ref, o_ref, *, tn):
    a = a_ref[...]
    for c in range(0, D_MODEL, tn):
        val = jnp.dot(a, w_ref[:, c:c + tn], preferred_element_type=F32)
        gate = jnp.dot(a, w_ref[:, D_MODEL + c:D_MODEL + c + tn], preferred_element_type=F32)
        o_ref[:, c:c + tn] = x_ref[:, c:c + tn] + val * _sigmoid(gate)


def _s5_out(x, act, w, *, tm=512, tn=512):
    return pl.pallas_call(
        functools.partial(_s5_out_body, tn=tn),
        grid=(TOKENS // tm,),
        in_specs=[
            pl.BlockSpec((tm, D_MODEL), lambda i: (i, 0)),
            pl.BlockSpec((tm, D_MODEL), lambda i: (i, 0)),
            _resident((D_MODEL, 2 * D_MODEL)),
        ],
        out_specs=pl.BlockSpec((tm, D_MODEL), lambda i: (i, 0)),
        out_shape=jax.ShapeDtypeStruct((TOKENS, D_MODEL), F32),
        compiler_params=_params(("parallel",)),
        name="s5_out",
    )(x, act, w)


def _s5(x, norm_g, layer, w_in, lam_re, lam_im, log_dt, b_re, b_im, c_re, c_im, d_skip, w_out):
    a_r, a_i, bb_r, bb_i = _s5_disc(lam_re, lam_im, log_dt, b_re, b_im)
    bmat, cmat, ar_rows, ai_rows = _s5_scan_operands(a_r, a_i, bb_r, bb_i, c_re, c_im)
    u = _s5_in(x, norm_g, w_in, layer).reshape(BATCH, SEQ, D_MODEL)
    act = _s5_scan(u, bmat, cmat, ar_rows, ai_rows, d_skip.reshape(1, D_MODEL))
    return _s5_out(x, act.reshape(TOKENS, D_MODEL), w_out)


def _ple_body(x_ref, g_ref, wg_ref, p_ref, wp_ref, fg_ref, o_ref, wgb_ref, wpb_ref, *, final):
    @pl.when(pl.program_id(0) == 0)
    def _():
        wgb_ref[...] = wg_ref[...].astype(BF16)
        wpb_ref[...] = wp_ref[...].astype(BF16)

    x = x_ref[...]
    h = _rms(x, g_ref[...]).astype(BF16)
    gate = _sigmoid(jnp.dot(h, wgb_ref[...], preferred_element_type=F32))
    proj = jnp.dot(p_ref[...].astype(BF16), wpb_ref[...], preferred_element_type=F32)
    y = x + gate * proj
    o_ref[...] = _rms(y, fg_ref[...]) if final else y


def _ple(x, norm_g, wg, p, wp, fg, layer, *, final, tm=512):
    once = pl.Buffered(1)
    return pl.pallas_call(
        functools.partial(_ple_body, final=final),
        grid=(TOKENS // tm,),
        in_specs=[
            pl.BlockSpec((tm, D_MODEL), lambda i: (i, 0)),
            pl.BlockSpec((None, None, 1, D_MODEL), lambda i: (layer, 3, 0, 0)),
            pl.BlockSpec((None, D_MODEL, D_MODEL), lambda i: (layer, 0, 0), pipeline_mode=once),
            pl.BlockSpec((None, tm, PLE_DIM), lambda i: (layer, i, 0)),
            pl.BlockSpec((None, PLE_DIM, D_MODEL), lambda i: (layer, 0, 0), pipeline_mode=once),
            pl.BlockSpec((1, D_MODEL), lambda i: (0, 0)),
        ],
        out_specs=pl.BlockSpec((tm, D_MODEL), lambda i: (i, 0)),
        out_shape=jax.ShapeDtypeStruct((TOKENS, D_MODEL), F32),
        scratch_shapes=[pltpu.VMEM((D_MODEL, D_MODEL), BF16), pltpu.VMEM((PLE_DIM, D_MODEL), BF16)],
        compiler_params=_params(("arbitrary",)),
        name="ple",
    )(x, norm_g, wg, p, wp, fg)


def kernel(x, p, norm_g, final_norm_g, ffn_w_gate, ffn_w_up, ffn_w_down, gmlp_w_in, gmlp_ln_g, gmlp_ln_b, gmlp_w_s, gmlp_b_s, gmlp_w_out, s5_w_in, s5_lam_re, s5_lam_im, s5_log_dt, s5_b_re, s5_b_im, s5_c_re, s5_c_im, s5_d, s5_w_out, ple_w_gate, ple_w_proj):
    bf = lambda w: w.astype(BF16)
    x = x.reshape(TOKENS, D_MODEL)
    p = p.reshape(DEPTH, TOKENS, PLE_DIM)
    norm_g = norm_g.reshape(DEPTH, 4, 1, D_MODEL)
    fg = final_norm_g.reshape(1, D_MODEL)
    for i in range(DEPTH):
        x = _ffn(x, norm_g, ffn_w_gate, ffn_w_up, ffn_w_down, i, 0)
        j = i // 2
        if i % 2 == 0:
            x = _gmlp(x, norm_g, bf(gmlp_w_in[j]), gmlp_ln_g[j].reshape(1, GMLP_HALF),
                      gmlp_ln_b[j].reshape(1, GMLP_HALF), gmlp_w_s[j],
                      gmlp_b_s[j].reshape(GMLP_GROUPS, CHUNK, 1), bf(gmlp_w_out[j]), i)
        else:
            x = _s5(x, norm_g, i, s5_w_in[j], s5_lam_re[j], s5_lam_im[j], s5_log_dt[j],
                    s5_b_re[j], s5_b_im[j], s5_c_re[j], s5_c_im[j], s5_d[j], bf(s5_w_out[j]))
        x = _ffn(x, norm_g, ffn_w_gate, ffn_w_up, ffn_w_down, i, 1)
        x = _ple(x, norm_g, ple_w_gate, p, ple_w_proj, fg, i, final=(i == DEPTH - 1))
    return x.reshape(BATCH, SEQ, D_MODEL)
```

```python
import functools
import math

import jax
import jax.numpy as jnp
from jax import lax
from jax.experimental import pallas as pl
from jax.experimental.pallas import tpu as pltpu

F32 = jnp.float32
BF16 = jnp.bfloat16

D_MODEL = 2048
BATCH = 4
SEQ = 2048
TOKENS = BATCH * SEQ
DEPTH = 2
D_FF = 5632
PLE_DIM = 256
CHUNK = 128
GMLP_HALF = 3 * D_MODEL
GMLP_GROUPS = 16
GMLP_GROUP_DIM = GMLP_HALF // GMLP_GROUPS
SSM_GROUP_DIM = 16
SSM_GROUPS = D_MODEL // SSM_GROUP_DIM
SSM_STATE = 64
EPS = 1e-6

LANES = 128
SUBLANES = 8
VMEM_LIMIT = 60 * 1024 * 1024

SLAB_GROUPS = LANES // SSM_GROUP_DIM
N_SLABS = D_MODEL // LANES
SLAB_STATES = SLAB_GROUPS * SSM_STATE
HALF_STATES = SLAB_STATES // 2
HALF_CH = LANES // 2
SCAN_PAIR = 2
SCAN_SLABS = 2 * SCAN_PAIR
SCAN_T = 128


def _rms(x, g):
    ms = jnp.mean(x * x, axis=-1, keepdims=True)
    return x * lax.rsqrt(ms + EPS) * g


def _gelu(x):
    c = math.sqrt(2.0 / math.pi)
    return x * (0.5 * (1.0 + jnp.tanh(c * (x + 0.044715 * (x * x * x)))))


def _sigmoid(x):
    return 1.0 / (1.0 + jnp.exp(-x))


def _params(sem):
    return pltpu.CompilerParams(dimension_semantics=sem, vmem_limit_bytes=VMEM_LIMIT)


def _gain_spec(layer, slot):
    return pl.BlockSpec((None, None, 1, D_MODEL), lambda i, j: (layer, slot, 0, 0))


def _ffn_body(x_hbm, g_ref, wg_ref, wu_ref, wd_ref, o_ref, h_ref, x_ref, sem, *, tm):
    i, j = pl.program_id(0), pl.program_id(1)

    def x_copy(tile):
        rows = pl.ds(pl.multiple_of(tile * tm, tm), tm)
        return pltpu.make_async_copy(x_hbm.at[rows, :], x_ref, sem)

    @pl.when(j == 0)
    def _():
        @pl.when(i == 0)
        def _():
            x_copy(i).start()

        x_copy(i).wait()
        x = x_ref[...]
        h_ref[...] = _rms(x, g_ref[...]).astype(BF16)
        o_ref[...] = x

    @pl.when(jnp.logical_and(j == 1, i + 1 < pl.num_programs(0)))
    def _():
        x_copy(i + 1).start()

    h = h_ref[...]
    gate = jnp.dot(h, wg_ref[...].astype(BF16), preferred_element_type=F32)
    up = jnp.dot(h, wu_ref[...].astype(BF16), preferred_element_type=F32)
    a = (gate * _sigmoid(gate)) * (0.5 * up)
    o_ref[...] += jnp.dot(a.astype(BF16), wd_ref[...].astype(BF16), preferred_element_type=F32)


def _ffn(x, norm_g, wg, wu, wd, layer, which, *, tm=1024, tf=512):
    return pl.pallas_call(
        functools.partial(_ffn_body, tm=tm),
        grid=(TOKENS // tm, D_FF // tf),
        in_specs=[
            pl.BlockSpec(memory_space=pl.ANY),
            _gain_spec(layer, 2 * which),
            pl.BlockSpec((None, None, D_MODEL, tf), lambda i, j: (layer, which, 0, j)),
            pl.BlockSpec((None, None, D_MODEL, tf), lambda i, j: (layer, which, 0, j)),
            pl.BlockSpec((None, None, tf, D_MODEL), lambda i, j: (layer, which, j, 0)),
        ],
        out_specs=pl.BlockSpec((tm, D_MODEL), lambda i, j: (i, 0)),
        out_shape=jax.ShapeDtypeStruct((TOKENS, D_MODEL), F32),
        scratch_shapes=[pltpu.VMEM((tm, D_MODEL), BF16), pltpu.VMEM((tm, D_MODEL), F32),
                        pltpu.SemaphoreType.DMA(())],
        compiler_params=_params(("arbitrary", "arbitrary")),
        name="ffn",
    )(x, norm_g, wg, wu, wd)


GMLP_STEP_GROUPS = 2
GMLP_STEP_W = GMLP_STEP_GROUPS * GMLP_GROUP_DIM


def _gmlp_body(x_ref, g_ref, wu_ref, wv_ref, lng_ref, lnb_ref, ws_ref, bs_ref, wo_ref,
               o_ref, h_ref, gated_ref, *, tm):
    j = pl.program_id(1)
    last = pl.num_programs(1) - 1

    def out_stage():
        o_ref[...] += jnp.dot(gated_ref[...], wo_ref[...], preferred_element_type=F32)

    def gate_stage(project_previous):
        h = h_ref[...]
        zu = jnp.dot(h, wu_ref[...], preferred_element_type=F32)
        zv = jnp.dot(h, wv_ref[...], preferred_element_type=F32)
        if project_previous:
            out_stage()
        zu, zv = _gelu(zu), _gelu(zv)
        row = lax.broadcasted_iota(jnp.int32, (CHUNK, CHUNK), 0)
        col = lax.broadcasted_iota(jnp.int32, (CHUNK, CHUNK), 1)
        for gi in range(GMLP_STEP_GROUPS):
            lo, hi = gi * GMLP_GROUP_DIM, (gi + 1) * GMLP_GROUP_DIM
            v = zv[:, lo:hi]
            d = v - jnp.mean(v, axis=-1, keepdims=True)
            var = jnp.mean(d * d, axis=-1, keepdims=True)
            vn = ((d * lax.rsqrt(var + EPS)) * lng_ref[:, lo:hi] + lnb_ref[:, lo:hi]).astype(BF16)
            ws = jnp.where(row >= col, ws_ref[gi], 0.0).astype(BF16)
            bias = bs_ref[gi]
            for c in range(tm // CHUNK):
                rows = slice(c * CHUNK, (c + 1) * CHUNK)
                sv = jnp.dot(ws, vn[rows], preferred_element_type=F32) + bias
                gated_ref[rows, lo:hi] = (zu[rows, lo:hi] * sv).astype(BF16)

    @pl.when(j == 0)
    def _():
        x = x_ref[...]
        h_ref[...] = _rms(x, g_ref[...]).astype(BF16)
        o_ref[...] = x
        gate_stage(False)

    @pl.when(jnp.logical_and(j > 0, j < last))
    def _():
        gate_stage(True)

    @pl.when(j == last)
    def _():
        out_stage()


def _gmlp(x, norm_g, w_in, ln_g, ln_b, w_s, b_s, w_out, layer, *, tm=512):
    nj = GMLP_GROUPS // GMLP_STEP_GROUPS
    cur = lambda j: jnp.minimum(j, nj - 1)
    prev = lambda j: jnp.maximum(j - 1, 0)
    return pl.pallas_call(
        functools.partial(_gmlp_body, tm=tm),
        grid=(TOKENS // tm, nj + 1),
        in_specs=[
            pl.BlockSpec((tm, D_MODEL), lambda i, j: (i, 0)),
            _gain_spec(layer, 1),
            pl.BlockSpec((D_MODEL, GMLP_STEP_W), lambda i, j: (0, cur(j))),
            pl.BlockSpec((D_MODEL, GMLP_STEP_W), lambda i, j: (0, cur(j) + nj)),
            pl.BlockSpec((1, GMLP_STEP_W), lambda i, j: (0, cur(j))),
            pl.BlockSpec((1, GMLP_STEP_W), lambda i, j: (0, cur(j))),
            pl.BlockSpec((GMLP_STEP_GROUPS, CHUNK, CHUNK), lambda i, j: (cur(j), 0, 0)),
            pl.BlockSpec((GMLP_STEP_GROUPS, CHUNK, 1), lambda i, j: (cur(j), 0, 0)),
            pl.BlockSpec((GMLP_STEP_W, D_MODEL), lambda i, j: (prev(j), 0)),
        ],
        out_specs=pl.BlockSpec((tm, D_MODEL), lambda i, j: (i, 0)),
        out_shape=jax.ShapeDtypeStruct((TOKENS, D_MODEL), F32),
        scratch_shapes=[pltpu.VMEM((tm, D_MODEL), BF16), pltpu.VMEM((tm, GMLP_STEP_W), BF16)],
        compiler_params=_params(("parallel", "arbitrary")),
        name="gmlp",
    )(x, norm_g, w_in, w_in, ln_g, ln_b, w_s, b_s, w_out)


def _s5_disc_body(lr_ref, li_ref, ldt_ref, br_ref, bi_ref, ar_ref, ai_ref, bbr_ref, bbi_ref):
    lr, li = lr_ref[...], li_ref[...]
    dt = jnp.exp(ldt_ref[...])
    mag = jnp.exp(lr * dt)
    ang = li * dt
    a_r = mag * jnp.cos(ang)
    a_i = mag * jnp.sin(ang)
    den = lr * lr + li * li
    nr = a_r - 1.0
    z_r = (nr * lr + a_i * li) / den
    z_i = (a_i * lr - nr * li) / den
    ar_ref[...] = a_r
    ai_ref[...] = a_i
    bbr_ref[...] = z_r * br_ref[...] - z_i * bi_ref[...]
    bbi_ref[...] = z_r * bi_ref[...] + z_i * br_ref[...]


def _s5_disc(lam_re, lam_im, log_dt, b_re, b_im):
    n = SSM_GROUPS * SSM_STATE
    row = jax.ShapeDtypeStruct((1, n), F32)
    mat = jax.ShapeDtypeStruct((SSM_GROUP_DIM, n), F32)
    to_mat = lambda b: jnp.transpose(b, (2, 0, 1)).reshape(SSM_GROUP_DIM, n)
    ldt = jnp.broadcast_to(log_dt[:, None], (SSM_GROUPS, SSM_STATE)).reshape(1, n)
    return pl.pallas_call(_s5_disc_body, out_shape=(row, row, mat, mat), name="s5_disc")(
        lam_re.reshape(1, n), lam_im.reshape(1, n), ldt, to_mat(b_re), to_mat(b_im))


def _resident(shape):
    zeros = (0,) * len(shape)
    return pl.BlockSpec(shape, lambda i: zeros, pipeline_mode=pl.Buffered(1))


def _s5_in_body(x_ref, g_ref, w_ref, o_ref, wb_ref):
    @pl.when(pl.program_id(0) == 0)
    def _():
        wb_ref[...] = w_ref[...].astype(BF16)

    h = _rms(x_ref[...], g_ref[...]).astype(BF16)
    o_ref[...] = jnp.dot(h, wb_ref[...], preferred_element_type=F32)


def _s5_in(x, norm_g, w, layer, *, tm=512):
    return pl.pallas_call(
        _s5_in_body,
        grid=(TOKENS // tm,),
        in_specs=[
            pl.BlockSpec((tm, D_MODEL), lambda i: (i, 0)),
            pl.BlockSpec((None, None, 1, D_MODEL), lambda i: (layer, 1, 0, 0)),
            _resident((D_MODEL, D_MODEL)),
        ],
        out_specs=pl.BlockSpec((tm, D_MODEL), lambda i: (i, 0)),
        out_shape=jax.ShapeDtypeStruct((TOKENS, D_MODEL), F32),
        scratch_shapes=[pltpu.VMEM((D_MODEL, D_MODEL), BF16)],
        compiler_params=_params(("arbitrary",)),
        name="s5_in",
    )(x, norm_g, w)


def _s5_scan_body(u_ref, bmat_ref, cmat_ref, ar_ref, ai_ref, d_ref, o_ref,
                  lhs_ref, bu0_ref, bu1_ref, hs0_ref, hs1_ref, yp_ref, carry_ref):
    tile = SUBLANES
    lane = lax.broadcasted_iota(jnp.int32, (tile, LANES), 1)
    low = lane < HALF_CH
    bu_refs = (bu0_ref, bu1_ref)
    hs_refs = (hs0_ref, hs1_ref)

    @pl.when(pl.program_id(1) == 0)
    def _():
        carry_ref[...] = jnp.zeros_like(carry_ref)

    for s in range(SCAN_SLABS):
        for b in range(BATCH):
            for t0 in range(0, SCAN_T, tile):
                v = u_ref[b, t0:t0 + tile, s * LANES:(s + 1) * LANES]
                lhs_ref[s, pl.ds(t0 * tile + b, tile, stride=tile), :] = jnp.where(low, v, 0.0)
                lhs_ref[s, pl.ds(t0 * tile + BATCH + b, tile, stride=tile), :] = jnp.where(low, 0.0, v)

    def to_states(pair):
        for k in range(SCAN_PAIR):
            s = pair * SCAN_PAIR + k
            bu_refs[pair][k] = jnp.dot(lhs_ref[s].astype(BF16), bmat_ref[s],
                                       preferred_element_type=F32)

    def scan(pair):
        bu_ref, hs_ref = bu_refs[pair], hs_refs[pair]
        re, im = slice(0, HALF_STATES), slice(HALF_STATES, 2 * HALF_STATES)
        hs = []
        for k in range(SCAN_PAIR):
            s = pair * SCAN_PAIR + k
            hs.append((ar_ref[s], ai_ref[s], carry_ref[s, :, re], carry_ref[s, :, im]))
        for t in range(SCAN_T):
            rows = slice(t * tile, (t + 1) * tile)
            for k in range(SCAN_PAIR):
                a_r, a_i, hr, hi = hs[k]
                hr, hi = ((a_r * hr - a_i * hi) + bu_ref[k, rows, re],
                          (a_r * hi + a_i * hr) + bu_ref[k, rows, im])
                hs_ref[k, rows, re] = hr
                hs_ref[k, rows, im] = hi
                hs[k] = (a_r, a_i, hr, hi)
        for k in range(SCAN_PAIR):
            s = pair * SCAN_PAIR + k
            carry_ref[s, :, re] = hs[k][2]
            carry_ref[s, :, im] = hs[k][3]

    def from_states(pair):
        for k in range(SCAN_PAIR):
            s = pair * SCAN_PAIR + k
            yp_ref[s] = jnp.dot(hs_refs[pair][k].astype(BF16), cmat_ref[s],
                                preferred_element_type=F32)

    to_states(0)
    to_states(1)
    scan(0)
    from_states(0)
    scan(1)
    from_states(1)

    for s in range(SCAN_SLABS):
        sl = slice(s * LANES, (s + 1) * LANES)
        dsk = d_ref[:, sl]
        for b in range(BATCH):
            for t0 in range(0, SCAN_T, 2 * tile):
                ys = []
                for t1 in (t0, t0 + tile):
                    y0 = yp_ref[s, pl.ds(t1 * tile + b, tile, stride=tile), :]
                    y1 = yp_ref[s, pl.ds(t1 * tile + BATCH + b, tile, stride=tile), :]
                    ys.append(jnp.where(low, y0, y1) + dsk * u_ref[b, t1:t1 + tile, sl])
                o_ref[b, t0:t0 + 2 * tile, sl] = _gelu(jnp.concatenate(ys, axis=0)).astype(BF16)


def _s5_scan(u, bmat, cmat, a_r, a_i, d):
    width = SCAN_SLABS * LANES
    rows = SCAN_T * SUBLANES
    pair_buf = pltpu.VMEM((SCAN_PAIR, rows, 2 * HALF_STATES), F32)
    return pl.pallas_call(
        _s5_scan_body,
        grid=(N_SLABS // SCAN_SLABS, SEQ // SCAN_T),
        in_specs=[
            pl.BlockSpec((BATCH, SCAN_T, width), lambda s, c: (0, c, s)),
            pl.BlockSpec((SCAN_SLABS, LANES, 2 * HALF_STATES), lambda s, c: (s, 0, 0)),
            pl.BlockSpec((SCAN_SLABS, 2 * HALF_STATES, LANES), lambda s, c: (s, 0, 0)),
            pl.BlockSpec((SCAN_SLABS, SUBLANES, HALF_STATES), lambda s, c: (s, 0, 0)),
            pl.BlockSpec((SCAN_SLABS, SUBLANES, HALF_STATES), lambda s, c: (s, 0, 0)),
            pl.BlockSpec((1, width), lambda s, c: (0, s)),
        ],
        out_specs=pl.BlockSpec((BATCH, SCAN_T, width), lambda s, c: (0, c, s)),
        out_shape=jax.ShapeDtypeStruct((BATCH, SEQ, D_MODEL), BF16),
        scratch_shapes=[
            pltpu.VMEM((SCAN_SLABS, rows, LANES), F32),
            pair_buf, pair_buf, pair_buf, pair_buf,
            pltpu.VMEM((SCAN_SLABS, rows, LANES), F32),
            pltpu.VMEM((SCAN_SLABS, SUBLANES, 2 * HALF_STATES), F32),
        ],
        compiler_params=_params(("parallel", "arbitrary")),
        name="s5_scan",
    )(u, bmat, cmat, a_r, a_i, d)


def _s5_scan_operands(a_r, a_i, bb_r, bb_i, c_re, c_im):
    halves = SLAB_GROUPS // 2
    eye = jnp.eye(halves, dtype=F32)

    def b_blocks(bb):
        bb = bb.reshape(SSM_GROUP_DIM, N_SLABS, 2, halves, SSM_STATE)
        blk = jnp.einsum('hsqgp,gk->sqghkp', bb, eye)
        return blk.reshape(N_SLABS, LANES, HALF_STATES)

    def c_blocks(c):
        c = c.reshape(N_SLABS, 2, halves, SSM_GROUP_DIM, SSM_STATE)
        blk = jnp.einsum('sqghp,gk->skpqgh', c, eye)
        return blk.reshape(N_SLABS, HALF_STATES, LANES)

    bmat = jnp.concatenate([b_blocks(bb_r), b_blocks(bb_i)], axis=-1).astype(BF16)
    cmat = jnp.concatenate([c_blocks(c_re), -c_blocks(c_im)], axis=1).astype(BF16)

    def a_rows(a):
        a = a.reshape(N_SLABS, 2, 1, HALF_STATES)
        return jnp.broadcast_to(a, (N_SLABS, 2, BATCH, HALF_STATES)).reshape(
            N_SLABS, SUBLANES, HALF_STATES)

    return bmat, cmat, a_rows(a_r), a_rows(a_i)


def _s5_out_body(x_ref, a_ref, w_ref, o_ref, *, tn):
    a = a_ref[...]
    for c in range(0, D_MODEL, tn):
        val = jnp.dot(a, w_ref[:, c:c + tn], preferred_element_type=F32)
        gate = jnp.dot(a, w_ref[:, D_MODEL + c:D_MODEL + c + tn], preferred_element_type=F32)
        o_ref[:, c:c + tn] = x_ref[:, c:c + tn] + val * _sigmoid(gate)


def _s5_out(x, act, w, *, tm=512, tn=512):
    return pl.pallas_call(
        functools.partial(_s5_out_body, tn=tn),
        grid=(TOKENS // tm,),
        in_specs=[
            pl.BlockSpec((tm, D_MODEL), lambda i: (i, 0)),
            pl.BlockSpec((tm, D_MODEL), lambda i: (i, 0)),
            _resident((D_MODEL, 2 * D_MODEL)),
        ],
        out_specs=pl.BlockSpec((tm, D_MODEL), lambda i: (i, 0)),
        out_shape=jax.ShapeDtypeStruct((TOKENS, D_MODEL), F32),
        compiler_params=_params(("parallel",)),
        name="s5_out",
    )(x, act, w)


def _s5(x, norm_g, layer, w_in, lam_re, lam_im, log_dt, b_re, b_im, c_re, c_im, d_skip, w_out):
    a_r, a_i, bb_r, bb_i = _s5_disc(lam_re, lam_im, log_dt, b_re, b_im)
    bmat, cmat, ar_rows, ai_rows = _s5_scan_operands(a_r, a_i, bb_r, bb_i, c_re, c_im)
    u = _s5_in(x, norm_g, w_in, layer).reshape(BATCH, SEQ, D_MODEL)
    act = _s5_scan(u, bmat, cmat, ar_rows, ai_rows, d_skip.reshape(1, D_MODEL))
    return _s5_out(x, act.reshape(TOKENS, D_MODEL), w_out)


def _ple_body(x_ref, g_ref, wg_ref, p_ref, wp_ref, fg_ref, o_ref, wgb_ref, wpb_ref, *, final):
    @pl.when(pl.program_id(0) == 0)
    def _():
        wgb_ref[...] = wg_ref[...].astype(BF16)
        wpb_ref[...] = wp_ref[...].astype(BF16)

    x = x_ref[...]
    h = _rms(x, g_ref[...]).astype(BF16)
    gate = _sigmoid(jnp.dot(h, wgb_ref[...], preferred_element_type=F32))
    proj = jnp.dot(p_ref[...].astype(BF16), wpb_ref[...], preferred_element_type=F32)
    y = x + gate * proj
    o_ref[...] = _rms(y, fg_ref[...]) if final else y


def _ple(x, norm_g, wg, p, wp, fg, layer, *, final, tm=512):
    once = pl.Buffered(1)
    return pl.pallas_call(
        functools.partial(_ple_body, final=final),
        grid=(TOKENS // tm,),
        in_specs=[
            pl.BlockSpec((tm, D_MODEL), lambda i: (i, 0)),
            pl.BlockSpec((None, None, 1, D_MODEL), lambda i: (layer, 3, 0, 0)),
            pl.BlockSpec((None, D_MODEL, D_MODEL), lambda i: (layer, 0, 0), pipeline_mode=once),
            pl.BlockSpec((None, tm, PLE_DIM), lambda i: (layer, i, 0)),
            pl.BlockSpec((None, PLE_DIM, D_MODEL), lambda i: (layer, 0, 0), pipeline_mode=once),
            pl.BlockSpec((1, D_MODEL), lambda i: (0, 0)),
        ],
        out_specs=pl.BlockSpec((tm, D_MODEL), lambda i: (i, 0)),
        out_shape=jax.ShapeDtypeStruct((TOKENS, D_MODEL), F32),
        scratch_shapes=[pltpu.VMEM((D_MODEL, D_MODEL), BF16), pltpu.VMEM((PLE_DIM, D_MODEL), BF16)],
        compiler_params=_params(("arbitrary",)),
        name="ple",
    )(x, norm_g, wg, p, wp, fg)


def kernel(x, p, norm_g, final_norm_g, ffn_w_gate, ffn_w_up, ffn_w_down, gmlp_w_in, gmlp_ln_g, gmlp_ln_b, gmlp_w_s, gmlp_b_s, gmlp_w_out, s5_w_in, s5_lam_re, s5_lam_im, s5_log_dt, s5_b_re, s5_b_im, s5_c_re, s5_c_im, s5_d, s5_w_out, ple_w_gate, ple_w_proj):
    bf = lambda w: w.astype(BF16)
    x = x.reshape(TOKENS, D_MODEL)
    p = p.reshape(DEPTH, TOKENS, PLE_DIM)
    norm_g = norm_g.reshape(DEPTH, 4, 1, D_MODEL)
    fg = final_norm_g.reshape(1, D_MODEL)
    for i in range(DEPTH):
        x = _ffn(x, norm_g, ffn_w_gate, ffn_w_up, ffn_w_down, i, 0)
        j = i // 2
        if i % 2 == 0:
            x = _gmlp(x, norm_g, bf(gmlp_w_in[j]), gmlp_ln_g[j].reshape(1, GMLP_HALF),
                      gmlp_ln_b[j].reshape(1, GMLP_HALF), gmlp_w_s[j],
                      gmlp_b_s[j].reshape(GMLP_GROUPS, CHUNK, 1), bf(gmlp_w_out[j]), i)
        else:
            x = _s5(x, norm_g, i, s5_w_in[j], s5_lam_re[j], s5_lam_im[j], s5_log_dt[j],
                    s5_b_re[j], s5_b_im[j], s5_c_re[j], s5_c_im[j], s5_d[j], bf(s5_w_out[j]))
        x = _ffn(x, norm_g, ffn_w_gate, ffn_w_up, ffn_w_down, i, 1)
        x = _ple(x, norm_g, ple_w_gate, p, ple_w_proj, fg, i, final=(i == DEPTH - 1))
    return x.reshape(BATCH, SEQ, D_MODEL)
```

```python
import functools
import math

import jax
import jax.numpy as jnp
from jax import lax
from jax.experimental import pallas as pl
from jax.experimental.pallas import tpu as pltpu

F32 = jnp.float32
BF16 = jnp.bfloat16

D_MODEL = 2048
BATCH = 4
SEQ = 2048
TOKENS = BATCH * SEQ
DEPTH = 2
D_FF = 5632
PLE_DIM = 256
CHUNK = 128
GMLP_HALF = 3 * D_MODEL
GMLP_GROUPS = 16
GMLP_GROUP_DIM = GMLP_HALF // GMLP_GROUPS
SSM_GROUP_DIM = 16
SSM_GROUPS = D_MODEL // SSM_GROUP_DIM
SSM_STATE = 64
EPS = 1e-6

LANES = 128
SUBLANES = 8
VMEM_LIMIT = 60 * 1024 * 1024

SLAB_GROUPS = LANES // SSM_GROUP_DIM
N_SLABS = D_MODEL // LANES
SLAB_STATES = SLAB_GROUPS * SSM_STATE
HALF_STATES = SLAB_STATES // 2
HALF_CH = LANES // 2
SCAN_PAIR = 2
SCAN_SLABS = 2 * SCAN_PAIR
SCAN_T = 128


def _rms(x, g):
    ms = jnp.mean(x * x, axis=-1, keepdims=True)
    return x * lax.rsqrt(ms + EPS) * g


def _gelu(x):
    c = math.sqrt(2.0 / math.pi)
    return x * (0.5 * (1.0 + jnp.tanh(c * (x + 0.044715 * (x * x * x)))))


def _sigmoid(x):
    return 1.0 / (1.0 + jnp.exp(-x))


def _params(sem):
    return pltpu.CompilerParams(dimension_semantics=sem, vmem_limit_bytes=VMEM_LIMIT)


def _gain_spec(layer, slot):
    return pl.BlockSpec((None, None, 1, D_MODEL), lambda i, j: (layer, slot, 0, 0))


def _ffn_body(*refs, tm, n_side):
    x_hbm, g_ref, wg_ref, wu_ref, wd_ref = refs[:5]
    side_in = refs[5:5 + n_side]
    o_ref = refs[5 + n_side]
    side_out = refs[6 + n_side:6 + 2 * n_side]
    h_ref, x_ref, sem = refs[6 + 2 * n_side:]
    i, j = pl.program_id(0), pl.program_id(1)

    def x_copy(tile):
        rows = pl.ds(pl.multiple_of(tile * tm, tm), tm)
        return pltpu.make_async_copy(x_hbm.at[rows, :], x_ref, sem)

    @pl.when(j == 0)
    def _():
        @pl.when(i == 0)
        def _():
            x_copy(i).start()

        x_copy(i).wait()
        x = x_ref[...]
        h_ref[...] = _rms(x, g_ref[...]).astype(BF16)
        o_ref[...] = x

    @pl.when(jnp.logical_and(j == 1, i + 1 < pl.num_programs(0)))
    def _():
        x_copy(i + 1).start()

    for src, dst in zip(side_in, side_out):
        dst[...] = src[...].astype(BF16)

    h = h_ref[...]
    gate = jnp.dot(h, wg_ref[...].astype(BF16), preferred_element_type=F32)
    up = jnp.dot(h, wu_ref[...].astype(BF16), preferred_element_type=F32)
    a = (gate * _sigmoid(gate)) * (0.5 * up)
    o_ref[...] += jnp.dot(a.astype(BF16), wd_ref[...].astype(BF16), preferred_element_type=F32)


def _ffn(x, norm_g, wg, wu, wd, layer, which, *, side=(), tm=1024, tf=512):
    ni, nj = TOKENS // tm, D_FF // tf
    side_specs, side_shapes = [], []
    for arr, n_blocks in side:
        assert n_blocks <= ni * nj and arr.shape[0] % n_blocks == 0
        block = (arr.shape[0] // n_blocks, arr.shape[1])
        assert block[0] % (2 * SUBLANES) == 0
        index = functools.partial(
            lambda i, j, last: (jnp.minimum(i * nj + j, last), 0), last=n_blocks - 1)
        side_specs.append(pl.BlockSpec(block, index))
        side_shapes.append(jax.ShapeDtypeStruct(arr.shape, BF16))
    out = pl.pallas_call(
        functools.partial(_ffn_body, tm=tm, n_side=len(side)),
        grid=(ni, nj),
        in_specs=[
            pl.BlockSpec(memory_space=pl.ANY),
            _gain_spec(layer, 2 * which),
            pl.BlockSpec((None, None, D_MODEL, tf), lambda i, j: (layer, which, 0, j)),
            pl.BlockSpec((None, None, D_MODEL, tf), lambda i, j: (layer, which, 0, j)),
            pl.BlockSpec((None, None, tf, D_MODEL), lambda i, j: (layer, which, j, 0)),
        ] + side_specs,
        out_specs=[pl.BlockSpec((tm, D_MODEL), lambda i, j: (i, 0))] + side_specs,
        out_shape=[jax.ShapeDtypeStruct((TOKENS, D_MODEL), F32)] + side_shapes,
        scratch_shapes=[pltpu.VMEM((tm, D_MODEL), BF16), pltpu.VMEM((tm, D_MODEL), F32),
                        pltpu.SemaphoreType.DMA(())],
        compiler_params=_params(("arbitrary", "arbitrary")),
        name="ffn",
    )(x, norm_g, wg, wu, wd, *[arr for arr, _ in side])
    return out[0], out[1:]


GMLP_STEP_GROUPS = 2
GMLP_CAST_BLOCKS = 128
GMLP_STEP_W = GMLP_STEP_GROUPS * GMLP_GROUP_DIM


def _gmlp_body(x_ref, g_ref, wu_ref, wv_ref, lng_ref, lnb_ref, ws_ref, bs_ref, wo_ref,
               o_ref, h_ref, gated_ref, *, tm):
    @pl.when(pl.program_id(1) == 0)
    def _():
        x = x_ref[...]
        h_ref[...] = _rms(x, g_ref[...]).astype(BF16)
        o_ref[...] = x

    h = h_ref[...]
    zv = _gelu(jnp.dot(h, wv_ref[...], preferred_element_type=F32))
    zu = _gelu(jnp.dot(h, wu_ref[...], preferred_element_type=F32))
    row = lax.broadcasted_iota(jnp.int32, (CHUNK, CHUNK), 0)
    col = lax.broadcasted_iota(jnp.int32, (CHUNK, CHUNK), 1)
    for gi in range(GMLP_STEP_GROUPS):
        lo, hi = gi * GMLP_GROUP_DIM, (gi + 1) * GMLP_GROUP_DIM
        v = zv[:, lo:hi]
        d = v - jnp.mean(v, axis=-1, keepdims=True)
        var = jnp.mean(d * d, axis=-1, keepdims=True)
        vn = ((d * lax.rsqrt(var + EPS)) * lng_ref[:, lo:hi] + lnb_ref[:, lo:hi]).astype(BF16)
        ws = jnp.where(row >= col, ws_ref[gi], 0.0).astype(BF16)
        bias = bs_ref[gi]
        for c in range(tm // CHUNK):
            rows = slice(c * CHUNK, (c + 1) * CHUNK)
            sv = jnp.dot(ws, vn[rows], preferred_element_type=F32) + bias
            gated_ref[rows, lo:hi] = (zu[rows, lo:hi] * sv).astype(BF16)
    o_ref[...] += jnp.dot(gated_ref[...], wo_ref[...], preferred_element_type=F32)


def _gmlp(x, norm_g, w_in, ln_g, ln_b, w_s, b_s, w_out, layer, *, tm=512):
    nj = GMLP_GROUPS // GMLP_STEP_GROUPS
    return pl.pallas_call(
        functools.partial(_gmlp_body, tm=tm),
        grid=(TOKENS // tm, nj),
        in_specs=[
            pl.BlockSpec((tm, D_MODEL), lambda i, j: (i, 0)),
            _gain_spec(layer, 1),
            pl.BlockSpec((D_MODEL, GMLP_STEP_W), lambda i, j: (0, j)),
            pl.BlockSpec((D_MODEL, GMLP_STEP_W), lambda i, j: (0, j + nj)),
            pl.BlockSpec((1, GMLP_STEP_W), lambda i, j: (0, j)),
            pl.BlockSpec((1, GMLP_STEP_W), lambda i, j: (0, j)),
            pl.BlockSpec((GMLP_STEP_GROUPS, CHUNK, CHUNK), lambda i, j: (j, 0, 0)),
            pl.BlockSpec((GMLP_STEP_GROUPS, CHUNK, 1), lambda i, j: (j, 0, 0)),
            pl.BlockSpec((GMLP_STEP_W, D_MODEL), lambda i, j: (j, 0)),
        ],
        out_specs=pl.BlockSpec((tm, D_MODEL), lambda i, j: (i, 0)),
        out_shape=jax.ShapeDtypeStruct((TOKENS, D_MODEL), F32),
        scratch_shapes=[pltpu.VMEM((tm, D_MODEL), BF16), pltpu.VMEM((tm, GMLP_STEP_W), BF16)],
        compiler_params=_params(("parallel", "arbitrary")),
        name="gmlp",
    )(x, norm_g, w_in, w_in, ln_g, ln_b, w_s, b_s, w_out)


def _s5_disc_body(lr_ref, li_ref, ldt_ref, br_ref, bi_ref, ar_ref, ai_ref, bbr_ref, bbi_ref):
    lr, li = lr_ref[...], li_ref[...]
    dt = jnp.exp(ldt_ref[...])
    mag = jnp.exp(lr * dt)
    ang = li * dt
    a_r = mag * jnp.cos(ang)
    a_i = mag * jnp.sin(ang)
    den = lr * lr + li * li
    nr = a_r - 1.0
    z_r = (nr * lr + a_i * li) / den
    z_i = (a_i * lr - nr * li) / den
    ar_ref[...] = a_r
    ai_ref[...] = a_i
    bbr_ref[...] = z_r * br_ref[...] - z_i * bi_ref[...]
    bbi_ref[...] = z_r * bi_ref[...] + z_i * br_ref[...]


def _s5_disc(lam_re, lam_im, log_dt, b_re, b_im):
    n = SSM_GROUPS * SSM_STATE
    row = jax.ShapeDtypeStruct((1, n), F32)
    mat = jax.ShapeDtypeStruct((SSM_GROUP_DIM, n), F32)
    to_mat = lambda b: jnp.transpose(b, (2, 0, 1)).reshape(SSM_GROUP_DIM, n)
    ldt = jnp.broadcast_to(log_dt[:, None], (SSM_GROUPS, SSM_STATE)).reshape(1, n)
    return pl.pallas_call(_s5_disc_body, out_shape=(row, row, mat, mat), name="s5_disc")(
        lam_re.reshape(1, n), lam_im.reshape(1, n), ldt, to_mat(b_re), to_mat(b_im))


def _resident(shape):
    zeros = (0,) * len(shape)
    return pl.BlockSpec(shape, lambda i: zeros, pipeline_mode=pl.Buffered(1))


def _s5_in_body(x_ref, g_ref, w_ref, wo_ref, o_ref, wob_ref, wb_ref):
    @pl.when(pl.program_id(0) == 0)
    def _():
        wb_ref[...] = w_ref[...].astype(BF16)

    wob_ref[...] = wo_ref[...].astype(BF16)
    h = _rms(x_ref[...], g_ref[...]).astype(BF16)
    o_ref[...] = jnp.dot(h, wb_ref[...], preferred_element_type=F32)


def _s5_in(x, norm_g, w, w_out, layer, *, tm=512):
    steps = TOKENS // tm
    wo_block = pl.BlockSpec((D_MODEL // steps, 2 * D_MODEL), lambda i: (i, 0))
    return pl.pallas_call(
        _s5_in_body,
        grid=(steps,),
        in_specs=[
            pl.BlockSpec((tm, D_MODEL), lambda i: (i, 0)),
            pl.BlockSpec((None, None, 1, D_MODEL), lambda i: (layer, 1, 0, 0)),
            _resident((D_MODEL, D_MODEL)),
            wo_block,
        ],
        out_specs=[pl.BlockSpec((tm, D_MODEL), lambda i: (i, 0)), wo_block],
        out_shape=[jax.ShapeDtypeStruct((TOKENS, D_MODEL), F32),
                   jax.ShapeDtypeStruct((D_MODEL, 2 * D_MODEL), BF16)],
        scratch_shapes=[pltpu.VMEM((D_MODEL, D_MODEL), BF16)],
        compiler_params=_params(("arbitrary",)),
        name="s5_in",
    )(x, norm_g, w, w_out)


def _s5_scan_body(u_ref, bmat_ref, cmat_ref, ar_ref, ai_ref, d_ref, o_ref,
                  lhs_ref, bu0_ref, bu1_ref, hs0_ref, hs1_ref, yp_ref, carry_ref):
    tile = SUBLANES
    lane = lax.broadcasted_iota(jnp.int32, (tile, LANES), 1)
    low = lane < HALF_CH
    bu_refs = (bu0_ref, bu1_ref)
    hs_refs = (hs0_ref, hs1_ref)

    @pl.when(pl.program_id(1) == 0)
    def _():
        carry_ref[...] = jnp.zeros_like(carry_ref)

    for s in range(SCAN_SLABS):
        for b in range(BATCH):
            for t0 in range(0, SCAN_T, tile):
                v = u_ref[b, t0:t0 + tile, s * LANES:(s + 1) * LANES]
                lhs_ref[s, pl.ds(t0 * tile + b, tile, stride=tile), :] = jnp.where(low, v, 0.0)
                lhs_ref[s, pl.ds(t0 * tile + BATCH + b, tile, stride=tile), :] = jnp.where(low, 0.0, v)

    def to_states(pair):
        for k in range(SCAN_PAIR):
            s = pair * SCAN_PAIR + k
            bu_refs[pair][k] = jnp.dot(lhs_ref[s].astype(BF16), bmat_ref[s],
                                       preferred_element_type=F32)

    def scan(pair):
        bu_ref, hs_ref = bu_refs[pair], hs_refs[pair]
        re, im = slice(0, HALF_STATES), slice(HALF_STATES, 2 * HALF_STATES)
        hs = []
        for k in range(SCAN_PAIR):
            s = pair * SCAN_PAIR + k
            hs.append((ar_ref[s], ai_ref[s], carry_ref[s, :, re], carry_ref[s, :, im]))
        for t in range(SCAN_T):
            rows = slice(t * tile, (t + 1) * tile)
            for k in range(SCAN_PAIR):
                a_r, a_i, hr, hi = hs[k]
                hr, hi = ((a_r * hr - a_i * hi) + bu_ref[k, rows, re],
                          (a_r * hi + a_i * hr) + bu_ref[k, rows, im])
                hs_ref[k, rows, re] = hr
                hs_ref[k, rows, im] = hi
                hs[k] = (a_r, a_i, hr, hi)
        for k in range(SCAN_PAIR):
            s = pair * SCAN_PAIR + k
            carry_ref[s, :, re] = hs[k][2]
            carry_ref[s, :, im] = hs[k][3]

    def from_states(pair):
        for k in range(SCAN_PAIR):
            s = pair * SCAN_PAIR + k
            yp_ref[s] = jnp.dot(hs_refs[pair][k].astype(BF16), cmat_ref[s],
                                preferred_element_type=F32)

    to_states(0)
    to_states(1)
    scan(0)
    from_states(0)
    scan(1)
    from_states(1)

    for s in range(SCAN_SLABS):
        sl = slice(s * LANES, (s + 1) * LANES)
        dsk = d_ref[:, sl]
        for b in range(BATCH):
            for t0 in range(0, SCAN_T, 2 * tile):
                ys = []
                for t1 in (t0, t0 + tile):
                    y0 = yp_ref[s, pl.ds(t1 * tile + b, tile, stride=tile), :]
                    y1 = yp_ref[s, pl.ds(t1 * tile + BATCH + b, tile, stride=tile), :]
                    ys.append(jnp.where(low, y0, y1) + dsk * u_ref[b, t1:t1 + tile, sl])
                o_ref[b, t0:t0 + 2 * tile, sl] = _gelu(jnp.concatenate(ys, axis=0)).astype(BF16)


def _s5_scan(u, bmat, cmat, a_r, a_i, d):
    width = SCAN_SLABS * LANES
    rows = SCAN_T * SUBLANES
    pair_buf = pltpu.VMEM((SCAN_PAIR, rows, 2 * HALF_STATES), F32)
    return pl.pallas_call(
        _s5_scan_body,
        grid=(N_SLABS // SCAN_SLABS, SEQ // SCAN_T),
        in_specs=[
            pl.BlockSpec((BATCH, SCAN_T, width), lambda s, c: (0, c, s)),
            pl.BlockSpec((SCAN_SLABS, LANES, 2 * HALF_STATES), lambda s, c: (s, 0, 0)),
            pl.BlockSpec((SCAN_SLABS, 2 * HALF_STATES, LANES), lambda s, c: (s, 0, 0)),
            pl.BlockSpec((SCAN_SLABS, SUBLANES, HALF_STATES), lambda s, c: (s, 0, 0)),
            pl.BlockSpec((SCAN_SLABS, SUBLANES, HALF_STATES), lambda s, c: (s, 0, 0)),
            pl.BlockSpec((1, width), lambda s, c: (0, s)),
        ],
        out_specs=pl.BlockSpec((BATCH, SCAN_T, width), lambda s, c: (0, c, s)),
        out_shape=jax.ShapeDtypeStruct((BATCH, SEQ, D_MODEL), BF16),
        scratch_shapes=[
            pltpu.VMEM((SCAN_SLABS, rows, LANES), F32),
            pair_buf, pair_buf, pair_buf, pair_buf,
            pltpu.VMEM((SCAN_SLABS, rows, LANES), F32),
            pltpu.VMEM((SCAN_SLABS, SUBLANES, 2 * HALF_STATES), F32),
        ],
        compiler_params=_params(("parallel", "arbitrary")),
        name="s5_scan",
    )(u, bmat, cmat, a_r, a_i, d)


def _s5_scan_operands(a_r, a_i, bb_r, bb_i, c_re, c_im):
    halves = SLAB_GROUPS // 2
    eye = jnp.eye(halves, dtype=F32)

    def b_blocks(bb):
        bb = bb.reshape(SSM_GROUP_DIM, N_SLABS, 2, halves, SSM_STATE)
        blk = jnp.einsum('hsqgp,gk->sqghkp', bb, eye)
        return blk.reshape(N_SLABS, LANES, HALF_STATES)

    def c_blocks(c):
        c = c.reshape(N_SLABS, 2, halves, SSM_GROUP_DIM, SSM_STATE)
        blk = jnp.einsum('sqghp,gk->skpqgh', c, eye)
        return blk.reshape(N_SLABS, HALF_STATES, LANES)

    bmat = jnp.concatenate([b_blocks(bb_r), b_blocks(bb_i)], axis=-1).astype(BF16)
    cmat = jnp.concatenate([c_blocks(c_re), -c_blocks(c_im)], axis=1).astype(BF16)

    def a_rows(a):
        a = a.reshape(N_SLABS, 2, 1, HALF_STATES)
        return jnp.broadcast_to(a, (N_SLABS, 2, BATCH, HALF_STATES)).reshape(
            N_SLABS, SUBLANES, HALF_STATES)

    return bmat, cmat, a_rows(a_r), a_rows(a_i)


def _s5_out_body(x_ref, a_ref, w_ref, o_ref, *, tn):
    a = a_ref[...]
    for c in range(0, D_MODEL, tn):
        val = jnp.dot(a, w_ref[:, c:c + tn], preferred_element_type=F32)
        gate = jnp.dot(a, w_ref[:, D_MODEL + c:D_MODEL + c + tn], preferred_element_type=F32)
        o_ref[:, c:c + tn] = x_ref[:, c:c + tn] + val * _sigmoid(gate)


def _s5_out(x, act, w, *, tm=512, tn=512):
    return pl.pallas_call(
        functools.partial(_s5_out_body, tn=tn),
        grid=(TOKENS // tm,),
        in_specs=[
            pl.BlockSpec((tm, D_MODEL), lambda i: (i, 0)),
            pl.BlockSpec((tm, D_MODEL), lambda i: (i, 0)),
            _resident((D_MODEL, 2 * D_MODEL)),
        ],
        out_specs=pl.BlockSpec((tm, D_MODEL), lambda i: (i, 0)),
        out_shape=jax.ShapeDtypeStruct((TOKENS, D_MODEL), F32),
        compiler_params=_params(("parallel",)),
        name="s5_out",
    )(x, act, w)


def _s5(x, norm_g, layer, w_in, lam_re, lam_im, log_dt, b_re, b_im, c_re, c_im, d_skip, w_out):
    a_r, a_i, bb_r, bb_i = _s5_disc(lam_re, lam_im, log_dt, b_re, b_im)
    bmat, cmat, ar_rows, ai_rows = _s5_scan_operands(a_r, a_i, bb_r, bb_i, c_re, c_im)
    u, w_out_bf = _s5_in(x, norm_g, w_in, w_out, layer)
    act = _s5_scan(u.reshape(BATCH, SEQ, D_MODEL), bmat, cmat, ar_rows, ai_rows,
                   d_skip.reshape(1, D_MODEL))
    return _s5_out(x, act.reshape(TOKENS, D_MODEL), w_out_bf)


def _ple_body(x_ref, g_ref, wg_ref, p_ref, wp_ref, fg_ref, o_ref, wgb_ref, wpb_ref, *, final):
    @pl.when(pl.program_id(0) == 0)
    def _():
        wgb_ref[...] = wg_ref[...].astype(BF16)
        wpb_ref[...] = wp_ref[...].astype(BF16)

    x = x_ref[...]
    h = _rms(x, g_ref[...]).astype(BF16)
    gate = _sigmoid(jnp.dot(h, wgb_ref[...], preferred_element_type=F32))
    proj = jnp.dot(p_ref[...].astype(BF16), wpb_ref[...], preferred_element_type=F32)
    y = x + gate * proj
    o_ref[...] = _rms(y, fg_ref[...]) if final else y


def _ple(x, norm_g, wg, p, wp, fg, layer, *, final, tm=512):
    once = pl.Buffered(1)
    return pl.pallas_call(
        functools.partial(_ple_body, final=final),
        grid=(TOKENS // tm,),
        in_specs=[
            pl.BlockSpec((tm, D_MODEL), lambda i: (i, 0)),
            pl.BlockSpec((None, None, 1, D_MODEL), lambda i: (layer, 3, 0, 0)),
            pl.BlockSpec((None, D_MODEL, D_MODEL), lambda i: (layer, 0, 0), pipeline_mode=once),
            pl.BlockSpec((None, tm, PLE_DIM), lambda i: (layer, i, 0)),
            pl.BlockSpec((None, PLE_DIM, D_MODEL), lambda i: (layer, 0, 0), pipeline_mode=once),
            pl.BlockSpec((1, D_MODEL), lambda i: (0, 0)),
        ],
        out_specs=pl.BlockSpec((tm, D_MODEL), lambda i: (i, 0)),
        out_shape=jax.ShapeDtypeStruct((TOKENS, D_MODEL), F32),
        scratch_shapes=[pltpu.VMEM((D_MODEL, D_MODEL), BF16), pltpu.VMEM((PLE_DIM, D_MODEL), BF16)],
        compiler_params=_params(("arbitrary",)),
        name="ple",
    )(x, norm_g, wg, p, wp, fg)


def kernel(x, p, norm_g, final_norm_g, ffn_w_gate, ffn_w_up, ffn_w_down, gmlp_w_in, gmlp_ln_g, gmlp_ln_b, gmlp_w_s, gmlp_b_s, gmlp_w_out, s5_w_in, s5_lam_re, s5_lam_im, s5_log_dt, s5_b_re, s5_b_im, s5_c_re, s5_c_im, s5_d, s5_w_out, ple_w_gate, ple_w_proj):
    x = x.reshape(TOKENS, D_MODEL)
    p = p.reshape(DEPTH, TOKENS, PLE_DIM)
    norm_g = norm_g.reshape(DEPTH, 4, 1, D_MODEL)
    fg = final_norm_g.reshape(1, D_MODEL)
    for i in range(DEPTH):
        j = i // 2
        if i % 2 == 0:
            side = ((gmlp_w_in[j], GMLP_CAST_BLOCKS), (gmlp_w_out[j], GMLP_CAST_BLOCKS))
            x, (w_in, w_out) = _ffn(x, norm_g, ffn_w_gate, ffn_w_up, ffn_w_down, i, 0,
                                    side=side, tf=256)
            x = _gmlp(x, norm_g, w_in, gmlp_ln_g[j].reshape(1, GMLP_HALF),
                      gmlp_ln_b[j].reshape(1, GMLP_HALF), gmlp_w_s[j],
                      gmlp_b_s[j].reshape(GMLP_GROUPS, CHUNK, 1), w_out, i)
        else:
            x, _ = _ffn(x, norm_g, ffn_w_gate, ffn_w_up, ffn_w_down, i, 0)
            x = _s5(x, norm_g, i, s5_w_in[j], s5_lam_re[j], s5_lam_im[j], s5_log_dt[j],
                    s5_b_re[j], s5_b_im[j], s5_c_re[j], s5_c_im[j], s5_d[j], s5_w_out[j])
        x, _ = _ffn(x, norm_g, ffn_w_gate, ffn_w_up, ffn_w_down, i, 1)
        x = _ple(x, norm_g, ple_w_gate, p, ple_w_proj, fg, i, final=(i == DEPTH - 1))
    return x.reshape(BATCH, SEQ, D_MODEL)
```

```python
import functools
import math

import jax
import jax.numpy as jnp
from jax import lax
from jax.experimental import pallas as pl
from jax.experimental.pallas import tpu as pltpu

F32 = jnp.float32
BF16 = jnp.bfloat16

D_MODEL = 2048
BATCH = 4
SEQ = 2048
TOKENS = BATCH * SEQ
DEPTH = 2
D_FF = 5632
PLE_DIM = 256
CHUNK = 128
GMLP_HALF = 3 * D_MODEL
GMLP_GROUPS = 16
GMLP_GROUP_DIM = GMLP_HALF // GMLP_GROUPS
SSM_GROUP_DIM = 16
SSM_GROUPS = D_MODEL // SSM_GROUP_DIM
SSM_STATE = 64
EPS = 1e-6

LANES = 128
SUBLANES = 8
VMEM_LIMIT = 60 * 1024 * 1024

SLAB_GROUPS = LANES // SSM_GROUP_DIM
N_SLABS = D_MODEL // LANES
SLAB_STATES = SLAB_GROUPS * SSM_STATE
HALF_STATES = SLAB_STATES // 2
HALF_CH = LANES // 2
SCAN_PAIR = 2
SCAN_SLABS = 2 * SCAN_PAIR
SCAN_T = 128
SCAN_PHASES = 2


def _rms(x, g):
    ms = jnp.mean(x * x, axis=-1, keepdims=True)
    return x * lax.rsqrt(ms + EPS) * g


def _gelu(x):
    c = math.sqrt(2.0 / math.pi)
    return x * (0.5 * (1.0 + jnp.tanh(c * (x + 0.044715 * (x * x * x)))))


def _sigmoid(x):
    return 1.0 / (1.0 + jnp.exp(-x))


def _params(sem):
    return pltpu.CompilerParams(dimension_semantics=sem, vmem_limit_bytes=VMEM_LIMIT)


def _gain_spec(layer, slot):
    return pl.BlockSpec((None, None, 1, D_MODEL), lambda i, j: (layer, slot, 0, 0))


FFN_FIRST_STEP_SUBTILES = 2

def _ffn_body(*refs, tm, n_side):
    x_hbm, g_ref, wg_ref, wu_ref, wd_ref = refs[:5]
    side_in = refs[5:5 + n_side]
    o_ref = refs[5 + n_side]
    side_out = refs[6 + n_side:6 + 2 * n_side]
    h_ref, x_ref, sem = refs[6 + 2 * n_side:]
    i, j = pl.program_id(0), pl.program_id(1)

    def x_copy(tile):
        rows = pl.ds(pl.multiple_of(tile * tm, tm), tm)
        return pltpu.make_async_copy(x_hbm.at[rows, :], x_ref, sem)

    def side_casts():
        for src, dst in zip(side_in, side_out):
            dst[...] = src[...].astype(BF16)

    def swiglu_chunk(h, weights):
        wg, wu, wd = weights
        gate = jnp.dot(h, wg, preferred_element_type=F32)
        up = jnp.dot(h, wu, preferred_element_type=F32)
        a = (gate * _sigmoid(gate)) * (0.5 * up)
        return jnp.dot(a.astype(BF16), wd, preferred_element_type=F32)

    def cast_weights():
        return (wg_ref[...].astype(BF16), wu_ref[...].astype(BF16), wd_ref[...].astype(BF16))

    @pl.when(j == 0)
    def _():
        @pl.when(i == 0)
        def _():
            x_copy(i).start()

        x_copy(i).wait()
        side_casts()
        weights = cast_weights()
        sub = tm // FFN_FIRST_STEP_SUBTILES
        for r in range(0, tm, sub):
            x = x_ref[r:r + sub, :]
            h = _rms(x, g_ref[...]).astype(BF16)
            h_ref[r:r + sub, :] = h
            o_ref[r:r + sub, :] = x + swiglu_chunk(h, weights)

    @pl.when(jnp.logical_and(j == 1, i + 1 < pl.num_programs(0)))
    def _():
        x_copy(i + 1).start()

    @pl.when(j > 0)
    def _():
        side_casts()
        o_ref[...] += swiglu_chunk(h_ref[...], cast_weights())


def _ffn(x, norm_g, wg, wu, wd, layer, which, *, side=(), tm=1024, tf=512):
    ni, nj = TOKENS // tm, D_FF // tf
    side_specs, side_shapes = [], []
    for arr, n_blocks in side:
        assert n_blocks <= ni * nj and arr.shape[0] % n_blocks == 0
        block = (arr.shape[0] // n_blocks, arr.shape[1])
        assert block[0] % (2 * SUBLANES) == 0
        index = functools.partial(
            lambda i, j, last: (jnp.minimum(i * nj + j, last), 0), last=n_blocks - 1)
        side_specs.append(pl.BlockSpec(block, index))
        side_shapes.append(jax.ShapeDtypeStruct(arr.shape, BF16))
    out = pl.pallas_call(
        functools.partial(_ffn_body, tm=tm, n_side=len(side)),
        grid=(ni, nj),
        in_specs=[
            pl.BlockSpec(memory_space=pl.ANY),
            _gain_spec(layer, 2 * which),
            pl.BlockSpec((None, None, D_MODEL, tf), lambda i, j: (layer, which, 0, j)),
            pl.BlockSpec((None, None, D_MODEL, tf), lambda i, j: (layer, which, 0, j)),
            pl.BlockSpec((None, None, tf, D_MODEL), lambda i, j: (layer, which, j, 0)),
        ] + side_specs,
        out_specs=[pl.BlockSpec((tm, D_MODEL), lambda i, j: (i, 0))] + side_specs,
        out_shape=[jax.ShapeDtypeStruct((TOKENS, D_MODEL), F32)] + side_shapes,
        scratch_shapes=[pltpu.VMEM((tm, D_MODEL), BF16), pltpu.VMEM((tm, D_MODEL), F32),
                        pltpu.SemaphoreType.DMA(())],
        compiler_params=_params(("arbitrary", "arbitrary")),
        name="ffn",
    )(x, norm_g, wg, wu, wd, *[arr for arr, _ in side])
    return out[0], out[1:]


GMLP_STEP_GROUPS = 2
GMLP_CAST_BLOCKS = 128
GMLP_STEP_W = GMLP_STEP_GROUPS * GMLP_GROUP_DIM


def _gmlp_body(x_ref, g_ref, wu_ref, wv_ref, lng_ref, lnb_ref, ws_ref, bs_ref, wo_ref,
               o_ref, h_ref, gated_ref, *, tm):
    @pl.when(pl.program_id(1) == 0)
    def _():
        x = x_ref[...]
        h_ref[...] = _rms(x, g_ref[...]).astype(BF16)
        o_ref[...] = x

    h = h_ref[...]
    zv = _gelu(jnp.dot(h, wv_ref[...], preferred_element_type=F32))
    zu = _gelu(jnp.dot(h, wu_ref[...], preferred_element_type=F32))
    row = lax.broadcasted_iota(jnp.int32, (CHUNK, CHUNK), 0)
    col = lax.broadcasted_iota(jnp.int32, (CHUNK, CHUNK), 1)
    for gi in range(GMLP_STEP_GROUPS):
        lo, hi = gi * GMLP_GROUP_DIM, (gi + 1) * GMLP_GROUP_DIM
        v = zv[:, lo:hi]
        d = v - jnp.mean(v, axis=-1, keepdims=True)
        var = jnp.mean(d * d, axis=-1, keepdims=True)
        vn = ((d * lax.rsqrt(var + EPS)) * lng_ref[:, lo:hi] + lnb_ref[:, lo:hi]).astype(BF16)
        ws = jnp.where(row >= col, ws_ref[gi], 0.0).astype(BF16)
        bias = bs_ref[gi]
        for c in range(tm // CHUNK):
            rows = slice(c * CHUNK, (c + 1) * CHUNK)
            sv = jnp.dot(ws, vn[rows], preferred_element_type=F32) + bias
            gated_ref[rows, lo:hi] = (zu[rows, lo:hi] * sv).astype(BF16)
    o_ref[...] += jnp.dot(gated_ref[...], wo_ref[...], preferred_element_type=F32)


def _gmlp(x, norm_g, w_in, ln_g, ln_b, w_s, b_s, w_out, layer, *, tm=512):
    nj = GMLP_GROUPS // GMLP_STEP_GROUPS
    return pl.pallas_call(
        functools.partial(_gmlp_body, tm=tm),
        grid=(TOKENS // tm, nj),
        in_specs=[
            pl.BlockSpec((tm, D_MODEL), lambda i, j: (i, 0)),
            _gain_spec(layer, 1),
            pl.BlockSpec((D_MODEL, GMLP_STEP_W), lambda i, j: (0, j)),
            pl.BlockSpec((D_MODEL, GMLP_STEP_W), lambda i, j: (0, j + nj)),
            pl.BlockSpec((1, GMLP_STEP_W), lambda i, j: (0, j)),
            pl.BlockSpec((1, GMLP_STEP_W), lambda i, j: (0, j)),
            pl.BlockSpec((GMLP_STEP_GROUPS, CHUNK, CHUNK), lambda i, j: (j, 0, 0)),
            pl.BlockSpec((GMLP_STEP_GROUPS, CHUNK, 1), lambda i, j: (j, 0, 0)),
            pl.BlockSpec((GMLP_STEP_W, D_MODEL), lambda i, j: (j, 0)),
        ],
        out_specs=pl.BlockSpec((tm, D_MODEL), lambda i, j: (i, 0)),
        out_shape=jax.ShapeDtypeStruct((TOKENS, D_MODEL), F32),
        scratch_shapes=[pltpu.VMEM((tm, D_MODEL), BF16), pltpu.VMEM((tm, GMLP_STEP_W), BF16)],
        compiler_params=_params(("parallel", "arbitrary")),
        name="gmlp",
    )(x, norm_g, w_in, w_in, ln_g, ln_b, w_s, b_s, w_out)


def _s5_disc_body(lr_ref, li_ref, ldt_ref, br_ref, bi_ref, ar_ref, ai_ref, bbr_ref, bbi_ref):
    lr, li = lr_ref[...], li_ref[...]
    dt = jnp.exp(ldt_ref[...])
    mag = jnp.exp(lr * dt)
    ang = li * dt
    a_r = mag * jnp.cos(ang)
    a_i = mag * jnp.sin(ang)
    den = lr * lr + li * li
    nr = a_r - 1.0
    z_r = (nr * lr + a_i * li) / den
    z_i = (a_i * lr - nr * li) / den
    ar_ref[...] = a_r
    ai_ref[...] = a_i
    bbr_ref[...] = z_r * br_ref[...] - z_i * bi_ref[...]
    bbi_ref[...] = z_r * bi_ref[...] + z_i * br_ref[...]


def _s5_disc(lam_re, lam_im, log_dt, b_re, b_im):
    n = SSM_GROUPS * SSM_STATE
    row = jax.ShapeDtypeStruct((1, n), F32)
    mat = jax.ShapeDtypeStruct((SSM_GROUP_DIM, n), F32)
    to_mat = lambda b: jnp.transpose(b, (2, 0, 1)).reshape(SSM_GROUP_DIM, n)
    ldt = jnp.broadcast_to(log_dt[:, None], (SSM_GROUPS, SSM_STATE)).reshape(1, n)
    return pl.pallas_call(_s5_disc_body, out_shape=(row, row, mat, mat), name="s5_disc")(
        lam_re.reshape(1, n), lam_im.reshape(1, n), ldt, to_mat(b_re), to_mat(b_im))


def _resident(shape):
    zeros = (0,) * len(shape)
    return pl.BlockSpec(shape, lambda i: zeros, pipeline_mode=pl.Buffered(1))


def _s5_in_body(x_ref, g_ref, w_ref, wo_ref, o_ref, wob_ref, wb_ref):
    @pl.when(pl.program_id(0) == 0)
    def _():
        wb_ref[...] = w_ref[...].astype(BF16)

    wob_ref[...] = wo_ref[...].astype(BF16)
    h = _rms(x_ref[...], g_ref[...]).astype(BF16)
    o_ref[...] = jnp.dot(h, wb_ref[...], preferred_element_type=F32)


def _s5_in(x, norm_g, w, w_out, layer, *, tm=512):
    steps = TOKENS // tm
    wo_block = pl.BlockSpec((D_MODEL // steps, 2 * D_MODEL), lambda i: (i, 0))
    return pl.pallas_call(
        _s5_in_body,
        grid=(steps,),
        in_specs=[
            pl.BlockSpec((tm, D_MODEL), lambda i: (i, 0)),
            pl.BlockSpec((None, None, 1, D_MODEL), lambda i: (layer, 1, 0, 0)),
            _resident((D_MODEL, D_MODEL)),
            wo_block,
        ],
        out_specs=[pl.BlockSpec((tm, D_MODEL), lambda i: (i, 0)), wo_block],
        out_shape=[jax.ShapeDtypeStruct((TOKENS, D_MODEL), F32),
                   jax.ShapeDtypeStruct((D_MODEL, 2 * D_MODEL), BF16)],
        scratch_shapes=[pltpu.VMEM((D_MODEL, D_MODEL), BF16)],
        compiler_params=_params(("arbitrary",)),
        name="s5_in",
    )(x, norm_g, w, w_out)


def _s5_scan_body(u_ref, bmat_ref, cmat_ref, ar_ref, ai_ref, d_ref, o_ref,
                  lhs_ref, bu0_ref, bu1_ref, hs0_ref, hs1_ref, yp_ref, carry_ref):
    tile = SUBLANES
    lane = lax.broadcasted_iota(jnp.int32, (tile, LANES), 1)
    low = lane < HALF_CH
    bu_refs = (bu0_ref, bu1_ref)
    hs_refs = (hs0_ref, hs1_ref)

    @pl.when(pl.program_id(1) == 0)
    def _():
        carry_ref[...] = jnp.zeros_like(carry_ref)

    re, im = slice(0, HALF_STATES), slice(HALF_STATES, 2 * HALF_STATES)
    pair_slabs = lambda pair: range(pair * SCAN_PAIR, (pair + 1) * SCAN_PAIR)
    phase_t = SCAN_T // SCAN_PHASES

    def phase_rows(phase):
        return slice(phase * phase_t * tile, (phase + 1) * phase_t * tile)

    def build_lhs(pair, phase):
        for s in pair_slabs(pair):
            for b in range(BATCH):
                for t0 in range(phase * phase_t, (phase + 1) * phase_t, tile):
                    v = u_ref[b, t0:t0 + tile, s * LANES:(s + 1) * LANES]
                    lhs_ref[s, pl.ds(t0 * tile + b, tile, stride=tile), :] = jnp.where(low, v, 0.0)
                    lhs_ref[s, pl.ds(t0 * tile + BATCH + b, tile, stride=tile), :] = jnp.where(low, 0.0, v)

    def to_states(pair, phase):
        rows = phase_rows(phase)
        for k, s in enumerate(pair_slabs(pair)):
            bu_refs[pair][k, rows, :] = jnp.dot(lhs_ref[s, rows, :].astype(BF16), bmat_ref[s],
                                                preferred_element_type=F32)

    def scan(pair, phase, state):
        bu_ref, hs_ref = bu_refs[pair], hs_refs[pair]
        for t in range(phase * phase_t, (phase + 1) * phase_t):
            rows = slice(t * tile, (t + 1) * tile)
            for k, s in enumerate(pair_slabs(pair)):
                a_r, a_i, hr, hi = state[k]
                hr, hi = ((a_r * hr - a_i * hi) + bu_ref[k, rows, re],
                          (a_r * hi + a_i * hr) + bu_ref[k, rows, im])
                hs_ref[k, rows, re] = hr
                hs_ref[k, rows, im] = hi
                state[k] = (a_r, a_i, hr, hi)

    def from_states(pair, phase):
        rows = phase_rows(phase)
        for k, s in enumerate(pair_slabs(pair)):
            yp_ref[s, rows, :] = jnp.dot(hs_refs[pair][k, rows, :].astype(BF16), cmat_ref[s],
                                         preferred_element_type=F32)

    def write_out(pair, phase):
        for s in pair_slabs(pair):
            sl = slice(s * LANES, (s + 1) * LANES)
            dsk = d_ref[:, sl]
            for b in range(BATCH):
                for t0 in range(phase * phase_t, (phase + 1) * phase_t, 2 * tile):
                    ys = []
                    for t1 in (t0, t0 + tile):
                        y0 = yp_ref[s, pl.ds(t1 * tile + b, tile, stride=tile), :]
                        y1 = yp_ref[s, pl.ds(t1 * tile + BATCH + b, tile, stride=tile), :]
                        ys.append(jnp.where(low, y0, y1) + dsk * u_ref[b, t1:t1 + tile, sl])
                    o_ref[b, t0:t0 + 2 * tile, sl] = _gelu(jnp.concatenate(ys, axis=0)).astype(BF16)

    pairs = range(SCAN_SLABS // SCAN_PAIR)
    states = [[(ar_ref[s], ai_ref[s], carry_ref[s, :, re], carry_ref[s, :, im])
               for s in pair_slabs(pair)] for pair in pairs]
    for phase in range(SCAN_PHASES):
        for pair in pairs:
            build_lhs(pair, phase)
            to_states(pair, phase)
    for phase in range(SCAN_PHASES):
        for pair in pairs:
            scan(pair, phase, states[pair])
            from_states(pair, phase)
            write_out(pair, phase)
    for pair in pairs:
        for k, s in enumerate(pair_slabs(pair)):
            carry_ref[s, :, re] = states[pair][k][2]
            carry_ref[s, :, im] = states[pair][k][3]


def _s5_scan(u, bmat, cmat, a_r, a_i, d):
    width = SCAN_SLABS * LANES
    rows = SCAN_T * SUBLANES
    pair_buf = pltpu.VMEM((SCAN_PAIR, rows, 2 * HALF_STATES), F32)
    return pl.pallas_call(
        _s5_scan_body,
        grid=(N_SLABS // SCAN_SLABS, SEQ // SCAN_T),
        in_specs=[
            pl.BlockSpec((BATCH, SCAN_T, width), lambda s, c: (0, c, s)),
            pl.BlockSpec((SCAN_SLABS, LANES, 2 * HALF_STATES), lambda s, c: (s, 0, 0)),
            pl.BlockSpec((SCAN_SLABS, 2 * HALF_STATES, LANES), lambda s, c: (s, 0, 0)),
            pl.BlockSpec((SCAN_SLABS, SUBLANES, HALF_STATES), lambda s, c: (s, 0, 0)),
            pl.BlockSpec((SCAN_SLABS, SUBLANES, HALF_STATES), lambda s, c: (s, 0, 0)),
            pl.BlockSpec((1, width), lambda s, c: (0, s)),
        ],
        out_specs=pl.BlockSpec((BATCH, SCAN_T, width), lambda s, c: (0, c, s)),
        out_shape=jax.ShapeDtypeStruct((BATCH, SEQ, D_MODEL), BF16),
        scratch_shapes=[
            pltpu.VMEM((SCAN_SLABS, rows, LANES), F32),
            pair_buf, pair_buf, pair_buf, pair_buf,
            pltpu.VMEM((SCAN_SLABS, rows, LANES), F32),
            pltpu.VMEM((SCAN_SLABS, SUBLANES, 2 * HALF_STATES), F32),
        ],
        compiler_params=_params(("parallel", "arbitrary")),
        name="s5_scan",
    )(u, bmat, cmat, a_r, a_i, d)


def _s5_scan_operands(a_r, a_i, bb_r, bb_i, c_re, c_im):
    halves = SLAB_GROUPS // 2
    eye = jnp.eye(halves, dtype=F32)

    def b_blocks(bb):
        bb = bb.reshape(SSM_GROUP_DIM, N_SLABS, 2, halves, SSM_STATE)
        blk = jnp.einsum('hsqgp,gk->sqghkp', bb, eye)
        return blk.reshape(N_SLABS, LANES, HALF_STATES)

    def c_blocks(c):
        c = c.reshape(N_SLABS, 2, halves, SSM_GROUP_DIM, SSM_STATE)
        blk = jnp.einsum('sqghp,gk->skpqgh', c, eye)
        return blk.reshape(N_SLABS, HALF_STATES, LANES)

    bmat = jnp.concatenate([b_blocks(bb_r), b_blocks(bb_i)], axis=-1).astype(BF16)
    cmat = jnp.concatenate([c_blocks(c_re), -c_blocks(c_im)], axis=1).astype(BF16)

    def a_rows(a):
        a = a.reshape(N_SLABS, 2, 1, HALF_STATES)
        return jnp.broadcast_to(a, (N_SLABS, 2, BATCH, HALF_STATES)).reshape(
            N_SLABS, SUBLANES, HALF_STATES)

    return bmat, cmat, a_rows(a_r), a_rows(a_i)


def _s5_out_body(x_ref, a_ref, w_ref, o_ref, *, tn):
    a = a_ref[...]
    for c in range(0, D_MODEL, tn):
        val = jnp.dot(a, w_ref[:, c:c + tn], preferred_element_type=F32)
        gate = jnp.dot(a, w_ref[:, D_MODEL + c:D_MODEL + c + tn], preferred_element_type=F32)
        o_ref[:, c:c + tn] = x_ref[:, c:c + tn] + val * _sigmoid(gate)


def _s5_out(x, act, w, *, tm=512, tn=512):
    return pl.pallas_call(
        functools.partial(_s5_out_body, tn=tn),
        grid=(TOKENS // tm,),
        in_specs=[
            pl.BlockSpec((tm, D_MODEL), lambda i: (i, 0)),
            pl.BlockSpec((tm, D_MODEL), lambda i: (i, 0)),
            _resident((D_MODEL, 2 * D_MODEL)),
        ],
        out_specs=pl.BlockSpec((tm, D_MODEL), lambda i: (i, 0)),
        out_shape=jax.ShapeDtypeStruct((TOKENS, D_MODEL), F32),
        compiler_params=_params(("parallel",)),
        name="s5_out",
    )(x, act, w)


def _s5(x, norm_g, layer, w_in, lam_re, lam_im, log_dt, b_re, b_im, c_re, c_im, d_skip, w_out):
    a_r, a_i, bb_r, bb_i = _s5_disc(lam_re, lam_im, log_dt, b_re, b_im)
    bmat, cmat, ar_rows, ai_rows = _s5_scan_operands(a_r, a_i, bb_r, bb_i, c_re, c_im)
    u, w_out_bf = _s5_in(x, norm_g, w_in, w_out, layer)
    act = _s5_scan(u.reshape(BATCH, SEQ, D_MODEL), bmat, cmat, ar_rows, ai_rows,
                   d_skip.reshape(1, D_MODEL))
    return _s5_out(x, act.reshape(TOKENS, D_MODEL), w_out_bf)


def _ple_body(x_ref, g_ref, wg_ref, p_ref, wp_ref, fg_ref, o_ref, wgb_ref, wpb_ref, *, final):
    @pl.when(pl.program_id(0) == 0)
    def _():
        wgb_ref[...] = wg_ref[...].astype(BF16)
        wpb_ref[...] = wp_ref[...].astype(BF16)

    x = x_ref[...]
    h = _rms(x, g_ref[...]).astype(BF16)
    gate = _sigmoid(jnp.dot(h, wgb_ref[...], preferred_element_type=F32))
    proj = jnp.dot(p_ref[...].astype(BF16), wpb_ref[...], preferred_element_type=F32)
    y = x + gate * proj
    o_ref[...] = _rms(y, fg_ref[...]) if final else y


def _ple(x, norm_g, wg, p, wp, fg, layer, *, final, tm=512):
    once = pl.Buffered(1)
    return pl.pallas_call(
        functools.partial(_ple_body, final=final),
        grid=(TOKENS // tm,),
        in_specs=[
            pl.BlockSpec((tm, D_MODEL), lambda i: (i, 0)),
            pl.BlockSpec((None, None, 1, D_MODEL), lambda i: (layer, 3, 0, 0)),
            pl.BlockSpec((None, D_MODEL, D_MODEL), lambda i: (layer, 0, 0), pipeline_mode=once),
            pl.BlockSpec((None, tm, PLE_DIM), lambda i: (layer, i, 0)),
            pl.BlockSpec((None, PLE_DIM, D_MODEL), lambda i: (layer, 0, 0), pipeline_mode=once),
            pl.BlockSpec((1, D_MODEL), lambda i: (0, 0)),
        ],
        out_specs=pl.BlockSpec((tm, D_MODEL), lambda i: (i, 0)),
        out_shape=jax.ShapeDtypeStruct((TOKENS, D_MODEL), F32),
        scratch_shapes=[pltpu.VMEM((D_MODEL, D_MODEL), BF16), pltpu.VMEM((PLE_DIM, D_MODEL), BF16)],
        compiler_params=_params(("arbitrary",)),
        name="ple",
    )(x, norm_g, wg, p, wp, fg)


def kernel(x, p, norm_g, final_norm_g, ffn_w_gate, ffn_w_up, ffn_w_down, gmlp_w_in, gmlp_ln_g, gmlp_ln_b, gmlp_w_s, gmlp_b_s, gmlp_w_out, s5_w_in, s5_lam_re, s5_lam_im, s5_log_dt, s5_b_re, s5_b_im, s5_c_re, s5_c_im, s5_d, s5_w_out, ple_w_gate, ple_w_proj):
    x = x.reshape(TOKENS, D_MODEL)
    p = p.reshape(DEPTH, TOKENS, PLE_DIM)
    norm_g = norm_g.reshape(DEPTH, 4, 1, D_MODEL)
    fg = final_norm_g.reshape(1, D_MODEL)
    for i in range(DEPTH):
        j = i // 2
        if i % 2 == 0:
            side = ((gmlp_w_in[j], GMLP_CAST_BLOCKS), (gmlp_w_out[j], GMLP_CAST_BLOCKS))
            x, (w_in, w_out) = _ffn(x, norm_g, ffn_w_gate, ffn_w_up, ffn_w_down, i, 0,
                                    side=side, tf=256)
            x = _gmlp(x, norm_g, w_in, gmlp_ln_g[j].reshape(1, GMLP_HALF),
                      gmlp_ln_b[j].reshape(1, GMLP_HALF), gmlp_w_s[j],
                      gmlp_b_s[j].reshape(GMLP_GROUPS, CHUNK, 1), w_out, i)
        else:
            x, _ = _ffn(x, norm_g, ffn_w_gate, ffn_w_up, ffn_w_down, i, 0)
            x = _s5(x, norm_g, i, s5_w_in[j], s5_lam_re[j], s5_lam_im[j], s5_log_dt[j],
                    s5_b_re[j], s5_b_im[j], s5_c_re[j], s5_c_im[j], s5_d[j], s5_w_out[j])
        x, _ = _ffn(x, norm_g, ffn_w_gate, ffn_w_up, ffn_w_down, i, 1)
        x = _ple(x, norm_g, ple_w_gate, p, ple_w_proj, fg, i, final=(i == DEPTH - 1))
    return x.reshape(BATCH, SEQ, D_MODEL)
```

```python
import functools
import math

import jax
import jax.numpy as jnp
from jax import lax
from jax.experimental import pallas as pl
from jax.experimental.pallas import tpu as pltpu

F32 = jnp.float32
BF16 = jnp.bfloat16

D_MODEL = 2048
BATCH = 4
SEQ = 2048
TOKENS = BATCH * SEQ
DEPTH = 2
D_FF = 5632
PLE_DIM = 256
CHUNK = 128
GMLP_HALF = 3 * D_MODEL
GMLP_GROUPS = 16
GMLP_GROUP_DIM = GMLP_HALF // GMLP_GROUPS
SSM_GROUP_DIM = 16
SSM_GROUPS = D_MODEL // SSM_GROUP_DIM
SSM_STATE = 64
EPS = 1e-6

LANES = 128
SUBLANES = 8
VMEM_LIMIT = 60 * 1024 * 1024

SLAB_GROUPS = LANES // SSM_GROUP_DIM
N_SLABS = D_MODEL // LANES
SLAB_STATES = SLAB_GROUPS * SSM_STATE
HALF_STATES = SLAB_STATES // 2
HALF_CH = LANES // 2
SCAN_PAIR = 2
SCAN_SLABS = 2 * SCAN_PAIR
SCAN_T = 128
SCAN_PHASES = 2


def _rms(x, g):
    ms = jnp.mean(x * x, axis=-1, keepdims=True)
    return x * lax.rsqrt(ms + EPS) * g


def _gelu(x):
    c = math.sqrt(2.0 / math.pi)
    return x * (0.5 * (1.0 + jnp.tanh(c * (x + 0.044715 * (x * x * x)))))


def _sigmoid(x):
    return 1.0 / (1.0 + jnp.exp(-x))


def _params(sem):
    return pltpu.CompilerParams(dimension_semantics=sem, vmem_limit_bytes=VMEM_LIMIT)


def _gain_spec(layer, slot):
    return pl.BlockSpec((None, None, 1, D_MODEL), lambda i, j: (layer, slot, 0, 0))


def _token_tile_copy(x_hbm, x_ref, sem, tile):
    tm = x_ref.shape[0]
    rows = pl.ds(pl.multiple_of(tile * tm, tm), tm)
    return pltpu.make_async_copy(x_hbm.at[rows, :], x_ref, sem)


def _wait_token_tile(x_hbm, x_ref, sem):
    i = pl.program_id(0)

    @pl.when(i == 0)
    def _():
        _token_tile_copy(x_hbm, x_ref, sem, i).start()

    _token_tile_copy(x_hbm, x_ref, sem, i).wait()


def _prefetch_next_token_tile(x_hbm, x_ref, sem):
    i, j = pl.program_id(0), pl.program_id(1)

    @pl.when(jnp.logical_and(j == 1, i + 1 < pl.num_programs(0)))
    def _():
        _token_tile_copy(x_hbm, x_ref, sem, i + 1).start()


def _ffn_body(*refs, n_side):
    x_hbm, g_ref, wg_ref, wu_ref, wd_ref = refs[:5]
    side_in = refs[5:5 + n_side]
    o_ref = refs[5 + n_side]
    side_out = refs[6 + n_side:6 + 2 * n_side]
    h_ref, x_ref, sem = refs[6 + 2 * n_side:]

    @pl.when(pl.program_id(1) == 0)
    def _():
        _wait_token_tile(x_hbm, x_ref, sem)
        x = x_ref[...]
        h_ref[...] = _rms(x, g_ref[...]).astype(BF16)
        o_ref[...] = x

    _prefetch_next_token_tile(x_hbm, x_ref, sem)

    for src, dst in zip(side_in, side_out):
        dst[...] = src[...].astype(BF16)

    h = h_ref[...]
    gate = jnp.dot(h, wg_ref[...].astype(BF16), preferred_element_type=F32)
    up = jnp.dot(h, wu_ref[...].astype(BF16), preferred_element_type=F32)
    a = (gate * _sigmoid(gate)) * (0.5 * up)
    o_ref[...] += jnp.dot(a.astype(BF16), wd_ref[...].astype(BF16), preferred_element_type=F32)


def _ffn(x, norm_g, wg, wu, wd, layer, which, *, side=(), tm=1024, tf=512):
    ni, nj = TOKENS // tm, D_FF // tf
    side_specs, side_shapes = [], []
    for arr, n_blocks in side:
        assert n_blocks <= ni * nj and arr.shape[0] % n_blocks == 0
        block = (arr.shape[0] // n_blocks, arr.shape[1])
        assert block[0] % (2 * SUBLANES) == 0
        index = functools.partial(
            lambda i, j, last: (jnp.minimum(i * nj + j, last), 0), last=n_blocks - 1)
        side_specs.append(pl.BlockSpec(block, index))
        side_shapes.append(jax.ShapeDtypeStruct(arr.shape, BF16))
    out = pl.pallas_call(
        functools.partial(_ffn_body, n_side=len(side)),
        grid=(ni, nj),
        in_specs=[
            pl.BlockSpec(memory_space=pl.ANY),
            _gain_spec(layer, 2 * which),
            pl.BlockSpec((None, None, D_MODEL, tf), lambda i, j: (layer, which, 0, j)),
            pl.BlockSpec((None, None, D_MODEL, tf), lambda i, j: (layer, which, 0, j)),
            pl.BlockSpec((None, None, tf, D_MODEL), lambda i, j: (layer, which, j, 0)),
        ] + side_specs,
        out_specs=[pl.BlockSpec((tm, D_MODEL), lambda i, j: (i, 0))] + side_specs,
        out_shape=[jax.ShapeDtypeStruct((TOKENS, D_MODEL), F32)] + side_shapes,
        scratch_shapes=[pltpu.VMEM((tm, D_MODEL), BF16), pltpu.VMEM((tm, D_MODEL), F32),
                        pltpu.SemaphoreType.DMA(())],
        compiler_params=_params(("arbitrary", "arbitrary")),
        name="ffn",
    )(x, norm_g, wg, wu, wd, *[arr for arr, _ in side])
    return out[0], out[1:]


GMLP_STEP_GROUPS = 2
GMLP_CAST_BLOCKS = 128
GMLP_SUBTILE = 512
GMLP_STEP_W = GMLP_STEP_GROUPS * GMLP_GROUP_DIM


def _gmlp_body(x_hbm, g_ref, wu_ref, wv_ref, lng_ref, lnb_ref, ws_ref, bs_ref, wo_ref,
               o_ref, h_ref, gated_ref, x_ref, sem):
    @pl.when(pl.program_id(1) == 0)
    def _():
        _wait_token_tile(x_hbm, x_ref, sem)
        x = x_ref[...]
        h_ref[...] = _rms(x, g_ref[...]).astype(BF16)
        o_ref[...] = x

    _prefetch_next_token_tile(x_hbm, x_ref, sem)

    tm = x_ref.shape[0]
    subs = [slice(r, r + GMLP_SUBTILE) for r in range(0, tm, GMLP_SUBTILE)]
    z = []
    for rows in subs:
        h = h_ref[rows, :]
        zv = _gelu(jnp.dot(h, wv_ref[...], preferred_element_type=F32))
        zu = _gelu(jnp.dot(h, wu_ref[...], preferred_element_type=F32))
        z.append((zu, zv))
    row = lax.broadcasted_iota(jnp.int32, (CHUNK, CHUNK), 0)
    col = lax.broadcasted_iota(jnp.int32, (CHUNK, CHUNK), 1)
    ws = [jnp.where(row >= col, ws_ref[gi], 0.0).astype(BF16)
          for gi in range(GMLP_STEP_GROUPS)]
    for rows, (zu, zv) in zip(subs, z):
        for gi in range(GMLP_STEP_GROUPS):
            lo, hi = gi * GMLP_GROUP_DIM, (gi + 1) * GMLP_GROUP_DIM
            v = zv[:, lo:hi]
            d = v - jnp.mean(v, axis=-1, keepdims=True)
            var = jnp.mean(d * d, axis=-1, keepdims=True)
            vn = ((d * lax.rsqrt(var + EPS)) * lng_ref[:, lo:hi] + lnb_ref[:, lo:hi]).astype(BF16)
            bias = bs_ref[gi]
            for c in range(0, GMLP_SUBTILE, CHUNK):
                sv = jnp.dot(ws[gi], vn[c:c + CHUNK], preferred_element_type=F32) + bias
                out_rows = slice(rows.start + c, rows.start + c + CHUNK)
                gated_ref[out_rows, lo:hi] = (zu[c:c + CHUNK, lo:hi] * sv).astype(BF16)
    for rows in subs:
        o_ref[rows, :] += jnp.dot(gated_ref[rows, :], wo_ref[...], preferred_element_type=F32)


def _gmlp(x, norm_g, w_in, ln_g, ln_b, w_s, b_s, w_out, layer, *, tm=1024):
    nj = GMLP_GROUPS // GMLP_STEP_GROUPS
    return pl.pallas_call(
        _gmlp_body,
        grid=(TOKENS // tm, nj),
        in_specs=[
            pl.BlockSpec(memory_space=pl.ANY),
            _gain_spec(layer, 1),
            pl.BlockSpec((D_MODEL, GMLP_STEP_W), lambda i, j: (0, j)),
            pl.BlockSpec((D_MODEL, GMLP_STEP_W), lambda i, j: (0, j + nj)),
            pl.BlockSpec((1, GMLP_STEP_W), lambda i, j: (0, j)),
            pl.BlockSpec((1, GMLP_STEP_W), lambda i, j: (0, j)),
            pl.BlockSpec((GMLP_STEP_GROUPS, CHUNK, CHUNK), lambda i, j: (j, 0, 0)),
            pl.BlockSpec((GMLP_STEP_GROUPS, CHUNK, 1), lambda i, j: (j, 0, 0)),
            pl.BlockSpec((GMLP_STEP_W, D_MODEL), lambda i, j: (j, 0)),
        ],
        out_specs=pl.BlockSpec((tm, D_MODEL), lambda i, j: (i, 0)),
        out_shape=jax.ShapeDtypeStruct((TOKENS, D_MODEL), F32),
        scratch_shapes=[pltpu.VMEM((tm, D_MODEL), BF16), pltpu.VMEM((tm, GMLP_STEP_W), BF16),
                        pltpu.VMEM((tm, D_MODEL), F32), pltpu.SemaphoreType.DMA(())],
        compiler_params=_params(("arbitrary", "arbitrary")),
        name="gmlp",
    )(x, norm_g, w_in, w_in, ln_g, ln_b, w_s, b_s, w_out)


def _s5_disc_body(lr_ref, li_ref, ldt_ref, br_ref, bi_ref, ar_ref, ai_ref, bbr_ref, bbi_ref):
    lr, li = lr_ref[...], li_ref[...]
    dt = jnp.exp(ldt_ref[...])
    mag = jnp.exp(lr * dt)
    ang = li * dt
    a_r = mag * jnp.cos(ang)
    a_i = mag * jnp.sin(ang)
    den = lr * lr + li * li
    nr = a_r - 1.0
    z_r = (nr * lr + a_i * li) / den
    z_i = (a_i * lr - nr * li) / den
    ar_ref[...] = a_r
    ai_ref[...] = a_i
    bbr_ref[...] = z_r * br_ref[...] - z_i * bi_ref[...]
    bbi_ref[...] = z_r * bi_ref[...] + z_i * br_ref[...]


def _s5_disc(lam_re, lam_im, log_dt, b_re, b_im):
    n = SSM_GROUPS * SSM_STATE
    row = jax.ShapeDtypeStruct((1, n), F32)
    mat = jax.ShapeDtypeStruct((SSM_GROUP_DIM, n), F32)
    to_mat = lambda b: jnp.transpose(b, (2, 0, 1)).reshape(SSM_GROUP_DIM, n)
    ldt = jnp.broadcast_to(log_dt[:, None], (SSM_GROUPS, SSM_STATE)).reshape(1, n)
    return pl.pallas_call(_s5_disc_body, out_shape=(row, row, mat, mat), name="s5_disc")(
        lam_re.reshape(1, n), lam_im.reshape(1, n), ldt, to_mat(b_re), to_mat(b_im))


def _resident(shape):
    zeros = (0,) * len(shape)
    return pl.BlockSpec(shape, lambda i: zeros, pipeline_mode=pl.Buffered(1))


def _s5_in_body(x_ref, g_ref, w_ref, wo_ref, o_ref, wob_ref, wb_ref):
    @pl.when(pl.program_id(0) == 0)
    def _():
        wb_ref[...] = w_ref[...].astype(BF16)

    wob_ref[...] = wo_ref[...].astype(BF16)
    h = _rms(x_ref[...], g_ref[...]).astype(BF16)
    o_ref[...] = jnp.dot(h, wb_ref[...], preferred_element_type=F32)


def _s5_in(x, norm_g, w, w_out, layer, *, tm=512):
    steps = TOKENS // tm
    wo_block = pl.BlockSpec((D_MODEL // steps, 2 * D_MODEL), lambda i: (i, 0))
    return pl.pallas_call(
        _s5_in_body,
        grid=(steps,),
        in_specs=[
            pl.BlockSpec((tm, D_MODEL), lambda i: (i, 0)),
            pl.BlockSpec((None, None, 1, D_MODEL), lambda i: (layer, 1, 0, 0)),
            _resident((D_MODEL, D_MODEL)),
            wo_block,
        ],
        out_specs=[pl.BlockSpec((tm, D_MODEL), lambda i: (i, 0)), wo_block],
        out_shape=[jax.ShapeDtypeStruct((TOKENS, D_MODEL), F32),
                   jax.ShapeDtypeStruct((D_MODEL, 2 * D_MODEL), BF16)],
        scratch_shapes=[pltpu.VMEM((D_MODEL, D_MODEL), BF16)],
        compiler_params=_params(("arbitrary",)),
        name="s5_in",
    )(x, norm_g, w, w_out)


def _s5_scan_body(u_ref, bmat_ref, cmat_ref, ar_ref, ai_ref, d_ref, o_ref,
                  lhs_ref, bu0_ref, bu1_ref, hs0_ref, hs1_ref, yp_ref, carry_ref):
    tile = SUBLANES
    lane = lax.broadcasted_iota(jnp.int32, (tile, LANES), 1)
    low = lane < HALF_CH
    bu_refs = (bu0_ref, bu1_ref)
    hs_refs = (hs0_ref, hs1_ref)

    @pl.when(pl.program_id(1) == 0)
    def _():
        carry_ref[...] = jnp.zeros_like(carry_ref)

    re, im = slice(0, HALF_STATES), slice(HALF_STATES, 2 * HALF_STATES)
    pair_slabs = lambda pair: range(pair * SCAN_PAIR, (pair + 1) * SCAN_PAIR)
    phase_t = SCAN_T // SCAN_PHASES

    def phase_rows(phase):
        return slice(phase * phase_t * tile, (phase + 1) * phase_t * tile)

    def build_lhs(pair, phase):
        for s in pair_slabs(pair):
            for b in range(BATCH):
                for t0 in range(phase * phase_t, (phase + 1) * phase_t, tile):
                    v = u_ref[b, t0:t0 + tile, s * LANES:(s + 1) * LANES]
                    lhs_ref[s, pl.ds(t0 * tile + b, tile, stride=tile), :] = jnp.where(low, v, 0.0)
                    lhs_ref[s, pl.ds(t0 * tile + BATCH + b, tile, stride=tile), :] = jnp.where(low, 0.0, v)

    def to_states(pair, phase):
        rows = phase_rows(phase)
        for k, s in enumerate(pair_slabs(pair)):
            bu_refs[pair][k, rows, :] = jnp.dot(lhs_ref[s, rows, :].astype(BF16), bmat_ref[s],
                                                preferred_element_type=F32)

    def scan(pair, phase, state):
        bu_ref, hs_ref = bu_refs[pair], hs_refs[pair]
        for t in range(phase * phase_t, (phase + 1) * phase_t):
            rows = slice(t * tile, (t + 1) * tile)
            for k, s in enumerate(pair_slabs(pair)):
                a_r, a_i, hr, hi = state[k]
                hr, hi = ((a_r * hr - a_i * hi) + bu_ref[k, rows, re],
                          (a_r * hi + a_i * hr) + bu_ref[k, rows, im])
                hs_ref[k, rows, re] = hr
                hs_ref[k, rows, im] = hi
                state[k] = (a_r, a_i, hr, hi)

    def from_states(pair, phase):
        rows = phase_rows(phase)
        for k, s in enumerate(pair_slabs(pair)):
            yp_ref[s, rows, :] = jnp.dot(hs_refs[pair][k, rows, :].astype(BF16), cmat_ref[s],
                                         preferred_element_type=F32)

    def write_out(pair, phase):
        for s in pair_slabs(pair):
            sl = slice(s * LANES, (s + 1) * LANES)
            dsk = d_ref[:, sl]
            for b in range(BATCH):
                for t0 in range(phase * phase_t, (phase + 1) * phase_t, 2 * tile):
                    ys = []
                    for t1 in (t0, t0 + tile):
                        y0 = yp_ref[s, pl.ds(t1 * tile + b, tile, stride=tile), :]
                        y1 = yp_ref[s, pl.ds(t1 * tile + BATCH + b, tile, stride=tile), :]
                        ys.append(jnp.where(low, y0, y1) + dsk * u_ref[b, t1:t1 + tile, sl])
                    o_ref[b, t0:t0 + 2 * tile, sl] = _gelu(jnp.concatenate(ys, axis=0)).astype(BF16)

    pairs = range(SCAN_SLABS // SCAN_PAIR)
    states = [[(ar_ref[s], ai_ref[s], carry_ref[s, :, re], carry_ref[s, :, im])
               for s in pair_slabs(pair)] for pair in pairs]
    for phase in range(SCAN_PHASES):
        for pair in pairs:
            build_lhs(pair, phase)
            to_states(pair, phase)
    for phase in range(SCAN_PHASES):
        for pair in pairs:
            scan(pair, phase, states[pair])
            from_states(pair, phase)
            write_out(pair, phase)
    for pair in pairs:
        for k, s in enumerate(pair_slabs(pair)):
            carry_ref[s, :, re] = states[pair][k][2]
            carry_ref[s, :, im] = states[pair][k][3]


def _s5_scan(u, bmat, cmat, a_r, a_i, d):
    width = SCAN_SLABS * LANES
    rows = SCAN_T * SUBLANES
    pair_buf = pltpu.VMEM((SCAN_PAIR, rows, 2 * HALF_STATES), F32)
    return pl.pallas_call(
        _s5_scan_body,
        grid=(N_SLABS // SCAN_SLABS, SEQ // SCAN_T),
        in_specs=[
            pl.BlockSpec((BATCH, SCAN_T, width), lambda s, c: (0, c, s)),
            pl.BlockSpec((SCAN_SLABS, LANES, 2 * HALF_STATES), lambda s, c: (s, 0, 0)),
            pl.BlockSpec((SCAN_SLABS, 2 * HALF_STATES, LANES), lambda s, c: (s, 0, 0)),
            pl.BlockSpec((SCAN_SLABS, SUBLANES, HALF_STATES), lambda s, c: (s, 0, 0)),
            pl.BlockSpec((SCAN_SLABS, SUBLANES, HALF_STATES), lambda s, c: (s, 0, 0)),
            pl.BlockSpec((1, width), lambda s, c: (0, s)),
        ],
        out_specs=pl.BlockSpec((BATCH, SCAN_T, width), lambda s, c: (0, c, s)),
        out_shape=jax.ShapeDtypeStruct((BATCH, SEQ, D_MODEL), BF16),
        scratch_shapes=[
            pltpu.VMEM((SCAN_SLABS, rows, LANES), F32),
            pair_buf, pair_buf, pair_buf, pair_buf,
            pltpu.VMEM((SCAN_SLABS, rows, LANES), F32),
            pltpu.VMEM((SCAN_SLABS, SUBLANES, 2 * HALF_STATES), F32),
        ],
        compiler_params=_params(("parallel", "arbitrary")),
        name="s5_scan",
    )(u, bmat, cmat, a_r, a_i, d)


def _s5_scan_operands(a_r, a_i, bb_r, bb_i, c_re, c_im):
    halves = SLAB_GROUPS // 2
    eye = jnp.eye(halves, dtype=F32)

    def b_blocks(bb):
        bb = bb.reshape(SSM_GROUP_DIM, N_SLABS, 2, halves, SSM_STATE)
        blk = jnp.einsum('hsqgp,gk->sqghkp', bb, eye)
        return blk.reshape(N_SLABS, LANES, HALF_STATES)

    def c_blocks(c):
        c = c.reshape(N_SLABS, 2, halves, SSM_GROUP_DIM, SSM_STATE)
        blk = jnp.einsum('sqghp,gk->skpqgh', c, eye)
        return blk.reshape(N_SLABS, HALF_STATES, LANES)

    bmat = jnp.concatenate([b_blocks(bb_r), b_blocks(bb_i)], axis=-1).astype(BF16)
    cmat = jnp.concatenate([c_blocks(c_re), -c_blocks(c_im)], axis=1).astype(BF16)

    def a_rows(a):
        a = a.reshape(N_SLABS, 2, 1, HALF_STATES)
        return jnp.broadcast_to(a, (N_SLABS, 2, BATCH, HALF_STATES)).reshape(
            N_SLABS, SUBLANES, HALF_STATES)

    return bmat, cmat, a_rows(a_r), a_rows(a_i)


def _s5_out_body(x_ref, a_ref, w_ref, o_ref, *, tn):
    a = a_ref[...]
    for c in range(0, D_MODEL, tn):
        val = jnp.dot(a, w_ref[:, c:c + tn], preferred_element_type=F32)
        gate = jnp.dot(a, w_ref[:, D_MODEL + c:D_MODEL + c + tn], preferred_element_type=F32)
        o_ref[:, c:c + tn] = x_ref[:, c:c + tn] + val * _sigmoid(gate)


def _s5_out(x, act, w, *, tm=512, tn=512):
    return pl.pallas_call(
        functools.partial(_s5_out_body, tn=tn),
        grid=(TOKENS // tm,),
        in_specs=[
            pl.BlockSpec((tm, D_MODEL), lambda i: (i, 0)),
            pl.BlockSpec((tm, D_MODEL), lambda i: (i, 0)),
            _resident((D_MODEL, 2 * D_MODEL)),
        ],
        out_specs=pl.BlockSpec((tm, D_MODEL), lambda i: (i, 0)),
        out_shape=jax.ShapeDtypeStruct((TOKENS, D_MODEL), F32),
        compiler_params=_params(("parallel",)),
        name="s5_out",
    )(x, act, w)


def _s5(x, norm_g, layer, w_in, lam_re, lam_im, log_dt, b_re, b_im, c_re, c_im, d_skip, w_out):
    a_r, a_i, bb_r, bb_i = _s5_disc(lam_re, lam_im, log_dt, b_re, b_im)
    bmat, cmat, ar_rows, ai_rows = _s5_scan_operands(a_r, a_i, bb_r, bb_i, c_re, c_im)
    u, w_out_bf = _s5_in(x, norm_g, w_in, w_out, layer)
    act = _s5_scan(u.reshape(BATCH, SEQ, D_MODEL), bmat, cmat, ar_rows, ai_rows,
                   d_skip.reshape(1, D_MODEL))
    return _s5_out(x, act.reshape(TOKENS, D_MODEL), w_out_bf)


def _ple_body(x_ref, g_ref, wg_ref, p_ref, wp_ref, fg_ref, o_ref, wgb_ref, wpb_ref, *, final):
    @pl.when(pl.program_id(0) == 0)
    def _():
        wgb_ref[...] = wg_ref[...].astype(BF16)
        wpb_ref[...] = wp_ref[...].astype(BF16)

    x = x_ref[...]
    h = _rms(x, g_ref[...]).astype(BF16)
    gate = _sigmoid(jnp.dot(h, wgb_ref[...], preferred_element_type=F32))
    proj = jnp.dot(p_ref[...].astype(BF16), wpb_ref[...], preferred_element_type=F32)
    y = x + gate * proj
    o_ref[...] = _rms(y, fg_ref[...]) if final else y


def _ple(x, norm_g, wg, p, wp, fg, layer, *, final, tm=512):
    once = pl.Buffered(1)
    return pl.pallas_call(
        functools.partial(_ple_body, final=final),
        grid=(TOKENS // tm,),
        in_specs=[
            pl.BlockSpec((tm, D_MODEL), lambda i: (i, 0)),
            pl.BlockSpec((None, None, 1, D_MODEL), lambda i: (layer, 3, 0, 0)),
            pl.BlockSpec((None, D_MODEL, D_MODEL), lambda i: (layer, 0, 0), pipeline_mode=once),
            pl.BlockSpec((None, tm, PLE_DIM), lambda i: (layer, i, 0)),
            pl.BlockSpec((None, PLE_DIM, D_MODEL), lambda i: (layer, 0, 0), pipeline_mode=once),
            pl.BlockSpec((1, D_MODEL), lambda i: (0, 0)),
        ],
        out_specs=pl.BlockSpec((tm, D_MODEL), lambda i: (i, 0)),
        out_shape=jax.ShapeDtypeStruct((TOKENS, D_MODEL), F32),
        scratch_shapes=[pltpu.VMEM((D_MODEL, D_MODEL), BF16), pltpu.VMEM((PLE_DIM, D_MODEL), BF16)],
        compiler_params=_params(("arbitrary",)),
        name="ple",
    )(x, norm_g, wg, p, wp, fg)


def kernel(x, p, norm_g, final_norm_g, ffn_w_gate, ffn_w_up, ffn_w_down, gmlp_w_in, gmlp_ln_g, gmlp_ln_b, gmlp_w_s, gmlp_b_s, gmlp_w_out, s5_w_in, s5_lam_re, s5_lam_im, s5_log_dt, s5_b_re, s5_b_im, s5_c_re, s5_c_im, s5_d, s5_w_out, ple_w_gate, ple_w_proj):
    x = x.reshape(TOKENS, D_MODEL)
    p = p.reshape(DEPTH, TOKENS, PLE_DIM)
    norm_g = norm_g.reshape(DEPTH, 4, 1, D_MODEL)
    fg = final_norm_g.reshape(1, D_MODEL)
    for i in range(DEPTH):
        j = i // 2
        if i % 2 == 0:
            side = ((gmlp_w_in[j], GMLP_CAST_BLOCKS), (gmlp_w_out[j], GMLP_CAST_BLOCKS))
            x, (w_in, w_out) = _ffn(x, norm_g, ffn_w_gate, ffn_w_up, ffn_w_down, i, 0,
                                    side=side, tf=256)
            x = _gmlp(x, norm_g, w_in, gmlp_ln_g[j].reshape(1, GMLP_HALF),
                      gmlp_ln_b[j].reshape(1, GMLP_HALF), gmlp_w_s[j],
                      gmlp_b_s[j].reshape(GMLP_GROUPS, CHUNK, 1), w_out, i)
        else:
            x, _ = _ffn(x, norm_g, ffn_w_gate, ffn_w_up, ffn_w_down, i, 0)
            x = _s5(x, norm_g, i, s5_w_in[j], s5_lam_re[j], s5_lam_im[j], s5_log_dt[j],
                    s5_b_re[j], s5_b_im[j], s5_c_re[j], s5_c_im[j], s5_d[j], s5_w_out[j])
        x, _ = _ffn(x, norm_g, ffn_w_gate, ffn_w_up, ffn_w_down, i, 1)
        x = _ple(x, norm_g, ple_w_gate, p, ple_w_proj, fg, i, final=(i == DEPTH - 1))
    return x.reshape(BATCH, SEQ, D_MODEL)
```

```python
import functools
import math

import jax
import jax.numpy as jnp
from jax import lax
from jax.experimental import pallas as pl
from jax.experimental.pallas import tpu as pltpu

F32 = jnp.float32
BF16 = jnp.bfloat16

D_MODEL = 2048
BATCH = 4
SEQ = 2048
TOKENS = BATCH * SEQ
DEPTH = 2
D_FF = 5632
PLE_DIM = 256
CHUNK = 128
GMLP_HALF = 3 * D_MODEL
GMLP_GROUPS = 16
GMLP_GROUP_DIM = GMLP_HALF // GMLP_GROUPS
SSM_GROUP_DIM = 16
SSM_GROUPS = D_MODEL // SSM_GROUP_DIM
SSM_STATE = 64
EPS = 1e-6

LANES = 128
SUBLANES = 8
VMEM_LIMIT = 60 * 1024 * 1024

SLAB_GROUPS = LANES // SSM_GROUP_DIM
N_SLABS = D_MODEL // LANES
SLAB_STATES = SLAB_GROUPS * SSM_STATE
HALF_STATES = SLAB_STATES // 2
HALF_CH = LANES // 2
SCAN_PAIR = 2
SCAN_SLABS = 2 * SCAN_PAIR
SCAN_T = 128
SCAN_PHASES = 1


def _rms(x, g):
    ms = jnp.mean(x * x, axis=-1, keepdims=True)
    return x * lax.rsqrt(ms + EPS) * g


def _gelu(x):
    c = math.sqrt(2.0 / math.pi)
    return x * (0.5 * (1.0 + jnp.tanh(c * (x + 0.044715 * (x * x * x)))))


def _sigmoid(x):
    return 1.0 / (1.0 + jnp.exp(-x))


def _params(sem):
    return pltpu.CompilerParams(dimension_semantics=sem, vmem_limit_bytes=VMEM_LIMIT)


def _gain_spec(layer, slot):
    return pl.BlockSpec((None, None, 1, D_MODEL), lambda i, j: (layer, slot, 0, 0))


TOKEN_PREFETCH_STEP = 3


def _token_tile_copy(x_hbm, x_ref, sem, tile):
    tm = x_ref.shape[0]
    rows = pl.ds(pl.multiple_of(tile * tm, tm), tm)
    return pltpu.make_async_copy(x_hbm.at[rows, :], x_ref, sem)


def _wait_token_tile(x_hbm, x_ref, sem):
    i = pl.program_id(0)

    @pl.when(i == 0)
    def _():
        _token_tile_copy(x_hbm, x_ref, sem, i).start()

    _token_tile_copy(x_hbm, x_ref, sem, i).wait()


def _prefetch_next_token_tile(x_hbm, x_ref, sem):
    i, j = pl.program_id(0), pl.program_id(1)

    @pl.when(jnp.logical_and(j == TOKEN_PREFETCH_STEP, i + 1 < pl.num_programs(0)))
    def _():
        _token_tile_copy(x_hbm, x_ref, sem, i + 1).start()


def _ffn_body(*refs, n_side):
    x_hbm, g_ref, wg_ref, wu_ref, wd_ref = refs[:5]
    side_in = refs[5:5 + n_side]
    o_ref = refs[5 + n_side]
    side_out = refs[6 + n_side:6 + 2 * n_side]
    h_ref, x_ref, sem = refs[6 + 2 * n_side:]

    @pl.when(pl.program_id(1) == 0)
    def _():
        _wait_token_tile(x_hbm, x_ref, sem)
        x = x_ref[...]
        h_ref[...] = _rms(x, g_ref[...]).astype(BF16)
        o_ref[...] = x

    _prefetch_next_token_tile(x_hbm, x_ref, sem)

    for src, dst in zip(side_in, side_out):
        dst[...] = src[...].astype(BF16)

    h = h_ref[...]
    gate = jnp.dot(h, wg_ref[...].astype(BF16), preferred_element_type=F32)
    up = jnp.dot(h, wu_ref[...].astype(BF16), preferred_element_type=F32)
    a = (gate * _sigmoid(gate)) * (0.5 * up)
    o_ref[...] += jnp.dot(a.astype(BF16), wd_ref[...].astype(BF16), preferred_element_type=F32)


def _ffn(x, norm_g, wg, wu, wd, layer, which, *, side=(), tm=1024, tf=512):
    ni, nj = TOKENS // tm, D_FF // tf
    side_specs, side_shapes = [], []
    for arr, n_blocks in side:
        assert n_blocks <= ni * nj and arr.shape[0] % n_blocks == 0
        block = (arr.shape[0] // n_blocks, arr.shape[1])
        assert block[0] % (2 * SUBLANES) == 0
        index = functools.partial(
            lambda i, j, last: (jnp.minimum(i * nj + j, last), 0), last=n_blocks - 1)
        side_specs.append(pl.BlockSpec(block, index))
        side_shapes.append(jax.ShapeDtypeStruct(arr.shape, BF16))
    out = pl.pallas_call(
        functools.partial(_ffn_body, n_side=len(side)),
        grid=(ni, nj),
        in_specs=[
            pl.BlockSpec(memory_space=pl.ANY),
            _gain_spec(layer, 2 * which),
            pl.BlockSpec((None, None, D_MODEL, tf), lambda i, j: (layer, which, 0, j)),
            pl.BlockSpec((None, None, D_MODEL, tf), lambda i, j: (layer, which, 0, j)),
            pl.BlockSpec((None, None, tf, D_MODEL), lambda i, j: (layer, which, j, 0)),
        ] + side_specs,
        out_specs=[pl.BlockSpec((tm, D_MODEL), lambda i, j: (i, 0))] + side_specs,
        out_shape=[jax.ShapeDtypeStruct((TOKENS, D_MODEL), F32)] + side_shapes,
        scratch_shapes=[pltpu.VMEM((tm, D_MODEL), BF16), pltpu.VMEM((tm, D_MODEL), F32),
                        pltpu.SemaphoreType.DMA(())],
        compiler_params=_params(("arbitrary", "arbitrary")),
        name="ffn",
    )(x, norm_g, wg, wu, wd, *[arr for arr, _ in side])
    return out[0], out[1:]


GMLP_STEP_GROUPS = 2
GMLP_CAST_BLOCKS = 128
GMLP_SUBTILE = 512
GMLP_STEP_W = GMLP_STEP_GROUPS * GMLP_GROUP_DIM


def _gmlp_body(x_hbm, g_ref, wu_ref, wv_ref, lng_ref, lnb_ref, ws_ref, bs_ref, wo_ref,
               o_ref, h_ref, gated_ref, x_ref, sem):
    @pl.when(pl.program_id(1) == 0)
    def _():
        _wait_token_tile(x_hbm, x_ref, sem)
        x = x_ref[...]
        h_ref[...] = _rms(x, g_ref[...]).astype(BF16)
        o_ref[...] = x

    _prefetch_next_token_tile(x_hbm, x_ref, sem)

    tm = x_ref.shape[0]
    subs = [slice(r, r + GMLP_SUBTILE) for r in range(0, tm, GMLP_SUBTILE)]
    z = []
    for rows in subs:
        h = h_ref[rows, :]
        zv = _gelu(jnp.dot(h, wv_ref[...], preferred_element_type=F32))
        zu = _gelu(jnp.dot(h, wu_ref[...], preferred_element_type=F32))
        z.append((zu, zv))
    row = lax.broadcasted_iota(jnp.int32, (CHUNK, CHUNK), 0)
    col = lax.broadcasted_iota(jnp.int32, (CHUNK, CHUNK), 1)
    ws = [jnp.where(row >= col, ws_ref[gi], 0.0).astype(BF16)
          for gi in range(GMLP_STEP_GROUPS)]
    for rows, (zu, zv) in zip(subs, z):
        for gi in range(GMLP_STEP_GROUPS):
            lo, hi = gi * GMLP_GROUP_DIM, (gi + 1) * GMLP_GROUP_DIM
            v = zv[:, lo:hi]
            d = v - jnp.mean(v, axis=-1, keepdims=True)
            var = jnp.mean(d * d, axis=-1, keepdims=True)
            vn = ((d * lax.rsqrt(var + EPS)) * lng_ref[:, lo:hi] + lnb_ref[:, lo:hi]).astype(BF16)
            bias = bs_ref[gi]
            for c in range(0, GMLP_SUBTILE, CHUNK):
                sv = jnp.dot(ws[gi], vn[c:c + CHUNK], preferred_element_type=F32) + bias
                out_rows = slice(rows.start + c, rows.start + c + CHUNK)
                gated_ref[out_rows, lo:hi] = (zu[c:c + CHUNK, lo:hi] * sv).astype(BF16)
    for rows in subs:
        o_ref[rows, :] += jnp.dot(gated_ref[rows, :], wo_ref[...], preferred_element_type=F32)


def _gmlp(x, norm_g, w_in, ln_g, ln_b, w_s, b_s, w_out, layer, *, tm=1024):
    nj = GMLP_GROUPS // GMLP_STEP_GROUPS
    return pl.pallas_call(
        _gmlp_body,
        grid=(TOKENS // tm, nj),
        in_specs=[
            pl.BlockSpec(memory_space=pl.ANY),
            _gain_spec(layer, 1),
            pl.BlockSpec((D_MODEL, GMLP_STEP_W), lambda i, j: (0, j)),
            pl.BlockSpec((D_MODEL, GMLP_STEP_W), lambda i, j: (0, j + nj)),
            pl.BlockSpec((1, GMLP_STEP_W), lambda i, j: (0, j)),
            pl.BlockSpec((1, GMLP_STEP_W), lambda i, j: (0, j)),
            pl.BlockSpec((GMLP_STEP_GROUPS, CHUNK, CHUNK), lambda i, j: (j, 0, 0)),
            pl.BlockSpec((GMLP_STEP_GROUPS, CHUNK, 1), lambda i, j: (j, 0, 0)),
            pl.BlockSpec((GMLP_STEP_W, D_MODEL), lambda i, j: (j, 0)),
        ],
        out_specs=pl.BlockSpec((tm, D_MODEL), lambda i, j: (i, 0)),
        out_shape=jax.ShapeDtypeStruct((TOKENS, D_MODEL), F32),
        scratch_shapes=[pltpu.VMEM((tm, D_MODEL), BF16), pltpu.VMEM((tm, GMLP_STEP_W), BF16),
                        pltpu.VMEM((tm, D_MODEL), F32), pltpu.SemaphoreType.DMA(())],
        compiler_params=_params(("arbitrary", "arbitrary")),
        name="gmlp",
    )(x, norm_g, w_in, w_in, ln_g, ln_b, w_s, b_s, w_out)


def _s5_disc_body(lr_ref, li_ref, ldt_ref, br_ref, bi_ref, ar_ref, ai_ref, bbr_ref, bbi_ref):
    lr, li = lr_ref[...], li_ref[...]
    dt = jnp.exp(ldt_ref[...])
    mag = jnp.exp(lr * dt)
    ang = li * dt
    a_r = mag * jnp.cos(ang)
    a_i = mag * jnp.sin(ang)
    den = lr * lr + li * li
    nr = a_r - 1.0
    z_r = (nr * lr + a_i * li) / den
    z_i = (a_i * lr - nr * li) / den
    ar_ref[...] = a_r
    ai_ref[...] = a_i
    bbr_ref[...] = z_r * br_ref[...] - z_i * bi_ref[...]
    bbi_ref[...] = z_r * bi_ref[...] + z_i * br_ref[...]


def _s5_disc(lam_re, lam_im, log_dt, b_re, b_im):
    n = SSM_GROUPS * SSM_STATE
    row = jax.ShapeDtypeStruct((1, n), F32)
    mat = jax.ShapeDtypeStruct((SSM_GROUP_DIM, n), F32)
    to_mat = lambda b: jnp.transpose(b, (2, 0, 1)).reshape(SSM_GROUP_DIM, n)
    ldt = jnp.broadcast_to(log_dt[:, None], (SSM_GROUPS, SSM_STATE)).reshape(1, n)
    return pl.pallas_call(_s5_disc_body, out_shape=(row, row, mat, mat), name="s5_disc")(
        lam_re.reshape(1, n), lam_im.reshape(1, n), ldt, to_mat(b_re), to_mat(b_im))


def _resident(shape):
    zeros = (0,) * len(shape)
    return pl.BlockSpec(shape, lambda i: zeros, pipeline_mode=pl.Buffered(1))


def _s5_in_body(x_ref, g_ref, w_ref, wo_ref, o_ref, wob_ref, wb_ref):
    @pl.when(pl.program_id(0) == 0)
    def _():
        wb_ref[...] = w_ref[...].astype(BF16)

    wob_ref[...] = wo_ref[...].astype(BF16)
    h = _rms(x_ref[...], g_ref[...]).astype(BF16)
    o_ref[...] = jnp.dot(h, wb_ref[...], preferred_element_type=F32)


def _s5_in(x, norm_g, w, w_out, layer, *, tm=512):
    steps = TOKENS // tm
    wo_block = pl.BlockSpec((D_MODEL // steps, 2 * D_MODEL), lambda i: (i, 0))
    return pl.pallas_call(
        _s5_in_body,
        grid=(steps,),
        in_specs=[
            pl.BlockSpec((tm, D_MODEL), lambda i: (i, 0)),
            pl.BlockSpec((None, None, 1, D_MODEL), lambda i: (layer, 1, 0, 0)),
            _resident((D_MODEL, D_MODEL)),
            wo_block,
        ],
        out_specs=[pl.BlockSpec((tm, D_MODEL), lambda i: (i, 0)), wo_block],
        out_shape=[jax.ShapeDtypeStruct((TOKENS, D_MODEL), F32),
                   jax.ShapeDtypeStruct((D_MODEL, 2 * D_MODEL), BF16)],
        scratch_shapes=[pltpu.VMEM((D_MODEL, D_MODEL), BF16)],
        compiler_params=_params(("arbitrary",)),
        name="s5_in",
    )(x, norm_g, w, w_out)


def _s5_scan_body(u_ref, bmat_ref, cmat_ref, ar_ref, ai_ref, d_ref, o_ref,
                  lhs_ref, bu0_ref, bu1_ref, hs0_ref, hs1_ref, yp_ref, carry_ref):
    tile = SUBLANES
    lane = lax.broadcasted_iota(jnp.int32, (tile, LANES), 1)
    low = lane < HALF_CH
    bu_refs = (bu0_ref, bu1_ref)
    hs_refs = (hs0_ref, hs1_ref)

    @pl.when(pl.program_id(1) == 0)
    def _():
        carry_ref[...] = jnp.zeros_like(carry_ref)

    re, im = slice(0, HALF_STATES), slice(HALF_STATES, 2 * HALF_STATES)
    pair_slabs = lambda pair: range(pair * SCAN_PAIR, (pair + 1) * SCAN_PAIR)
    phase_t = SCAN_T // SCAN_PHASES

    def phase_rows(phase):
        return slice(phase * phase_t * tile, (phase + 1) * phase_t * tile)

    def build_lhs(pair, phase):
        for s in pair_slabs(pair):
            for b in range(BATCH):
                for t0 in range(phase * phase_t, (phase + 1) * phase_t, tile):
                    v = u_ref[b, t0:t0 + tile, s * LANES:(s + 1) * LANES]
                    lhs_ref[s, pl.ds(t0 * tile + b, tile, stride=tile), :] = jnp.where(low, v, 0.0)
                    lhs_ref[s, pl.ds(t0 * tile + BATCH + b, tile, stride=tile), :] = jnp.where(low, 0.0, v)

    def to_states(pair, phase):
        rows = phase_rows(phase)
        for k, s in enumerate(pair_slabs(pair)):
            bu_refs[pair][k, rows, :] = jnp.dot(lhs_ref[s, rows, :].astype(BF16), bmat_ref[s],
                                                preferred_element_type=F32)

    def scan(pair, phase, state):
        bu_ref, hs_ref = bu_refs[pair], hs_refs[pair]
        for t in range(phase * phase_t, (phase + 1) * phase_t):
            rows = slice(t * tile, (t + 1) * tile)
            for k, s in enumerate(pair_slabs(pair)):
                a_r, a_i, hr, hi = state[k]
                hr, hi = ((a_r * hr - a_i * hi) + bu_ref[k, rows, re],
                          (a_r * hi + a_i * hr) + bu_ref[k, rows, im])
                hs_ref[k, rows, re] = hr
                hs_ref[k, rows, im] = hi
                state[k] = (a_r, a_i, hr, hi)

    def from_states(pair, phase):
        rows = phase_rows(phase)
        for k, s in enumerate(pair_slabs(pair)):
            yp_ref[s, rows, :] = jnp.dot(hs_refs[pair][k, rows, :].astype(BF16), cmat_ref[s],
                                         preferred_element_type=F32)

    def write_out(pair, phase):
        for s in pair_slabs(pair):
            sl = slice(s * LANES, (s + 1) * LANES)
            dsk = d_ref[:, sl]
            for b in range(BATCH):
                for t0 in range(phase * phase_t, (phase + 1) * phase_t, 2 * tile):
                    ys = []
                    for t1 in (t0, t0 + tile):
                        y0 = yp_ref[s, pl.ds(t1 * tile + b, tile, stride=tile), :]
                        y1 = yp_ref[s, pl.ds(t1 * tile + BATCH + b, tile, stride=tile), :]
                        ys.append(jnp.where(low, y0, y1) + dsk * u_ref[b, t1:t1 + tile, sl])
                    o_ref[b, t0:t0 + 2 * tile, sl] = _gelu(jnp.concatenate(ys, axis=0)).astype(BF16)

    pairs = range(SCAN_SLABS // SCAN_PAIR)
    states = [[(ar_ref[s], ai_ref[s], carry_ref[s, :, re], carry_ref[s, :, im])
               for s in pair_slabs(pair)] for pair in pairs]
    for phase in range(SCAN_PHASES):
        for pair in pairs:
            build_lhs(pair, phase)
            to_states(pair, phase)
    for phase in range(SCAN_PHASES):
        for pair in pairs:
            scan(pair, phase, states[pair])
            from_states(pair, phase)
            write_out(pair, phase)
    for pair in pairs:
        for k, s in enumerate(pair_slabs(pair)):
            carry_ref[s, :, re] = states[pair][k][2]
            carry_ref[s, :, im] = states[pair][k][3]


def _s5_scan(u, bmat, cmat, a_r, a_i, d):
    width = SCAN_SLABS * LANES
    rows = SCAN_T * SUBLANES
    pair_buf = pltpu.VMEM((SCAN_PAIR, rows, 2 * HALF_STATES), F32)
    return pl.pallas_call(
        _s5_scan_body,
        grid=(N_SLABS // SCAN_SLABS, SEQ // SCAN_T),
        in_specs=[
            pl.BlockSpec((BATCH, SCAN_T, width), lambda s, c: (0, c, s)),
            pl.BlockSpec((SCAN_SLABS, LANES, 2 * HALF_STATES), lambda s, c: (s, 0, 0)),
            pl.BlockSpec((SCAN_SLABS, 2 * HALF_STATES, LANES), lambda s, c: (s, 0, 0)),
            pl.BlockSpec((SCAN_SLABS, SUBLANES, HALF_STATES), lambda s, c: (s, 0, 0)),
            pl.BlockSpec((SCAN_SLABS, SUBLANES, HALF_STATES), lambda s, c: (s, 0, 0)),
            pl.BlockSpec((1, width), lambda s, c: (0, s)),
        ],
        out_specs=pl.BlockSpec((BATCH, SCAN_T, width), lambda s, c: (0, c, s)),
        out_shape=jax.ShapeDtypeStruct((BATCH, SEQ, D_MODEL), BF16),
        scratch_shapes=[
            pltpu.VMEM((SCAN_SLABS, rows, LANES), F32),
            pair_buf, pair_buf, pair_buf, pair_buf,
            pltpu.VMEM((SCAN_SLABS, rows, LANES), F32),
            pltpu.VMEM((SCAN_SLABS, SUBLANES, 2 * HALF_STATES), F32),
        ],
        compiler_params=_params(("parallel", "arbitrary")),
        name="s5_scan",
    )(u, bmat, cmat, a_r, a_i, d)


def _s5_scan_operands(a_r, a_i, bb_r, bb_i, c_re, c_im):
    halves = SLAB_GROUPS // 2
    eye = jnp.eye(halves, dtype=F32)

    def b_blocks(bb):
        bb = bb.reshape(SSM_GROUP_DIM, N_SLABS, 2, halves, SSM_STATE)
        blk = jnp.einsum('hsqgp,gk->sqghkp', bb, eye)
        return blk.reshape(N_SLABS, LANES, HALF_STATES)

    def c_blocks(c):
        c = c.reshape(N_SLABS, 2, halves, SSM_GROUP_DIM, SSM_STATE)
        blk = jnp.einsum('sqghp,gk->skpqgh', c, eye)
        return blk.reshape(N_SLABS, HALF_STATES, LANES)

    bmat = jnp.concatenate([b_blocks(bb_r), b_blocks(bb_i)], axis=-1).astype(BF16)
    cmat = jnp.concatenate([c_blocks(c_re), -c_blocks(c_im)], axis=1).astype(BF16)

    def a_rows(a):
        a = a.reshape(N_SLABS, 2, 1, HALF_STATES)
        return jnp.broadcast_to(a, (N_SLABS, 2, BATCH, HALF_STATES)).reshape(
            N_SLABS, SUBLANES, HALF_STATES)

    return bmat, cmat, a_rows(a_r), a_rows(a_i)


def _s5_out_body(x_ref, a_ref, w_ref, o_ref, *, tn):
    a = a_ref[...]
    for c in range(0, D_MODEL, tn):
        val = jnp.dot(a, w_ref[:, c:c + tn], preferred_element_type=F32)
        gate = jnp.dot(a, w_ref[:, D_MODEL + c:D_MODEL + c + tn], preferred_element_type=F32)
        o_ref[:, c:c + tn] = x_ref[:, c:c + tn] + val * _sigmoid(gate)


def _s5_out(x, act, w, *, tm=512, tn=512):
    return pl.pallas_call(
        functools.partial(_s5_out_body, tn=tn),
        grid=(TOKENS // tm,),
        in_specs=[
            pl.BlockSpec((tm, D_MODEL), lambda i: (i, 0)),
            pl.BlockSpec((tm, D_MODEL), lambda i: (i, 0)),
            _resident((D_MODEL, 2 * D_MODEL)),
        ],
        out_specs=pl.BlockSpec((tm, D_MODEL), lambda i: (i, 0)),
        out_shape=jax.ShapeDtypeStruct((TOKENS, D_MODEL), F32),
        compiler_params=_params(("parallel",)),
        name="s5_out",
    )(x, act, w)


def _s5(x, norm_g, layer, w_in, lam_re, lam_im, log_dt, b_re, b_im, c_re, c_im, d_skip, w_out):
    a_r, a_i, bb_r, bb_i = _s5_disc(lam_re, lam_im, log_dt, b_re, b_im)
    bmat, cmat, ar_rows, ai_rows = _s5_scan_operands(a_r, a_i, bb_r, bb_i, c_re, c_im)
    u, w_out_bf = _s5_in(x, norm_g, w_in, w_out, layer)
    act = _s5_scan(u.reshape(BATCH, SEQ, D_MODEL), bmat, cmat, ar_rows, ai_rows,
                   d_skip.reshape(1, D_MODEL))
    return _s5_out(x, act.reshape(TOKENS, D_MODEL), w_out_bf)


def _ple_body(x_ref, g_ref, wg_ref, p_ref, wp_ref, fg_ref, o_ref, wgb_ref, wpb_ref, *, final):
    @pl.when(pl.program_id(0) == 0)
    def _():
        wgb_ref[...] = wg_ref[...].astype(BF16)
        wpb_ref[...] = wp_ref[...].astype(BF16)

    x = x_ref[...]
    h = _rms(x, g_ref[...]).astype(BF16)
    gate = _sigmoid(jnp.dot(h, wgb_ref[...], preferred_element_type=F32))
    proj = jnp.dot(p_ref[...].astype(BF16), wpb_ref[...], preferred_element_type=F32)
    y = x + gate * proj
    o_ref[...] = _rms(y, fg_ref[...]) if final else y


def _ple(x, norm_g, wg, p, wp, fg, layer, *, final, tm=512):
    once = pl.Buffered(1)
    return pl.pallas_call(
        functools.partial(_ple_body, final=final),
        grid=(TOKENS // tm,),
        in_specs=[
            pl.BlockSpec((tm, D_MODEL), lambda i: (i, 0)),
            pl.BlockSpec((None, None, 1, D_MODEL), lambda i: (layer, 3, 0, 0)),
            pl.BlockSpec((None, D_MODEL, D_MODEL), lambda i: (layer, 0, 0), pipeline_mode=once),
            pl.BlockSpec((None, tm, PLE_DIM), lambda i: (layer, i, 0)),
            pl.BlockSpec((None, PLE_DIM, D_MODEL), lambda i: (layer, 0, 0), pipeline_mode=once),
            pl.BlockSpec((1, D_MODEL), lambda i: (0, 0)),
        ],
        out_specs=pl.BlockSpec((tm, D_MODEL), lambda i: (i, 0)),
        out_shape=jax.ShapeDtypeStruct((TOKENS, D_MODEL), F32),
        scratch_shapes=[pltpu.VMEM((D_MODEL, D_MODEL), BF16), pltpu.VMEM((PLE_DIM, D_MODEL), BF16)],
        compiler_params=_params(("arbitrary",)),
        name="ple",
    )(x, norm_g, wg, p, wp, fg)


def kernel(x, p, norm_g, final_norm_g, ffn_w_gate, ffn_w_up, ffn_w_down, gmlp_w_in, gmlp_ln_g, gmlp_ln_b, gmlp_w_s, gmlp_b_s, gmlp_w_out, s5_w_in, s5_lam_re, s5_lam_im, s5_log_dt, s5_b_re, s5_b_im, s5_c_re, s5_c_im, s5_d, s5_w_out, ple_w_gate, ple_w_proj):
    x = x.reshape(TOKENS, D_MODEL)
    p = p.reshape(DEPTH, TOKENS, PLE_DIM)
    norm_g = norm_g.reshape(DEPTH, 4, 1, D_MODEL)
    fg = final_norm_g.reshape(1, D_MODEL)
    for i in range(DEPTH):
        j = i // 2
        if i % 2 == 0:
            side = ((gmlp_w_in[j], GMLP_CAST_BLOCKS), (gmlp_w_out[j], GMLP_CAST_BLOCKS))
            x, (w_in, w_out) = _ffn(x, norm_g, ffn_w_gate, ffn_w_up, ffn_w_down, i, 0,
                                    side=side, tf=256)
            x = _gmlp(x, norm_g, w_in, gmlp_ln_g[j].reshape(1, GMLP_HALF),
                      gmlp_ln_b[j].reshape(1, GMLP_HALF), gmlp_w_s[j],
                      gmlp_b_s[j].reshape(GMLP_GROUPS, CHUNK, 1), w_out, i)
        else:
            x, _ = _ffn(x, norm_g, ffn_w_gate, ffn_w_up, ffn_w_down, i, 0)
            x = _s5(x, norm_g, i, s5_w_in[j], s5_lam_re[j], s5_lam_im[j], s5_log_dt[j],
                    s5_b_re[j], s5_b_im[j], s5_c_re[j], s5_c_im[j], s5_d[j], s5_w_out[j])
        x, _ = _ffn(x, norm_g, ffn_w_gate, ffn_w_up, ffn_w_down, i, 1)
        x = _ple(x, norm_g, ple_w_gate, p, ple_w_proj, fg, i, final=(i == DEPTH - 1))
    return x.reshape(BATCH, SEQ, D_MODEL)
```

```python
import functools
import math

import jax
import jax.numpy as jnp
from jax import lax
from jax.experimental import pallas as pl
from jax.experimental.pallas import tpu as pltpu

F32 = jnp.float32
BF16 = jnp.bfloat16

D_MODEL = 2048
BATCH = 4
SEQ = 2048
TOKENS = BATCH * SEQ
DEPTH = 2
D_FF = 5632
PLE_DIM = 256
CHUNK = 128
GMLP_HALF = 3 * D_MODEL
GMLP_GROUPS = 16
GMLP_GROUP_DIM = GMLP_HALF // GMLP_GROUPS
SSM_GROUP_DIM = 16
SSM_GROUPS = D_MODEL // SSM_GROUP_DIM
SSM_STATE = 64
EPS = 1e-6

LANES = 128
SUBLANES = 8
VMEM_LIMIT = 60 * 1024 * 1024

SLAB_GROUPS = LANES // SSM_GROUP_DIM
N_SLABS = D_MODEL // LANES
SLAB_STATES = SLAB_GROUPS * SSM_STATE
HALF_STATES = SLAB_STATES // 2
HALF_CH = LANES // 2
SCAN_PAIR = 2
SCAN_SLABS = 2 * SCAN_PAIR
SCAN_T = 128


def _rms(x, g):
    ms = jnp.mean(x * x, axis=-1, keepdims=True)
    return x * lax.rsqrt(ms + EPS) * g


def _gelu(x):
    c = math.sqrt(2.0 / math.pi)
    return x * (0.5 * (1.0 + jnp.tanh(c * (x + 0.044715 * (x * x * x)))))


def _sigmoid(x):
    return 1.0 / (1.0 + jnp.exp(-x))


def _params(sem):
    return pltpu.CompilerParams(dimension_semantics=sem, vmem_limit_bytes=VMEM_LIMIT)


def _gain_spec(layer, slot):
    return pl.BlockSpec((None, None, 1, D_MODEL), lambda i, j: (layer, slot, 0, 0))


def _token_tile_copy(x_hbm, x_ref, sem, tile):
    tm = x_ref.shape[0]
    rows = pl.ds(pl.multiple_of(tile * tm, tm), tm)
    return pltpu.make_async_copy(x_hbm.at[rows, :], x_ref, sem)


def _wait_token_tile(x_hbm, x_ref, sem):
    i = pl.program_id(0)

    @pl.when(i == 0)
    def _():
        _token_tile_copy(x_hbm, x_ref, sem, i).start()

    _token_tile_copy(x_hbm, x_ref, sem, i).wait()


def _prefetch_next_token_tile(x_hbm, x_ref, sem):
    i, j = pl.program_id(0), pl.program_id(1)

    @pl.when(jnp.logical_and(j == 1, i + 1 < pl.num_programs(0)))
    def _():
        _token_tile_copy(x_hbm, x_ref, sem, i + 1).start()


def _ffn_body(*refs, n_side):
    x_hbm, g_ref, wg_ref, wu_ref, wd_ref = refs[:5]
    side_in = refs[5:5 + n_side]
    o_ref = refs[5 + n_side]
    side_out = refs[6 + n_side:6 + 2 * n_side]
    h_ref, x_ref, sem = refs[6 + 2 * n_side:]

    @pl.when(pl.program_id(1) == 0)
    def _():
        _wait_token_tile(x_hbm, x_ref, sem)
        x = x_ref[...]
        h_ref[...] = _rms(x, g_ref[...]).astype(BF16)
        o_ref[...] = x

    _prefetch_next_token_tile(x_hbm, x_ref, sem)

    for src, dst in zip(side_in, side_out):
        dst[...] = src[...].astype(BF16)

    h = h_ref[...]
    gate = jnp.dot(h, wg_ref[...].astype(BF16), preferred_element_type=F32)
    up = jnp.dot(h, wu_ref[...].astype(BF16), preferred_element_type=F32)
    a = (gate * _sigmoid(gate)) * (0.5 * up)
    o_ref[...] += jnp.dot(a.astype(BF16), wd_ref[...].astype(BF16), preferred_element_type=F32)


def _ffn(x, norm_g, wg, wu, wd, layer, which, *, side=(), tm=1024, tf=512):
    ni, nj = TOKENS // tm, D_FF // tf
    side_specs, side_shapes = [], []
    for arr, n_blocks in side:
        assert n_blocks <= ni * nj and arr.shape[0] % n_blocks == 0
        block = (arr.shape[0] // n_blocks, arr.shape[1])
        assert block[0] % (2 * SUBLANES) == 0
        index = functools.partial(
            lambda i, j, last: (jnp.minimum(i * nj + j, last), 0), last=n_blocks - 1)
        side_specs.append(pl.BlockSpec(block, index))
        side_shapes.append(jax.ShapeDtypeStruct(arr.shape, BF16))
    out = pl.pallas_call(
        functools.partial(_ffn_body, n_side=len(side)),
        grid=(ni, nj),
        in_specs=[
            pl.BlockSpec(memory_space=pl.ANY),
            _gain_spec(layer, 2 * which),
            pl.BlockSpec((None, None, D_MODEL, tf), lambda i, j: (layer, which, 0, j)),
            pl.BlockSpec((None, None, D_MODEL, tf), lambda i, j: (layer, which, 0, j)),
            pl.BlockSpec((None, None, tf, D_MODEL), lambda i, j: (layer, which, j, 0)),
        ] + side_specs,
        out_specs=[pl.BlockSpec((tm, D_MODEL), lambda i, j: (i, 0))] + side_specs,
        out_shape=[jax.ShapeDtypeStruct((TOKENS, D_MODEL), F32)] + side_shapes,
        scratch_shapes=[pltpu.VMEM((tm, D_MODEL), BF16), pltpu.VMEM((tm, D_MODEL), F32),
                        pltpu.SemaphoreType.DMA(())],
        compiler_params=_params(("arbitrary", "arbitrary")),
        name="ffn",
    )(x, norm_g, wg, wu, wd, *[arr for arr, _ in side])
    return out[0], out[1:]


GMLP_STEP_GROUPS = 2
GMLP_CAST_BLOCKS = 128
GMLP_SUBTILE = 512
GMLP_STEP_W = GMLP_STEP_GROUPS * GMLP_GROUP_DIM


def _gmlp_body(x_hbm, g_ref, wu_ref, wv_ref, lng_ref, lnb_ref, ws_ref, bs_ref, wo_ref,
               o_ref, h_ref, gated_ref, x_ref, sem):
    @pl.when(pl.program_id(1) == 0)
    def _():
        _wait_token_tile(x_hbm, x_ref, sem)
        x = x_ref[...]
        h_ref[...] = _rms(x, g_ref[...]).astype(BF16)
        o_ref[...] = x

    _prefetch_next_token_tile(x_hbm, x_ref, sem)

    tm = x_ref.shape[0]
    subs = [slice(r, r + GMLP_SUBTILE) for r in range(0, tm, GMLP_SUBTILE)]
    z = []
    for rows in subs:
        h = h_ref[rows, :]
        zv = _gelu(jnp.dot(h, wv_ref[...], preferred_element_type=F32))
        zu = _gelu(jnp.dot(h, wu_ref[...], preferred_element_type=F32))
        z.append((zu, zv))
    row = lax.broadcasted_iota(jnp.int32, (CHUNK, CHUNK), 0)
    col = lax.broadcasted_iota(jnp.int32, (CHUNK, CHUNK), 1)
    ws = [jnp.where(row >= col, ws_ref[gi], 0.0).astype(BF16)
          for gi in range(GMLP_STEP_GROUPS)]
    for rows, (zu, zv) in zip(subs, z):
        for gi in range(GMLP_STEP_GROUPS):
            lo, hi = gi * GMLP_GROUP_DIM, (gi + 1) * GMLP_GROUP_DIM
            v = zv[:, lo:hi]
            d = v - jnp.mean(v, axis=-1, keepdims=True)
            var = jnp.mean(d * d, axis=-1, keepdims=True)
            vn = ((d * lax.rsqrt(var + EPS)) * lng_ref[:, lo:hi] + lnb_ref[:, lo:hi]).astype(BF16)
            bias = bs_ref[gi]
            for c in range(0, GMLP_SUBTILE, CHUNK):
                sv = jnp.dot(ws[gi], vn[c:c + CHUNK], preferred_element_type=F32) + bias
                out_rows = slice(rows.start + c, rows.start + c + CHUNK)
                gated_ref[out_rows, lo:hi] = (zu[c:c + CHUNK, lo:hi] * sv).astype(BF16)
    for rows in subs:
        o_ref[rows, :] += jnp.dot(gated_ref[rows, :], wo_ref[...], preferred_element_type=F32)


def _gmlp(x, norm_g, w_in, ln_g, ln_b, w_s, b_s, w_out, layer, *, tm=1024):
    nj = GMLP_GROUPS // GMLP_STEP_GROUPS
    return pl.pallas_call(
        _gmlp_body,
        grid=(TOKENS // tm, nj),
        in_specs=[
            pl.BlockSpec(memory_space=pl.ANY),
            _gain_spec(layer, 1),
            pl.BlockSpec((D_MODEL, GMLP_STEP_W), lambda i, j: (0, j)),
            pl.BlockSpec((D_MODEL, GMLP_STEP_W), lambda i, j: (0, j + nj)),
            pl.BlockSpec((1, GMLP_STEP_W), lambda i, j: (0, j)),
            pl.BlockSpec((1, GMLP_STEP_W), lambda i, j: (0, j)),
            pl.BlockSpec((GMLP_STEP_GROUPS, CHUNK, CHUNK), lambda i, j: (j, 0, 0)),
            pl.BlockSpec((GMLP_STEP_GROUPS, CHUNK, 1), lambda i, j: (j, 0, 0)),
            pl.BlockSpec((GMLP_STEP_W, D_MODEL), lambda i, j: (j, 0)),
        ],
        out_specs=pl.BlockSpec((tm, D_MODEL), lambda i, j: (i, 0)),
        out_shape=jax.ShapeDtypeStruct((TOKENS, D_MODEL), F32),
        scratch_shapes=[pltpu.VMEM((tm, D_MODEL), BF16), pltpu.VMEM((tm, GMLP_STEP_W), BF16),
                        pltpu.VMEM((tm, D_MODEL), F32), pltpu.SemaphoreType.DMA(())],
        compiler_params=_params(("arbitrary", "arbitrary")),
        name="gmlp",
    )(x, norm_g, w_in, w_in, ln_g, ln_b, w_s, b_s, w_out)


def _s5_disc_body(lr_ref, li_ref, ldt_ref, br_ref, bi_ref, ar_ref, ai_ref, bbr_ref, bbi_ref):
    lr, li = lr_ref[...], li_ref[...]
    dt = jnp.exp(ldt_ref[...])
    mag = jnp.exp(lr * dt)
    ang = li * dt
    a_r = mag * jnp.cos(ang)
    a_i = mag * jnp.sin(ang)
    den = lr * lr + li * li
    nr = a_r - 1.0
    z_r = (nr * lr + a_i * li) / den
    z_i = (a_i * lr - nr * li) / den
    ar_ref[...] = a_r
    ai_ref[...] = a_i
    bbr_ref[...] = z_r * br_ref[...] - z_i * bi_ref[...]
    bbi_ref[...] = z_r * bi_ref[...] + z_i * br_ref[...]


def _s5_disc(lam_re, lam_im, log_dt, b_re, b_im):
    n = SSM_GROUPS * SSM_STATE
    row = jax.ShapeDtypeStruct((1, n), F32)
    mat = jax.ShapeDtypeStruct((SSM_GROUP_DIM, n), F32)
    to_mat = lambda b: jnp.transpose(b, (2, 0, 1)).reshape(SSM_GROUP_DIM, n)
    ldt = jnp.broadcast_to(log_dt[:, None], (SSM_GROUPS, SSM_STATE)).reshape(1, n)
    return pl.pallas_call(_s5_disc_body, out_shape=(row, row, mat, mat), name="s5_disc")(
        lam_re.reshape(1, n), lam_im.reshape(1, n), ldt, to_mat(b_re), to_mat(b_im))


def _resident(shape):
    zeros = (0,) * len(shape)
    return pl.BlockSpec(shape, lambda i: zeros, pipeline_mode=pl.Buffered(1))


def _s5_in_body(x_ref, g_ref, w_ref, wo_ref, o_ref, wob_ref, wb_ref):
    @pl.when(pl.program_id(0) == 0)
    def _():
        wb_ref[...] = w_ref[...].astype(BF16)

    wob_ref[...] = wo_ref[...].astype(BF16)
    h = _rms(x_ref[...], g_ref[...]).astype(BF16)
    o_ref[...] = jnp.dot(h, wb_ref[...], preferred_element_type=F32)


def _s5_in(x, norm_g, w, w_out, layer, *, tm=512):
    steps = TOKENS // tm
    wo_block = pl.BlockSpec((D_MODEL // steps, 2 * D_MODEL), lambda i: (i, 0))
    return pl.pallas_call(
        _s5_in_body,
        grid=(steps,),
        in_specs=[
            pl.BlockSpec((tm, D_MODEL), lambda i: (i, 0)),
            pl.BlockSpec((None, None, 1, D_MODEL), lambda i: (layer, 1, 0, 0)),
            _resident((D_MODEL, D_MODEL)),
            wo_block,
        ],
        out_specs=[pl.BlockSpec((tm, D_MODEL), lambda i: (i, 0)), wo_block],
        out_shape=[jax.ShapeDtypeStruct((TOKENS, D_MODEL), F32),
                   jax.ShapeDtypeStruct((D_MODEL, 2 * D_MODEL), BF16)],
        scratch_shapes=[pltpu.VMEM((D_MODEL, D_MODEL), BF16)],
        compiler_params=_params(("arbitrary",)),
        name="s5_in",
    )(x, norm_g, w, w_out)


def _s5_scan_body(u_ref, bmat_ref, cmat_ref, ar_ref, ai_ref, d_ref, o_ref,
                  lhs_ref, bu0_ref, bu1_ref, hs0_ref, hs1_ref, yp_ref, carry_ref):
    tile = SUBLANES
    lane = lax.broadcasted_iota(jnp.int32, (tile, LANES), 1)
    low = lane < HALF_CH
    bu_refs = (bu0_ref, bu1_ref)
    hs_refs = (hs0_ref, hs1_ref)

    @pl.when(pl.program_id(1) == 0)
    def _():
        carry_ref[...] = jnp.zeros_like(carry_ref)

    re, im = slice(0, HALF_STATES), slice(HALF_STATES, 2 * HALF_STATES)
    pair_slabs = lambda pair: range(pair * SCAN_PAIR, (pair + 1) * SCAN_PAIR)

    def to_states(pair):
        for k, s in enumerate(pair_slabs(pair)):
            for b in range(BATCH):
                for t0 in range(0, SCAN_T, tile):
                    v = u_ref[b, t0:t0 + tile, s * LANES:(s + 1) * LANES]
                    lhs_ref[s, pl.ds(t0 * tile + b, tile, stride=tile), :] = jnp.where(low, v, 0.0)
                    lhs_ref[s, pl.ds(t0 * tile + BATCH + b, tile, stride=tile), :] = jnp.where(low, 0.0, v)
            bu_refs[pair][k] = jnp.dot(lhs_ref[s].astype(BF16), bmat_ref[s],
                                       preferred_element_type=F32)

    def scan(pair):
        bu_ref, hs_ref = bu_refs[pair], hs_refs[pair]
        state = [(ar_ref[s], ai_ref[s], carry_ref[s, :, re], carry_ref[s, :, im])
                 for s in pair_slabs(pair)]
        for t in range(0, SCAN_T, 2):
            for k in range(SCAN_PAIR):
                a_r, a_i, hr, hi = state[k]
                steps = []
                for rows in (slice(t * tile, (t + 1) * tile), slice((t + 1) * tile, (t + 2) * tile)):
                    hr, hi = ((a_r * hr - a_i * hi) + bu_ref[k, rows, re],
                              (a_r * hi + a_i * hr) + bu_ref[k, rows, im])
                    steps.append((hr, hi))
                both = slice(t * tile, (t + 2) * tile)
                hs_ref[k, both, re] = jnp.concatenate([steps[0][0], steps[1][0]], axis=0).astype(BF16)
                hs_ref[k, both, im] = jnp.concatenate([steps[0][1], steps[1][1]], axis=0).astype(BF16)
                state[k] = (a_r, a_i, hr, hi)
        for k, s in enumerate(pair_slabs(pair)):
            carry_ref[s, :, re] = state[k][2]
            carry_ref[s, :, im] = state[k][3]

    def from_states(pair):
        for k, s in enumerate(pair_slabs(pair)):
            yp_ref[s] = jnp.dot(hs_refs[pair][k], cmat_ref[s], preferred_element_type=F32)
            sl = slice(s * LANES, (s + 1) * LANES)
            dsk = d_ref[:, sl]
            for b in range(BATCH):
                for t0 in range(0, SCAN_T, 2 * tile):
                    ys = []
                    for t1 in (t0, t0 + tile):
                        y0 = yp_ref[s, pl.ds(t1 * tile + b, tile, stride=tile), :]
                        y1 = yp_ref[s, pl.ds(t1 * tile + BATCH + b, tile, stride=tile), :]
                        ys.append(jnp.where(low, y0, y1) + dsk * u_ref[b, t1:t1 + tile, sl])
                    o_ref[b, t0:t0 + 2 * tile, sl] = _gelu(jnp.concatenate(ys, axis=0)).astype(BF16)

    pairs = range(SCAN_SLABS // SCAN_PAIR)
    for pair in pairs:
        to_states(pair)
    for pair in pairs:
        scan(pair)
        from_states(pair)


def _s5_scan(u, bmat, cmat, a_r, a_i, d):
    width = SCAN_SLABS * LANES
    rows = SCAN_T * SUBLANES
    bu_buf = pltpu.VMEM((SCAN_PAIR, rows, 2 * HALF_STATES), F32)
    hs_buf = pltpu.VMEM((SCAN_PAIR, rows, 2 * HALF_STATES), BF16)
    return pl.pallas_call(
        _s5_scan_body,
        grid=(N_SLABS // SCAN_SLABS, SEQ // SCAN_T),
        in_specs=[
            pl.BlockSpec((BATCH, SCAN_T, width), lambda s, c: (0, c, s)),
            pl.BlockSpec((SCAN_SLABS, LANES, 2 * HALF_STATES), lambda s, c: (s, 0, 0)),
            pl.BlockSpec((SCAN_SLABS, 2 * HALF_STATES, LANES), lambda s, c: (s, 0, 0)),
            pl.BlockSpec((SCAN_SLABS, SUBLANES, HALF_STATES), lambda s, c: (s, 0, 0)),
            pl.BlockSpec((SCAN_SLABS, SUBLANES, HALF_STATES), lambda s, c: (s, 0, 0)),
            pl.BlockSpec((1, width), lambda s, c: (0, s)),
        ],
        out_specs=pl.BlockSpec((BATCH, SCAN_T, width), lambda s, c: (0, c, s)),
        out_shape=jax.ShapeDtypeStruct((BATCH, SEQ, D_MODEL), BF16),
        scratch_shapes=[
            pltpu.VMEM((SCAN_SLABS, rows, LANES), F32),
            bu_buf, bu_buf, hs_buf, hs_buf,
            pltpu.VMEM((SCAN_SLABS, rows, LANES), F32),
            pltpu.VMEM((SCAN_SLABS, SUBLANES, 2 * HALF_STATES), F32),
        ],
        compiler_params=_params(("parallel", "arbitrary")),
        name="s5_scan",
    )(u, bmat, cmat, a_r, a_i, d)


def _s5_scan_operands(a_r, a_i, bb_r, bb_i, c_re, c_im):
    halves = SLAB_GROUPS // 2
    eye = jnp.eye(halves, dtype=F32)

    def b_blocks(bb):
        bb = bb.reshape(SSM_GROUP_DIM, N_SLABS, 2, halves, SSM_STATE)
        blk = jnp.einsum('hsqgp,gk->sqghkp', bb, eye)
        return blk.reshape(N_SLABS, LANES, HALF_STATES)

    def c_blocks(c):
        c = c.reshape(N_SLABS, 2, halves, SSM_GROUP_DIM, SSM_STATE)
        blk = jnp.einsum('sqghp,gk->skpqgh', c, eye)
        return blk.reshape(N_SLABS, HALF_STATES, LANES)

    bmat = jnp.concatenate([b_blocks(bb_r), b_blocks(bb_i)], axis=-1).astype(BF16)
    cmat = jnp.concatenate([c_blocks(c_re), -c_blocks(c_im)], axis=1).astype(BF16)

    def a_rows(a):
        a = a.reshape(N_SLABS, 2, 1, HALF_STATES)
        return jnp.broadcast_to(a, (N_SLABS, 2, BATCH, HALF_STATES)).reshape(
            N_SLABS, SUBLANES, HALF_STATES)

    return bmat, cmat, a_rows(a_r), a_rows(a_i)


def _s5_out_body(x_ref, a_ref, w_ref, o_ref, *, tn):
    a = a_ref[...]
    for c in range(0, D_MODEL, tn):
        val = jnp.dot(a, w_ref[:, c:c + tn], preferred_element_type=F32)
        gate = jnp.dot(a, w_ref[:, D_MODEL + c:D_MODEL + c + tn], preferred_element_type=F32)
        o_ref[:, c:c + tn] = x_ref[:, c:c + tn] + val * _sigmoid(gate)


def _s5_out(x, act, w, *, tm=512, tn=512):
    return pl.pallas_call(
        functools.partial(_s5_out_body, tn=tn),
        grid=(TOKENS // tm,),
        in_specs=[
            pl.BlockSpec((tm, D_MODEL), lambda i: (i, 0)),
            pl.BlockSpec((tm, D_MODEL), lambda i: (i, 0)),
            _resident((D_MODEL, 2 * D_MODEL)),
        ],
        out_specs=pl.BlockSpec((tm, D_MODEL), lambda i: (i, 0)),
        out_shape=jax.ShapeDtypeStruct((TOKENS, D_MODEL), F32),
        compiler_params=_params(("parallel",)),
        name="s5_out",
    )(x, act, w)


def _s5(x, norm_g, layer, w_in, lam_re, lam_im, log_dt, b_re, b_im, c_re, c_im, d_skip, w_out):
    a_r, a_i, bb_r, bb_i = _s5_disc(lam_re, lam_im, log_dt, b_re, b_im)
    bmat, cmat, ar_rows, ai_rows = _s5_scan_operands(a_r, a_i, bb_r, bb_i, c_re, c_im)
    u, w_out_bf = _s5_in(x, norm_g, w_in, w_out, layer)
    act = _s5_scan(u.reshape(BATCH, SEQ, D_MODEL), bmat, cmat, ar_rows, ai_rows,
                   d_skip.reshape(1, D_MODEL))
    return _s5_out(x, act.reshape(TOKENS, D_MODEL), w_out_bf)


def _ple_body(x_ref, g_ref, wg_ref, p_ref, wp_ref, fg_ref, o_ref, wgb_ref, wpb_ref, *, final):
    @pl.when(pl.program_id(0) == 0)
    def _():
        wgb_ref[...] = wg_ref[...].astype(BF16)
        wpb_ref[...] = wp_ref[...].astype(BF16)

    x = x_ref[...]
    h = _rms(x, g_ref[...]).astype(BF16)
    gate = _sigmoid(jnp.dot(h, wgb_ref[...], preferred_element_type=F32))
    proj = jnp.dot(p_ref[...].astype(BF16), wpb_ref[...], preferred_element_type=F32)
    y = x + gate * proj
    o_ref[...] = _rms(y, fg_ref[...]) if final else y


def _ple(x, norm_g, wg, p, wp, fg, layer, *, final, tm=512):
    once = pl.Buffered(1)
    return pl.pallas_call(
        functools.partial(_ple_body, final=final),
        grid=(TOKENS // tm,),
        in_specs=[
            pl.BlockSpec((tm, D_MODEL), lambda i: (i, 0)),
            pl.BlockSpec((None, None, 1, D_MODEL), lambda i: (layer, 3, 0, 0)),
            pl.BlockSpec((None, D_MODEL, D_MODEL), lambda i: (layer, 0, 0), pipeline_mode=once),
            pl.BlockSpec((None, tm, PLE_DIM), lambda i: (layer, i, 0)),
            pl.BlockSpec((None, PLE_DIM, D_MODEL), lambda i: (layer, 0, 0), pipeline_mode=once),
            pl.BlockSpec((1, D_MODEL), lambda i: (0, 0)),
        ],
        out_specs=pl.BlockSpec((tm, D_MODEL), lambda i: (i, 0)),
        out_shape=jax.ShapeDtypeStruct((TOKENS, D_MODEL), F32),
        scratch_shapes=[pltpu.VMEM((D_MODEL, D_MODEL), BF16), pltpu.VMEM((PLE_DIM, D_MODEL), BF16)],
        compiler_params=_params(("arbitrary",)),
        name="ple",
    )(x, norm_g, wg, p, wp, fg)


def kernel(x, p, norm_g, final_norm_g, ffn_w_gate, ffn_w_up, ffn_w_down, gmlp_w_in, gmlp_ln_g, gmlp_ln_b, gmlp_w_s, gmlp_b_s, gmlp_w_out, s5_w_in, s5_lam_re, s5_lam_im, s5_log_dt, s5_b_re, s5_b_im, s5_c_re, s5_c_im, s5_d, s5_w_out, ple_w_gate, ple_w_proj):
    x = x.reshape(TOKENS, D_MODEL)
    p = p.reshape(DEPTH, TOKENS, PLE_DIM)
    norm_g = norm_g.reshape(DEPTH, 4, 1, D_MODEL)
    fg = final_norm_g.reshape(1, D_MODEL)
    for i in range(DEPTH):
        j = i // 2
        if i % 2 == 0:
            side = ((gmlp_w_in[j], GMLP_CAST_BLOCKS), (gmlp_w_out[j], GMLP_CAST_BLOCKS))
            x, (w_in, w_out) = _ffn(x, norm_g, ffn_w_gate, ffn_w_up, ffn_w_down, i, 0,
                                    side=side, tf=256)
            x = _gmlp(x, norm_g, w_in, gmlp_ln_g[j].reshape(1, GMLP_HALF),
                      gmlp_ln_b[j].reshape(1, GMLP_HALF), gmlp_w_s[j],
                      gmlp_b_s[j].reshape(GMLP_GROUPS, CHUNK, 1), w_out, i)
        else:
            x, _ = _ffn(x, norm_g, ffn_w_gate, ffn_w_up, ffn_w_down, i, 0)
            x = _s5(x, norm_g, i, s5_w_in[j], s5_lam_re[j], s5_lam_im[j], s5_log_dt[j],
                    s5_b_re[j], s5_b_im[j], s5_c_re[j], s5_c_im[j], s5_d[j], s5_w_out[j])
        x, _ = _ffn(x, norm_g, ffn_w_gate, ffn_w_up, ffn_w_down, i, 1)
        x = _ple(x, norm_g, ple_w_gate, p, ple_w_proj, fg, i, final=(i == DEPTH - 1))
    return x.reshape(BATCH, SEQ, D_MODEL)
```

```python
import functools
import math

import jax
import jax.numpy as jnp
from jax import lax
from jax.experimental import pallas as pl
from jax.experimental.pallas import tpu as pltpu

F32 = jnp.float32
BF16 = jnp.bfloat16

D_MODEL = 2048
BATCH = 4
SEQ = 2048
TOKENS = BATCH * SEQ
DEPTH = 2
D_FF = 5632
PLE_DIM = 256
CHUNK = 128
GMLP_HALF = 3 * D_MODEL
GMLP_GROUPS = 16
GMLP_GROUP_DIM = GMLP_HALF // GMLP_GROUPS
SSM_GROUP_DIM = 16
SSM_GROUPS = D_MODEL // SSM_GROUP_DIM
SSM_STATE = 64
EPS = 1e-6

LANES = 128
SUBLANES = 8
VMEM_LIMIT = 60 * 1024 * 1024

SLAB_GROUPS = LANES // SSM_GROUP_DIM
N_SLABS = D_MODEL // LANES
SLAB_STATES = SLAB_GROUPS * SSM_STATE
HALF_STATES = SLAB_STATES // 2
HALF_CH = LANES // 2
SCAN_PAIR = 2
SCAN_SLABS = 2 * SCAN_PAIR
SCAN_T = 256


def _rms(x, g):
    ms = jnp.mean(x * x, axis=-1, keepdims=True)
    return x * lax.rsqrt(ms + EPS) * g


def _gelu(x):
    c = math.sqrt(2.0 / math.pi)
    return x * (0.5 * (1.0 + jnp.tanh(c * (x + 0.044715 * (x * x * x)))))


def _sigmoid(x):
    return 1.0 / (1.0 + jnp.exp(-x))


def _params(sem):
    return pltpu.CompilerParams(dimension_semantics=sem, vmem_limit_bytes=VMEM_LIMIT)


def _gain_spec(layer, slot):
    return pl.BlockSpec((None, None, 1, D_MODEL), lambda i, j: (layer, slot, 0, 0))


def _token_tile_copy(x_hbm, x_ref, sem, tile):
    tm = x_ref.shape[0]
    rows = pl.ds(pl.multiple_of(tile * tm, tm), tm)
    return pltpu.make_async_copy(x_hbm.at[rows, :], x_ref, sem)


def _wait_token_tile(x_hbm, x_ref, sem):
    i = pl.program_id(0)

    @pl.when(i == 0)
    def _():
        _token_tile_copy(x_hbm, x_ref, sem, i).start()

    _token_tile_copy(x_hbm, x_ref, sem, i).wait()


def _prefetch_next_token_tile(x_hbm, x_ref, sem):
    i, j = pl.program_id(0), pl.program_id(1)

    @pl.when(jnp.logical_and(j == 1, i + 1 < pl.num_programs(0)))
    def _():
        _token_tile_copy(x_hbm, x_ref, sem, i + 1).start()


FFN_SUBTILE = 512


def _ffn_body(*refs, n_side):
    x_hbm, g_ref, wg_ref, wu_ref, wd_ref = refs[:5]
    side_in = refs[5:5 + n_side]
    o_ref = refs[5 + n_side]
    side_out = refs[6 + n_side:6 + 2 * n_side]
    h_ref, x_ref, sem = refs[6 + 2 * n_side:]

    @pl.when(pl.program_id(1) == 0)
    def _():
        _wait_token_tile(x_hbm, x_ref, sem)
        x = x_ref[...]
        h_ref[...] = _rms(x, g_ref[...]).astype(BF16)
        o_ref[...] = x

    _prefetch_next_token_tile(x_hbm, x_ref, sem)

    for src, dst in zip(side_in, side_out):
        dst[...] = src[...].astype(BF16)

    wg, wu, wd = (wg_ref[...].astype(BF16), wu_ref[...].astype(BF16), wd_ref[...].astype(BF16))
    subs = [slice(r, r + FFN_SUBTILE) for r in range(0, x_ref.shape[0], FFN_SUBTILE)]
    acts = []
    for rows in subs:
        h = h_ref[rows, :]
        gate = jnp.dot(h, wg, preferred_element_type=F32)
        up = jnp.dot(h, wu, preferred_element_type=F32)
        acts.append(((gate * _sigmoid(gate)) * (0.5 * up)).astype(BF16))
    for rows, a in zip(subs, acts):
        o_ref[rows, :] += jnp.dot(a, wd, preferred_element_type=F32)


def _ffn(x, norm_g, wg, wu, wd, layer, which, *, side=(), tm=1024, tf=512):
    ni, nj = TOKENS // tm, D_FF // tf
    side_specs, side_shapes = [], []
    for arr, n_blocks in side:
        assert n_blocks <= ni * nj and arr.shape[0] % n_blocks == 0
        block = (arr.shape[0] // n_blocks, arr.shape[1])
        assert block[0] % (2 * SUBLANES) == 0
        index = functools.partial(
            lambda i, j, last: (jnp.minimum(i * nj + j, last), 0), last=n_blocks - 1)
        side_specs.append(pl.BlockSpec(block, index))
        side_shapes.append(jax.ShapeDtypeStruct(arr.shape, BF16))
    out = pl.pallas_call(
        functools.partial(_ffn_body, n_side=len(side)),
        grid=(ni, nj),
        in_specs=[
            pl.BlockSpec(memory_space=pl.ANY),
            _gain_spec(layer, 2 * which),
            pl.BlockSpec((None, None, D_MODEL, tf), lambda i, j: (layer, which, 0, j)),
            pl.BlockSpec((None, None, D_MODEL, tf), lambda i, j: (layer, which, 0, j)),
            pl.BlockSpec((None, None, tf, D_MODEL), lambda i, j: (layer, which, j, 0)),
        ] + side_specs,
        out_specs=[pl.BlockSpec((tm, D_MODEL), lambda i, j: (i, 0))] + side_specs,
        out_shape=[jax.ShapeDtypeStruct((TOKENS, D_MODEL), F32)] + side_shapes,
        scratch_shapes=[pltpu.VMEM((tm, D_MODEL), BF16), pltpu.VMEM((tm, D_MODEL), F32),
                        pltpu.SemaphoreType.DMA(())],
        compiler_params=_params(("arbitrary", "arbitrary")),
        name="ffn",
    )(x, norm_g, wg, wu, wd, *[arr for arr, _ in side])
    return out[0], out[1:]


GMLP_STEP_GROUPS = 2
GMLP_CAST_BLOCKS = 128
GMLP_SUBTILE = 256
GMLP_STEP_W = GMLP_STEP_GROUPS * GMLP_GROUP_DIM


def _gmlp_body(x_hbm, g_ref, wu_ref, wv_ref, lng_ref, lnb_ref, ws_ref, bs_ref, wo_ref,
               o_ref, h_ref, gated_ref, x_ref, sem):
    @pl.when(pl.program_id(1) == 0)
    def _():
        _wait_token_tile(x_hbm, x_ref, sem)
        x = x_ref[...]
        h_ref[...] = _rms(x, g_ref[...]).astype(BF16)
        o_ref[...] = x

    _prefetch_next_token_tile(x_hbm, x_ref, sem)

    tm = x_ref.shape[0]
    subs = [slice(r, r + GMLP_SUBTILE) for r in range(0, tm, GMLP_SUBTILE)]
    z = []
    for rows in subs:
        h = h_ref[rows, :]
        zv = _gelu(jnp.dot(h, wv_ref[...], preferred_element_type=F32))
        zu = _gelu(jnp.dot(h, wu_ref[...], preferred_element_type=F32))
        z.append((zu, zv))
    row = lax.broadcasted_iota(jnp.int32, (CHUNK, CHUNK), 0)
    col = lax.broadcasted_iota(jnp.int32, (CHUNK, CHUNK), 1)
    ws = [jnp.where(row >= col, ws_ref[gi], 0.0).astype(BF16)
          for gi in range(GMLP_STEP_GROUPS)]
    for rows, (zu, zv) in zip(subs, z):
        for gi in range(GMLP_STEP_GROUPS):
            lo, hi = gi * GMLP_GROUP_DIM, (gi + 1) * GMLP_GROUP_DIM
            v = zv[:, lo:hi]
            d = v - jnp.mean(v, axis=-1, keepdims=True)
            var = jnp.mean(d * d, axis=-1, keepdims=True)
            vn = ((d * lax.rsqrt(var + EPS)) * lng_ref[:, lo:hi] + lnb_ref[:, lo:hi]).astype(BF16)
            bias = bs_ref[gi]
            for c in range(0, GMLP_SUBTILE, CHUNK):
                sv = jnp.dot(ws[gi], vn[c:c + CHUNK], preferred_element_type=F32) + bias
                out_rows = slice(rows.start + c, rows.start + c + CHUNK)
                gated_ref[out_rows, lo:hi] = (zu[c:c + CHUNK, lo:hi] * sv).astype(BF16)
    for rows in subs:
        o_ref[rows, :] += jnp.dot(gated_ref[rows, :], wo_ref[...], preferred_element_type=F32)


def _gmlp(x, norm_g, w_in, ln_g, ln_b, w_s, b_s, w_out, layer, *, tm=1024):
    nj = GMLP_GROUPS // GMLP_STEP_GROUPS
    return pl.pallas_call(
        _gmlp_body,
        grid=(TOKENS // tm, nj),
        in_specs=[
            pl.BlockSpec(memory_space=pl.ANY),
            _gain_spec(layer, 1),
            pl.BlockSpec((D_MODEL, GMLP_STEP_W), lambda i, j: (0, j)),
            pl.BlockSpec((D_MODEL, GMLP_STEP_W), lambda i, j: (0, j + nj)),
            pl.BlockSpec((1, GMLP_STEP_W), lambda i, j: (0, j)),
            pl.BlockSpec((1, GMLP_STEP_W), lambda i, j: (0, j)),
            pl.BlockSpec((GMLP_STEP_GROUPS, CHUNK, CHUNK), lambda i, j: (j, 0, 0)),
            pl.BlockSpec((GMLP_STEP_GROUPS, CHUNK, 1), lambda i, j: (j, 0, 0)),
            pl.BlockSpec((GMLP_STEP_W, D_MODEL), lambda i, j: (j, 0)),
        ],
        out_specs=pl.BlockSpec((tm, D_MODEL), lambda i, j: (i, 0)),
        out_shape=jax.ShapeDtypeStruct((TOKENS, D_MODEL), F32),
        scratch_shapes=[pltpu.VMEM((tm, D_MODEL), BF16), pltpu.VMEM((tm, GMLP_STEP_W), BF16),
                        pltpu.VMEM((tm, D_MODEL), F32), pltpu.SemaphoreType.DMA(())],
        compiler_params=_params(("arbitrary", "arbitrary")),
        name="gmlp",
    )(x, norm_g, w_in, w_in, ln_g, ln_b, w_s, b_s, w_out)


def _s5_disc_body(lr_ref, li_ref, ldt_ref, br_ref, bi_ref, ar_ref, ai_ref, bbr_ref, bbi_ref):
    lr, li = lr_ref[...], li_ref[...]
    dt = jnp.exp(ldt_ref[...])
    mag = jnp.exp(lr * dt)
    ang = li * dt
    a_r = mag * jnp.cos(ang)
    a_i = mag * jnp.sin(ang)
    den = lr * lr + li * li
    nr = a_r - 1.0
    z_r = (nr * lr + a_i * li) / den
    z_i = (a_i * lr - nr * li) / den
    ar_ref[...] = a_r
    ai_ref[...] = a_i
    bbr_ref[...] = z_r * br_ref[...] - z_i * bi_ref[...]
    bbi_ref[...] = z_r * bi_ref[...] + z_i * br_ref[...]


def _s5_disc(lam_re, lam_im, log_dt, b_re, b_im):
    n = SSM_GROUPS * SSM_STATE
    row = jax.ShapeDtypeStruct((1, n), F32)
    mat = jax.ShapeDtypeStruct((SSM_GROUP_DIM, n), F32)
    to_mat = lambda b: jnp.transpose(b, (2, 0, 1)).reshape(SSM_GROUP_DIM, n)
    ldt = jnp.broadcast_to(log_dt[:, None], (SSM_GROUPS, SSM_STATE)).reshape(1, n)
    return pl.pallas_call(_s5_disc_body, out_shape=(row, row, mat, mat), name="s5_disc")(
        lam_re.reshape(1, n), lam_im.reshape(1, n), ldt, to_mat(b_re), to_mat(b_im))


def _resident(shape):
    zeros = (0,) * len(shape)
    return pl.BlockSpec(shape, lambda i: zeros, pipeline_mode=pl.Buffered(1))


def _s5_in_body(x_ref, g_ref, w_ref, wo_ref, o_ref, wob_ref, wb_ref):
    @pl.when(pl.program_id(0) == 0)
    def _():
        wb_ref[...] = w_ref[...].astype(BF16)

    wob_ref[...] = wo_ref[...].astype(BF16)
    h = _rms(x_ref[...], g_ref[...]).astype(BF16)
    o_ref[...] = jnp.dot(h, wb_ref[...], preferred_element_type=F32)


def _s5_in(x, norm_g, w, w_out, layer, *, tm=512):
    steps = TOKENS // tm
    wo_block = pl.BlockSpec((D_MODEL // steps, 2 * D_MODEL), lambda i: (i, 0))
    return pl.pallas_call(
        _s5_in_body,
        grid=(steps,),
        in_specs=[
            pl.BlockSpec((tm, D_MODEL), lambda i: (i, 0)),
            pl.BlockSpec((None, None, 1, D_MODEL), lambda i: (layer, 1, 0, 0)),
            _resident((D_MODEL, D_MODEL)),
            wo_block,
        ],
        out_specs=[pl.BlockSpec((tm, D_MODEL), lambda i: (i, 0)), wo_block],
        out_shape=[jax.ShapeDtypeStruct((TOKENS, D_MODEL), F32),
                   jax.ShapeDtypeStruct((D_MODEL, 2 * D_MODEL), BF16)],
        scratch_shapes=[pltpu.VMEM((D_MODEL, D_MODEL), BF16)],
        compiler_params=_params(("arbitrary",)),
        name="s5_in",
    )(x, norm_g, w, w_out)


def _s5_scan_body(u_ref, bmat_ref, cmat_ref, ar_ref, ai_ref, d_ref, o_ref,
                  lhs_ref, bu0_ref, bu1_ref, hs0_ref, hs1_ref, yp_ref, carry_ref):
    tile = SUBLANES
    lane = lax.broadcasted_iota(jnp.int32, (tile, LANES), 1)
    low = lane < HALF_CH
    bu_refs = (bu0_ref, bu1_ref)
    hs_refs = (hs0_ref, hs1_ref)

    @pl.when(pl.program_id(1) == 0)
    def _():
        carry_ref[...] = jnp.zeros_like(carry_ref)

    re, im = slice(0, HALF_STATES), slice(HALF_STATES, 2 * HALF_STATES)
    pair_slabs = lambda pair: range(pair * SCAN_PAIR, (pair + 1) * SCAN_PAIR)

    def to_states(pair):
        for k, s in enumerate(pair_slabs(pair)):
            for b in range(BATCH):
                for t0 in range(0, SCAN_T, tile):
                    v = u_ref[b, t0:t0 + tile, s * LANES:(s + 1) * LANES]
                    lhs_ref[s, pl.ds(t0 * tile + b, tile, stride=tile), :] = jnp.where(low, v, 0.0)
                    lhs_ref[s, pl.ds(t0 * tile + BATCH + b, tile, stride=tile), :] = jnp.where(low, 0.0, v)
            bu_refs[pair][k] = jnp.dot(lhs_ref[s].astype(BF16), bmat_ref[s],
                                       preferred_element_type=F32)

    def scan(pair):
        bu_ref, hs_ref = bu_refs[pair], hs_refs[pair]
        state = [(ar_ref[s], ai_ref[s], carry_ref[s, :, re], carry_ref[s, :, im])
                 for s in pair_slabs(pair)]
        for t in range(0, SCAN_T, 2):
            for k in range(SCAN_PAIR):
                a_r, a_i, hr, hi = state[k]
                steps = []
                for rows in (slice(t * tile, (t + 1) * tile), slice((t + 1) * tile, (t + 2) * tile)):
                    hr, hi = ((a_r * hr - a_i * hi) + bu_ref[k, rows, re],
                              (a_r * hi + a_i * hr) + bu_ref[k, rows, im])
                    steps.append((hr, hi))
                both = slice(t * tile, (t + 2) * tile)
                hs_ref[k, both, re] = jnp.concatenate([steps[0][0], steps[1][0]], axis=0).astype(BF16)
                hs_ref[k, both, im] = jnp.concatenate([steps[0][1], steps[1][1]], axis=0).astype(BF16)
                state[k] = (a_r, a_i, hr, hi)
        for k, s in enumerate(pair_slabs(pair)):
            carry_ref[s, :, re] = state[k][2]
            carry_ref[s, :, im] = state[k][3]

    def from_states(pair):
        for k, s in enumerate(pair_slabs(pair)):
            yp_ref[s] = jnp.dot(hs_refs[pair][k], cmat_ref[s], preferred_element_type=F32)
            sl = slice(s * LANES, (s + 1) * LANES)
            dsk = d_ref[:, sl]
            for b in range(BATCH):
                for t0 in range(0, SCAN_T, 2 * tile):
                    ys = []
                    for t1 in (t0, t0 + tile):
                        y0 = yp_ref[s, pl.ds(t1 * tile + b, tile, stride=tile), :]
                        y1 = yp_ref[s, pl.ds(t1 * tile + BATCH + b, tile, stride=tile), :]
                        ys.append(jnp.where(low, y0, y1) + dsk * u_ref[b, t1:t1 + tile, sl])
                    o_ref[b, t0:t0 + 2 * tile, sl] = _gelu(jnp.concatenate(ys, axis=0)).astype(BF16)

    pairs = range(SCAN_SLABS // SCAN_PAIR)
    for pair in pairs:
        to_states(pair)
    for pair in pairs:
        scan(pair)
        from_states(pair)


def _s5_scan(u, bmat, cmat, a_r, a_i, d):
    width = SCAN_SLABS * LANES
    rows = SCAN_T * SUBLANES
    bu_buf = pltpu.VMEM((SCAN_PAIR, rows, 2 * HALF_STATES), F32)
    hs_buf = pltpu.VMEM((SCAN_PAIR, rows, 2 * HALF_STATES), BF16)
    return pl.pallas_call(
        _s5_scan_body,
        grid=(N_SLABS // SCAN_SLABS, SEQ // SCAN_T),
        in_specs=[
            pl.BlockSpec((BATCH, SCAN_T, width), lambda s, c: (0, c, s)),
            pl.BlockSpec((SCAN_SLABS, LANES, 2 * HALF_STATES), lambda s, c: (s, 0, 0)),
            pl.BlockSpec((SCAN_SLABS, 2 * HALF_STATES, LANES), lambda s, c: (s, 0, 0)),
            pl.BlockSpec((SCAN_SLABS, SUBLANES, HALF_STATES), lambda s, c: (s, 0, 0)),
            pl.BlockSpec((SCAN_SLABS, SUBLANES, HALF_STATES), lambda s, c: (s, 0, 0)),
            pl.BlockSpec((1, width), lambda s, c: (0, s)),
        ],
        out_specs=pl.BlockSpec((BATCH, SCAN_T, width), lambda s, c: (0, c, s)),
        out_shape=jax.ShapeDtypeStruct((BATCH, SEQ, D_MODEL), BF16),
        scratch_shapes=[
            pltpu.VMEM((SCAN_SLABS, rows, LANES), F32),
            bu_buf, bu_buf, hs_buf, hs_buf,
            pltpu.VMEM((SCAN_SLABS, rows, LANES), F32),
            pltpu.VMEM((SCAN_SLABS, SUBLANES, 2 * HALF_STATES), F32),
        ],
        compiler_params=_params(("parallel", "arbitrary")),
        name="s5_scan",
    )(u, bmat, cmat, a_r, a_i, d)


def _s5_scan_operands(a_r, a_i, bb_r, bb_i, c_re, c_im):
    halves = SLAB_GROUPS // 2
    eye = jnp.eye(halves, dtype=F32)

    def b_blocks(bb):
        bb = bb.reshape(SSM_GROUP_DIM, N_SLABS, 2, halves, SSM_STATE)
        blk = jnp.einsum('hsqgp,gk->sqghkp', bb, eye)
        return blk.reshape(N_SLABS, LANES, HALF_STATES)

    def c_blocks(c):
        c = c.reshape(N_SLABS, 2, halves, SSM_GROUP_DIM, SSM_STATE)
        blk = jnp.einsum('sqghp,gk->skpqgh', c, eye)
        return blk.reshape(N_SLABS, HALF_STATES, LANES)

    bmat = jnp.concatenate([b_blocks(bb_r), b_blocks(bb_i)], axis=-1).astype(BF16)
    cmat = jnp.concatenate([c_blocks(c_re), -c_blocks(c_im)], axis=1).astype(BF16)

    def a_rows(a):
        a = a.reshape(N_SLABS, 2, 1, HALF_STATES)
        return jnp.broadcast_to(a, (N_SLABS, 2, BATCH, HALF_STATES)).reshape(
            N_SLABS, SUBLANES, HALF_STATES)

    return bmat, cmat, a_rows(a_r), a_rows(a_i)


def _s5_out_body(x_ref, a_ref, w_ref, o_ref, *, tn):
    a = a_ref[...]
    for c in range(0, D_MODEL, tn):
        val = jnp.dot(a, w_ref[:, c:c + tn], preferred_element_type=F32)
        gate = jnp.dot(a, w_ref[:, D_MODEL + c:D_MODEL + c + tn], preferred_element_type=F32)
        o_ref[:, c:c + tn] = x_ref[:, c:c + tn] + val * _sigmoid(gate)


def _s5_out(x, act, w, *, tm=512, tn=512):
    return pl.pallas_call(
        functools.partial(_s5_out_body, tn=tn),
        grid=(TOKENS // tm,),
        in_specs=[
            pl.BlockSpec((tm, D_MODEL), lambda i: (i, 0)),
            pl.BlockSpec((tm, D_MODEL), lambda i: (i, 0)),
            _resident((D_MODEL, 2 * D_MODEL)),
        ],
        out_specs=pl.BlockSpec((tm, D_MODEL), lambda i: (i, 0)),
        out_shape=jax.ShapeDtypeStruct((TOKENS, D_MODEL), F32),
        compiler_params=_params(("parallel",)),
        name="s5_out",
    )(x, act, w)


def _s5(x, norm_g, layer, w_in, lam_re, lam_im, log_dt, b_re, b_im, c_re, c_im, d_skip, w_out):
    a_r, a_i, bb_r, bb_i = _s5_disc(lam_re, lam_im, log_dt, b_re, b_im)
    bmat, cmat, ar_rows, ai_rows = _s5_scan_operands(a_r, a_i, bb_r, bb_i, c_re, c_im)
    u, w_out_bf = _s5_in(x, norm_g, w_in, w_out, layer)
    act = _s5_scan(u.reshape(BATCH, SEQ, D_MODEL), bmat, cmat, ar_rows, ai_rows,
                   d_skip.reshape(1, D_MODEL))
    return _s5_out(x, act.reshape(TOKENS, D_MODEL), w_out_bf)


def _ple_body(x_ref, g_ref, wg_ref, p_ref, wp_ref, fg_ref, o_ref, wgb_ref, wpb_ref, *, final):
    @pl.when(pl.program_id(0) == 0)
    def _():
        wgb_ref[...] = wg_ref[...].astype(BF16)
        wpb_ref[...] = wp_ref[...].astype(BF16)

    x = x_ref[...]
    h = _rms(x, g_ref[...]).astype(BF16)
    gate = _sigmoid(jnp.dot(h, wgb_ref[...], preferred_element_type=F32))
    proj = jnp.dot(p_ref[...].astype(BF16), wpb_ref[...], preferred_element_type=F32)
    y = x + gate * proj
    o_ref[...] = _rms(y, fg_ref[...]) if final else y


def _ple(x, norm_g, wg, p, wp, fg, layer, *, final, tm=512):
    once = pl.Buffered(1)
    return pl.pallas_call(
        functools.partial(_ple_body, final=final),
        grid=(TOKENS // tm,),
        in_specs=[
            pl.BlockSpec((tm, D_MODEL), lambda i: (i, 0)),
            pl.BlockSpec((None, None, 1, D_MODEL), lambda i: (layer, 3, 0, 0)),
            pl.BlockSpec((None, D_MODEL, D_MODEL), lambda i: (layer, 0, 0), pipeline_mode=once),
            pl.BlockSpec((None, tm, PLE_DIM), lambda i: (layer, i, 0)),
            pl.BlockSpec((None, PLE_DIM, D_MODEL), lambda i: (layer, 0, 0), pipeline_mode=once),
            pl.BlockSpec((1, D_MODEL), lambda i: (0, 0)),
        ],
        out_specs=pl.BlockSpec((tm, D_MODEL), lambda i: (i, 0)),
        out_shape=jax.ShapeDtypeStruct((TOKENS, D_MODEL), F32),
        scratch_shapes=[pltpu.VMEM((D_MODEL, D_MODEL), BF16), pltpu.VMEM((PLE_DIM, D_MODEL), BF16)],
        compiler_params=_params(("arbitrary",)),
        name="ple",
    )(x, norm_g, wg, p, wp, fg)


def kernel(x, p, norm_g, final_norm_g, ffn_w_gate, ffn_w_up, ffn_w_down, gmlp_w_in, gmlp_ln_g, gmlp_ln_b, gmlp_w_s, gmlp_b_s, gmlp_w_out, s5_w_in, s5_lam_re, s5_lam_im, s5_log_dt, s5_b_re, s5_b_im, s5_c_re, s5_c_im, s5_d, s5_w_out, ple_w_gate, ple_w_proj):
    x = x.reshape(TOKENS, D_MODEL)
    p = p.reshape(DEPTH, TOKENS, PLE_DIM)
    norm_g = norm_g.reshape(DEPTH, 4, 1, D_MODEL)
    fg = final_norm_g.reshape(1, D_MODEL)
    for i in range(DEPTH):
        j = i // 2
        if i % 2 == 0:
            side = ((gmlp_w_in[j], GMLP_CAST_BLOCKS), (gmlp_w_out[j], GMLP_CAST_BLOCKS))
            x, (w_in, w_out) = _ffn(x, norm_g, ffn_w_gate, ffn_w_up, ffn_w_down, i, 0,
                                    side=side, tf=256)
            x = _gmlp(x, norm_g, w_in, gmlp_ln_g[j].reshape(1, GMLP_HALF),
                      gmlp_ln_b[j].reshape(1, GMLP_HALF), gmlp_w_s[j],
                      gmlp_b_s[j].reshape(GMLP_GROUPS, CHUNK, 1), w_out, i)
        else:
            x, _ = _ffn(x, norm_g, ffn_w_gate, ffn_w_up, ffn_w_down, i, 0)
            x = _s5(x, norm_g, i, s5_w_in[j], s5_lam_re[j], s5_lam_im[j], s5_log_dt[j],
                    s5_b_re[j], s5_b_im[j], s5_c_re[j], s5_c_im[j], s5_d[j], s5_w_out[j])
        x, _ = _ffn(x, norm_g, ffn_w_gate, ffn_w_up, ffn_w_down, i, 1)
        x = _ple(x, norm_g, ple_w_gate, p, ple_w_proj, fg, i, final=(i == DEPTH - 1))
    return x.reshape(BATCH, SEQ, D_MODEL)
```

```python
import functools
import math

import jax
import jax.numpy as jnp
from jax import lax
from jax.experimental import pallas as pl
from jax.experimental.pallas import tpu as pltpu

F32 = jnp.float32
BF16 = jnp.bfloat16

D_MODEL = 2048
BATCH = 4
SEQ = 2048
TOKENS = BATCH * SEQ
DEPTH = 2
D_FF = 5632
PLE_DIM = 256
CHUNK = 128
GMLP_HALF = 3 * D_MODEL
GMLP_GROUPS = 16
GMLP_GROUP_DIM = GMLP_HALF // GMLP_GROUPS
SSM_GROUP_DIM = 16
SSM_GROUPS = D_MODEL // SSM_GROUP_DIM
SSM_STATE = 64
EPS = 1e-6

LANES = 128
SUBLANES = 8
VMEM_LIMIT = 60 * 1024 * 1024

SLAB_GROUPS = LANES // SSM_GROUP_DIM
N_SLABS = D_MODEL // LANES
SLAB_STATES = SLAB_GROUPS * SSM_STATE
HALF_STATES = SLAB_STATES // 2
HALF_CH = LANES // 2
SCAN_PAIR = 2
SCAN_SLABS = 2 * SCAN_PAIR
SCAN_T = 256


def _rms(x, g):
    ms = jnp.mean(x * x, axis=-1, keepdims=True)
    return x * lax.rsqrt(ms + EPS) * g


def _gelu(x):
    c = math.sqrt(2.0 / math.pi)
    return x * (0.5 * (1.0 + jnp.tanh(c * (x + 0.044715 * (x * x * x)))))


def _sigmoid(x):
    return 1.0 / (1.0 + jnp.exp(-x))


def _params(sem):
    return pltpu.CompilerParams(dimension_semantics=sem, vmem_limit_bytes=VMEM_LIMIT)


def _gain_spec(layer, slot):
    return pl.BlockSpec((None, None, 1, D_MODEL), lambda i, j: (layer, slot, 0, 0))


def _token_tile_copy(x_hbm, x_ref, sem, tile):
    tm = x_ref.shape[0]
    rows = pl.ds(pl.multiple_of(tile * tm, tm), tm)
    return pltpu.make_async_copy(x_hbm.at[rows, :], x_ref, sem)


def _wait_token_tile(x_hbm, x_ref, sem):
    i = pl.program_id(0)

    @pl.when(i == 0)
    def _():
        _token_tile_copy(x_hbm, x_ref, sem, i).start()

    _token_tile_copy(x_hbm, x_ref, sem, i).wait()


def _prefetch_next_token_tile(x_hbm, x_ref, sem):
    i, j = pl.program_id(0), pl.program_id(1)

    @pl.when(jnp.logical_and(j == 1, i + 1 < pl.num_programs(0)))
    def _():
        _token_tile_copy(x_hbm, x_ref, sem, i + 1).start()


def _ffn_body(*refs, n_side, sub_rows):
    x_hbm, g_ref, wg_ref, wu_ref, wd_ref = refs[:5]
    side_in = refs[5:5 + n_side]
    o_ref = refs[5 + n_side]
    side_out = refs[6 + n_side:6 + 2 * n_side]
    h_ref, x_ref, sem = refs[6 + 2 * n_side:]

    i, j = pl.program_id(0), pl.program_id(1)
    last = pl.num_programs(1) - 1

    def step(addend_ref):
        for src, dst in zip(side_in, side_out):
            dst[...] = src[...].astype(BF16)
        wg, wu, wd = (wg_ref[...].astype(BF16), wu_ref[...].astype(BF16),
                      wd_ref[...].astype(BF16))
        subs = [slice(r, r + sub_rows) for r in range(0, x_ref.shape[0], sub_rows)]
        acts = []
        for rows in subs:
            h = h_ref[rows, :]
            gate = jnp.dot(h, wg, preferred_element_type=F32)
            up = jnp.dot(h, wu, preferred_element_type=F32)
            acts.append(((gate * _sigmoid(gate)) * (0.5 * up)).astype(BF16))
        for rows, a in zip(subs, acts):
            o_ref[rows, :] = addend_ref[rows, :] + jnp.dot(a, wd, preferred_element_type=F32)

    def norm_tile():
        h_ref[...] = _rms(x_ref[...], g_ref[...]).astype(BF16)

    @pl.when(j == 0)
    def _():
        @pl.when(i == 0)
        def _():
            _token_tile_copy(x_hbm, x_ref, sem, i).start()
            _token_tile_copy(x_hbm, x_ref, sem, i).wait()
            norm_tile()

        step(x_ref)

    _prefetch_next_token_tile(x_hbm, x_ref, sem)

    @pl.when(jnp.logical_and(j > 0, j < last))
    def _():
        step(o_ref)

    @pl.when(j == last)
    def _():
        @pl.when(i + 1 < pl.num_programs(0))
        def _():
            _token_tile_copy(x_hbm, x_ref, sem, i + 1).wait()

        step(o_ref)
        norm_tile()


def _ffn(x, norm_g, wg, wu, wd, layer, which, *, side=(), tm=1024, tf=512, sub_rows=None):
    ni, nj = TOKENS // tm, D_FF // tf
    side_specs, side_shapes = [], []
    for arr, n_blocks in side:
        assert n_blocks <= ni * nj and arr.shape[0] % n_blocks == 0
        block = (arr.shape[0] // n_blocks, arr.shape[1])
        assert block[0] % (2 * SUBLANES) == 0
        index = functools.partial(
            lambda i, j, last: (jnp.minimum(i * nj + j, last), 0), last=n_blocks - 1)
        side_specs.append(pl.BlockSpec(block, index))
        side_shapes.append(jax.ShapeDtypeStruct(arr.shape, BF16))
    out = pl.pallas_call(
        functools.partial(_ffn_body, n_side=len(side), sub_rows=sub_rows or tm),
        grid=(ni, nj),
        in_specs=[
            pl.BlockSpec(memory_space=pl.ANY),
            _gain_spec(layer, 2 * which),
            pl.BlockSpec((None, None, D_MODEL, tf), lambda i, j: (layer, which, 0, j)),
            pl.BlockSpec((None, None, D_MODEL, tf), lambda i, j: (layer, which, 0, j)),
            pl.BlockSpec((None, None, tf, D_MODEL), lambda i, j: (layer, which, j, 0)),
        ] + side_specs,
        out_specs=[pl.BlockSpec((tm, D_MODEL), lambda i, j: (i, 0))] + side_specs,
        out_shape=[jax.ShapeDtypeStruct((TOKENS, D_MODEL), F32)] + side_shapes,
        scratch_shapes=[pltpu.VMEM((tm, D_MODEL), BF16), pltpu.VMEM((tm, D_MODEL), F32),
                        pltpu.SemaphoreType.DMA(())],
        compiler_params=_params(("arbitrary", "arbitrary")),
        name="ffn",
    )(x, norm_g, wg, wu, wd, *[arr for arr, _ in side])
    return out[0], out[1:]


GMLP_STEP_GROUPS = 2
GMLP_CAST_BLOCKS = 128
GMLP_SUBTILE = 256
GMLP_STEP_W = GMLP_STEP_GROUPS * GMLP_GROUP_DIM


def _gmlp_body(x_hbm, g_ref, wu_ref, wv_ref, lng_ref, lnb_ref, ws_ref, bs_ref, wo_ref,
               o_ref, h_ref, gated_ref, x_ref, sem):
    @pl.when(pl.program_id(1) == 0)
    def _():
        _wait_token_tile(x_hbm, x_ref, sem)
        x = x_ref[...]
        h_ref[...] = _rms(x, g_ref[...]).astype(BF16)
        o_ref[...] = x

    _prefetch_next_token_tile(x_hbm, x_ref, sem)

    tm = x_ref.shape[0]
    subs = [slice(r, r + GMLP_SUBTILE) for r in range(0, tm, GMLP_SUBTILE)]
    z = []
    for rows in subs:
        h = h_ref[rows, :]
        zv = _gelu(jnp.dot(h, wv_ref[...], preferred_element_type=F32))
        zu = _gelu(jnp.dot(h, wu_ref[...], preferred_element_type=F32))
        z.append((zu, zv))
    row = lax.broadcasted_iota(jnp.int32, (CHUNK, CHUNK), 0)
    col = lax.broadcasted_iota(jnp.int32, (CHUNK, CHUNK), 1)
    ws = [jnp.where(row >= col, ws_ref[gi], 0.0).astype(BF16)
          for gi in range(GMLP_STEP_GROUPS)]
    for rows, (zu, zv) in zip(subs, z):
        for gi in range(GMLP_STEP_GROUPS):
            lo, hi = gi * GMLP_GROUP_DIM, (gi + 1) * GMLP_GROUP_DIM
            v = zv[:, lo:hi]
            d = v - jnp.mean(v, axis=-1, keepdims=True)
            var = jnp.mean(d * d, axis=-1, keepdims=True)
            vn = ((d * lax.rsqrt(var + EPS)) * lng_ref[:, lo:hi] + lnb_ref[:, lo:hi]).astype(BF16)
            bias = bs_ref[gi]
            for c in range(0, GMLP_SUBTILE, CHUNK):
                sv = jnp.dot(ws[gi], vn[c:c + CHUNK], preferred_element_type=F32) + bias
                out_rows = slice(rows.start + c, rows.start + c + CHUNK)
                gated_ref[out_rows, lo:hi] = (zu[c:c + CHUNK, lo:hi] * sv).astype(BF16)
    for rows in subs:
        o_ref[rows, :] += jnp.dot(gated_ref[rows, :], wo_ref[...], preferred_element_type=F32)


def _gmlp(x, norm_g, w_in, ln_g, ln_b, w_s, b_s, w_out, layer, *, tm=1024):
    nj = GMLP_GROUPS // GMLP_STEP_GROUPS
    return pl.pallas_call(
        _gmlp_body,
        grid=(TOKENS // tm, nj),
        in_specs=[
            pl.BlockSpec(memory_space=pl.ANY),
            _gain_spec(layer, 1),
            pl.BlockSpec((D_MODEL, GMLP_STEP_W), lambda i, j: (0, j)),
            pl.BlockSpec((D_MODEL, GMLP_STEP_W), lambda i, j: (0, j + nj)),
            pl.BlockSpec((1, GMLP_STEP_W), lambda i, j: (0, j)),
            pl.BlockSpec((1, GMLP_STEP_W), lambda i, j: (0, j)),
            pl.BlockSpec((GMLP_STEP_GROUPS, CHUNK, CHUNK), lambda i, j: (j, 0, 0)),
            pl.BlockSpec((GMLP_STEP_GROUPS, CHUNK, 1), lambda i, j: (j, 0, 0)),
            pl.BlockSpec((GMLP_STEP_W, D_MODEL), lambda i, j: (j, 0)),
        ],
        out_specs=pl.BlockSpec((tm, D_MODEL), lambda i, j: (i, 0)),
        out_shape=jax.ShapeDtypeStruct((TOKENS, D_MODEL), F32),
        scratch_shapes=[pltpu.VMEM((tm, D_MODEL), BF16), pltpu.VMEM((tm, GMLP_STEP_W), BF16),
                        pltpu.VMEM((tm, D_MODEL), F32), pltpu.SemaphoreType.DMA(())],
        compiler_params=_params(("arbitrary", "arbitrary")),
        name="gmlp",
    )(x, norm_g, w_in, w_in, ln_g, ln_b, w_s, b_s, w_out)


def _s5_disc_body(lr_ref, li_ref, ldt_ref, br_ref, bi_ref, ar_ref, ai_ref, bbr_ref, bbi_ref):
    lr, li = lr_ref[...], li_ref[...]
    dt = jnp.exp(ldt_ref[...])
    mag = jnp.exp(lr * dt)
    ang = li * dt
    a_r = mag * jnp.cos(ang)
    a_i = mag * jnp.sin(ang)
    den = lr * lr + li * li
    nr = a_r - 1.0
    z_r = (nr * lr + a_i * li) / den
    z_i = (a_i * lr - nr * li) / den
    ar_ref[...] = a_r
    ai_ref[...] = a_i
    bbr_ref[...] = z_r * br_ref[...] - z_i * bi_ref[...]
    bbi_ref[...] = z_r * bi_ref[...] + z_i * br_ref[...]


def _s5_disc(lam_re, lam_im, log_dt, b_re, b_im):
    n = SSM_GROUPS * SSM_STATE
    row = jax.ShapeDtypeStruct((1, n), F32)
    mat = jax.ShapeDtypeStruct((SSM_GROUP_DIM, n), F32)
    to_mat = lambda b: jnp.transpose(b, (2, 0, 1)).reshape(SSM_GROUP_DIM, n)
    ldt = jnp.broadcast_to(log_dt[:, None], (SSM_GROUPS, SSM_STATE)).reshape(1, n)
    return pl.pallas_call(_s5_disc_body, out_shape=(row, row, mat, mat), name="s5_disc")(
        lam_re.reshape(1, n), lam_im.reshape(1, n), ldt, to_mat(b_re), to_mat(b_im))


def _resident(shape):
    zeros = (0,) * len(shape)
    return pl.BlockSpec(shape, lambda i: zeros, pipeline_mode=pl.Buffered(1))


def _s5_in_body(x_ref, g_ref, w_ref, wo_ref, o_ref, wob_ref, wb_ref):
    @pl.when(pl.program_id(0) == 0)
    def _():
        wb_ref[...] = w_ref[...].astype(BF16)

    wob_ref[...] = wo_ref[...].astype(BF16)
    h = _rms(x_ref[...], g_ref[...]).astype(BF16)
    o_ref[...] = jnp.dot(h, wb_ref[...], preferred_element_type=F32)


def _s5_in(x, norm_g, w, w_out, layer, *, tm=512):
    steps = TOKENS // tm
    wo_block = pl.BlockSpec((D_MODEL // steps, 2 * D_MODEL), lambda i: (i, 0))
    return pl.pallas_call(
        _s5_in_body,
        grid=(steps,),
        in_specs=[
            pl.BlockSpec((tm, D_MODEL), lambda i: (i, 0)),
            pl.BlockSpec((None, None, 1, D_MODEL), lambda i: (layer, 1, 0, 0)),
            _resident((D_MODEL, D_MODEL)),
            wo_block,
        ],
        out_specs=[pl.BlockSpec((tm, D_MODEL), lambda i: (i, 0)), wo_block],
        out_shape=[jax.ShapeDtypeStruct((TOKENS, D_MODEL), F32),
                   jax.ShapeDtypeStruct((D_MODEL, 2 * D_MODEL), BF16)],
        scratch_shapes=[pltpu.VMEM((D_MODEL, D_MODEL), BF16)],
        compiler_params=_params(("arbitrary",)),
        name="s5_in",
    )(x, norm_g, w, w_out)


def _s5_scan_body(u_ref, bmat_ref, cmat_ref, ar_ref, ai_ref, d_ref, o_ref,
                  lhs_ref, bu0_ref, bu1_ref, hs0_ref, hs1_ref, yp_ref, carry_ref):
    tile = SUBLANES
    lane = lax.broadcasted_iota(jnp.int32, (tile, LANES), 1)
    low = lane < HALF_CH
    bu_refs = (bu0_ref, bu1_ref)
    hs_refs = (hs0_ref, hs1_ref)

    @pl.when(pl.program_id(1) == 0)
    def _():
        carry_ref[...] = jnp.zeros_like(carry_ref)

    re, im = slice(0, HALF_STATES), slice(HALF_STATES, 2 * HALF_STATES)
    pair_slabs = lambda pair: range(pair * SCAN_PAIR, (pair + 1) * SCAN_PAIR)

    def to_states(pair):
        for k, s in enumerate(pair_slabs(pair)):
            for b in range(BATCH):
                for t0 in range(0, SCAN_T, tile):
                    v = u_ref[b, t0:t0 + tile, s * LANES:(s + 1) * LANES]
                    lhs_ref[s, pl.ds(t0 * tile + b, tile, stride=tile), :] = jnp.where(low, v, 0.0)
                    lhs_ref[s, pl.ds(t0 * tile + BATCH + b, tile, stride=tile), :] = jnp.where(low, 0.0, v)
            bu_refs[pair][k] = jnp.dot(lhs_ref[s].astype(BF16), bmat_ref[s],
                                       preferred_element_type=F32)

    def scan(pair):
        bu_ref, hs_ref = bu_refs[pair], hs_refs[pair]
        state = [(ar_ref[s], ai_ref[s], carry_ref[s, :, re], carry_ref[s, :, im])
                 for s in pair_slabs(pair)]
        for t in range(0, SCAN_T, 2):
            for k in range(SCAN_PAIR):
                a_r, a_i, hr, hi = state[k]
                steps = []
                for rows in (slice(t * tile, (t + 1) * tile), slice((t + 1) * tile, (t + 2) * tile)):
                    hr, hi = ((a_r * hr - a_i * hi) + bu_ref[k, rows, re],
                              (a_r * hi + a_i * hr) + bu_ref[k, rows, im])
                    steps.append((hr, hi))
                both = slice(t * tile, (t + 2) * tile)
                hs_ref[k, both, re] = jnp.concatenate([steps[0][0], steps[1][0]], axis=0).astype(BF16)
                hs_ref[k, both, im] = jnp.concatenate([steps[0][1], steps[1][1]], axis=0).astype(BF16)
                state[k] = (a_r, a_i, hr, hi)
        for k, s in enumerate(pair_slabs(pair)):
            carry_ref[s, :, re] = state[k][2]
            carry_ref[s, :, im] = state[k][3]

    def from_states(pair):
        for k, s in enumerate(pair_slabs(pair)):
            yp_ref[s] = jnp.dot(hs_refs[pair][k], cmat_ref[s], preferred_element_type=F32)
            sl = slice(s * LANES, (s + 1) * LANES)
            dsk = d_ref[:, sl]
            for b in range(BATCH):
                for t0 in range(0, SCAN_T, 2 * tile):
                    ys = []
                    for t1 in (t0, t0 + tile):
                        y0 = yp_ref[s, pl.ds(t1 * tile + b, tile, stride=tile), :]
                        y1 = yp_ref[s, pl.ds(t1 * tile + BATCH + b, tile, stride=tile), :]
                        ys.append(jnp.where(low, y0, y1) + dsk * u_ref[b, t1:t1 + tile, sl])
                    o_ref[b, t0:t0 + 2 * tile, sl] = _gelu(jnp.concatenate(ys, axis=0)).astype(BF16)

    pairs = range(SCAN_SLABS // SCAN_PAIR)
    for pair in pairs:
        to_states(pair)
    for pair in pairs:
        scan(pair)
        from_states(pair)


def _s5_scan(u, bmat, cmat, a_r, a_i, d):
    width = SCAN_SLABS * LANES
    rows = SCAN_T * SUBLANES
    bu_buf = pltpu.VMEM((SCAN_PAIR, rows, 2 * HALF_STATES), F32)
    hs_buf = pltpu.VMEM((SCAN_PAIR, rows, 2 * HALF_STATES), BF16)
    return pl.pallas_call(
        _s5_scan_body,
        grid=(N_SLABS // SCAN_SLABS, SEQ // SCAN_T),
        in_specs=[
            pl.BlockSpec((BATCH, SCAN_T, width), lambda s, c: (0, c, s)),
            pl.BlockSpec((SCAN_SLABS, LANES, 2 * HALF_STATES), lambda s, c: (s, 0, 0)),
            pl.BlockSpec((SCAN_SLABS, 2 * HALF_STATES, LANES), lambda s, c: (s, 0, 0)),
            pl.BlockSpec((SCAN_SLABS, SUBLANES, HALF_STATES), lambda s, c: (s, 0, 0)),
            pl.BlockSpec((SCAN_SLABS, SUBLANES, HALF_STATES), lambda s, c: (s, 0, 0)),
            pl.BlockSpec((1, width), lambda s, c: (0, s)),
        ],
        out_specs=pl.BlockSpec((BATCH, SCAN_T, width), lambda s, c: (0, c, s)),
        out_shape=jax.ShapeDtypeStruct((BATCH, SEQ, D_MODEL), BF16),
        scratch_shapes=[
            pltpu.VMEM((SCAN_SLABS, rows, LANES), F32),
            bu_buf, bu_buf, hs_buf, hs_buf,
            pltpu.VMEM((SCAN_SLABS, rows, LANES), F32),
            pltpu.VMEM((SCAN_SLABS, SUBLANES, 2 * HALF_STATES), F32),
        ],
        compiler_params=_params(("parallel", "arbitrary")),
        name="s5_scan",
    )(u, bmat, cmat, a_r, a_i, d)


def _s5_scan_operands(a_r, a_i, bb_r, bb_i, c_re, c_im):
    halves = SLAB_GROUPS // 2
    eye = jnp.eye(halves, dtype=F32)

    def b_blocks(bb):
        bb = bb.reshape(SSM_GROUP_DIM, N_SLABS, 2, halves, SSM_STATE)
        blk = jnp.einsum('hsqgp,gk->sqghkp', bb, eye)
        return blk.reshape(N_SLABS, LANES, HALF_STATES)

    def c_blocks(c):
        c = c.reshape(N_SLABS, 2, halves, SSM_GROUP_DIM, SSM_STATE)
        blk = jnp.einsum('sqghp,gk->skpqgh', c, eye)
        return blk.reshape(N_SLABS, HALF_STATES, LANES)

    bmat = jnp.concatenate([b_blocks(bb_r), b_blocks(bb_i)], axis=-1).astype(BF16)
    cmat = jnp.concatenate([c_blocks(c_re), -c_blocks(c_im)], axis=1).astype(BF16)

    def a_rows(a):
        a = a.reshape(N_SLABS, 2, 1, HALF_STATES)
        return jnp.broadcast_to(a, (N_SLABS, 2, BATCH, HALF_STATES)).reshape(
            N_SLABS, SUBLANES, HALF_STATES)

    return bmat, cmat, a_rows(a_r), a_rows(a_i)


def _s5_out_body(x_ref, a_ref, w_ref, o_ref, *, tn):
    a = a_ref[...]
    for c in range(0, D_MODEL, tn):
        val = jnp.dot(a, w_ref[:, c:c + tn], preferred_element_type=F32)
        gate = jnp.dot(a, w_ref[:, D_MODEL + c:D_MODEL + c + tn], preferred_element_type=F32)
        o_ref[:, c:c + tn] = x_ref[:, c:c + tn] + val * _sigmoid(gate)


def _s5_out(x, act, w, *, tm=512, tn=512):
    return pl.pallas_call(
        functools.partial(_s5_out_body, tn=tn),
        grid=(TOKENS // tm,),
        in_specs=[
            pl.BlockSpec((tm, D_MODEL), lambda i: (i, 0)),
            pl.BlockSpec((tm, D_MODEL), lambda i: (i, 0)),
            _resident((D_MODEL, 2 * D_MODEL)),
        ],
        out_specs=pl.BlockSpec((tm, D_MODEL), lambda i: (i, 0)),
        out_shape=jax.ShapeDtypeStruct((TOKENS, D_MODEL), F32),
        compiler_params=_params(("parallel",)),
        name="s5_out",
    )(x, act, w)


def _s5(x, norm_g, layer, w_in, lam_re, lam_im, log_dt, b_re, b_im, c_re, c_im, d_skip, w_out):
    a_r, a_i, bb_r, bb_i = _s5_disc(lam_re, lam_im, log_dt, b_re, b_im)
    bmat, cmat, ar_rows, ai_rows = _s5_scan_operands(a_r, a_i, bb_r, bb_i, c_re, c_im)
    u, w_out_bf = _s5_in(x, norm_g, w_in, w_out, layer)
    act = _s5_scan(u.reshape(BATCH, SEQ, D_MODEL), bmat, cmat, ar_rows, ai_rows,
                   d_skip.reshape(1, D_MODEL))
    return _s5_out(x, act.reshape(TOKENS, D_MODEL), w_out_bf)


def _ple_body(x_ref, g_ref, wg_ref, p_ref, wp_ref, fg_ref, o_ref, wgb_ref, wpb_ref, *, final):
    @pl.when(pl.program_id(0) == 0)
    def _():
        wgb_ref[...] = wg_ref[...].astype(BF16)
        wpb_ref[...] = wp_ref[...].astype(BF16)

    x = x_ref[...]
    h = _rms(x, g_ref[...]).astype(BF16)
    gate = _sigmoid(jnp.dot(h, wgb_ref[...], preferred_element_type=F32))
    proj = jnp.dot(p_ref[...].astype(BF16), wpb_ref[...], preferred_element_type=F32)
    y = x + gate * proj
    o_ref[...] = _rms(y, fg_ref[...]) if final else y


def _ple(x, norm_g, wg, p, wp, fg, layer, *, final, tm=512):
    once = pl.Buffered(1)
    return pl.pallas_call(
        functools.partial(_ple_body, final=final),
        grid=(TOKENS // tm,),
        in_specs=[
            pl.BlockSpec((tm, D_MODEL), lambda i: (i, 0)),
            pl.BlockSpec((None, None, 1, D_MODEL), lambda i: (layer, 3, 0, 0)),
            pl.BlockSpec((None, D_MODEL, D_MODEL), lambda i: (layer, 0, 0), pipeline_mode=once),
            pl.BlockSpec((None, tm, PLE_DIM), lambda i: (layer, i, 0)),
            pl.BlockSpec((None, PLE_DIM, D_MODEL), lambda i: (layer, 0, 0), pipeline_mode=once),
            pl.BlockSpec((1, D_MODEL), lambda i: (0, 0)),
        ],
        out_specs=pl.BlockSpec((tm, D_MODEL), lambda i: (i, 0)),
        out_shape=jax.ShapeDtypeStruct((TOKENS, D_MODEL), F32),
        scratch_shapes=[pltpu.VMEM((D_MODEL, D_MODEL), BF16), pltpu.VMEM((PLE_DIM, D_MODEL), BF16)],
        compiler_params=_params(("arbitrary",)),
        name="ple",
    )(x, norm_g, wg, p, wp, fg)


def kernel(x, p, norm_g, final_norm_g, ffn_w_gate, ffn_w_up, ffn_w_down, gmlp_w_in, gmlp_ln_g, gmlp_ln_b, gmlp_w_s, gmlp_b_s, gmlp_w_out, s5_w_in, s5_lam_re, s5_lam_im, s5_log_dt, s5_b_re, s5_b_im, s5_c_re, s5_c_im, s5_d, s5_w_out, ple_w_gate, ple_w_proj):
    x = x.reshape(TOKENS, D_MODEL)
    p = p.reshape(DEPTH, TOKENS, PLE_DIM)
    norm_g = norm_g.reshape(DEPTH, 4, 1, D_MODEL)
    fg = final_norm_g.reshape(1, D_MODEL)
    for i in range(DEPTH):
        j = i // 2
        if i % 2 == 0:
            side = ((gmlp_w_in[j], GMLP_CAST_BLOCKS), (gmlp_w_out[j], GMLP_CAST_BLOCKS))
            x, (w_in, w_out) = _ffn(x, norm_g, ffn_w_gate, ffn_w_up, ffn_w_down, i, 0,
                                    side=side, tf=256, sub_rows=512)
            x = _gmlp(x, norm_g, w_in, gmlp_ln_g[j].reshape(1, GMLP_HALF),
                      gmlp_ln_b[j].reshape(1, GMLP_HALF), gmlp_w_s[j],
                      gmlp_b_s[j].reshape(GMLP_GROUPS, CHUNK, 1), w_out, i)
        else:
            x, _ = _ffn(x, norm_g, ffn_w_gate, ffn_w_up, ffn_w_down, i, 0)
            x = _s5(x, norm_g, i, s5_w_in[j], s5_lam_re[j], s5_lam_im[j], s5_log_dt[j],
                    s5_b_re[j], s5_b_im[j], s5_c_re[j], s5_c_im[j], s5_d[j], s5_w_out[j])
        x, _ = _ffn(x, norm_g, ffn_w_gate, ffn_w_up, ffn_w_down, i, 1)
        x = _ple(x, norm_g, ple_w_gate, p, ple_w_proj, fg, i, final=(i == DEPTH - 1))
    return x.reshape(BATCH, SEQ, D_MODEL)
```

```python
import functools
import math

import jax
import jax.numpy as jnp
from jax import lax
from jax.experimental import pallas as pl
from jax.experimental.pallas import tpu as pltpu

F32 = jnp.float32
BF16 = jnp.bfloat16

D_MODEL = 2048
BATCH = 4
SEQ = 2048
TOKENS = BATCH * SEQ
DEPTH = 2
D_FF = 5632
PLE_DIM = 256
CHUNK = 128
GMLP_HALF = 3 * D_MODEL
GMLP_GROUPS = 16
GMLP_GROUP_DIM = GMLP_HALF // GMLP_GROUPS
SSM_GROUP_DIM = 16
SSM_GROUPS = D_MODEL // SSM_GROUP_DIM
SSM_STATE = 64
EPS = 1e-6

LANES = 128
SUBLANES = 8
VMEM_LIMIT = 60 * 1024 * 1024

SLAB_GROUPS = LANES // SSM_GROUP_DIM
N_SLABS = D_MODEL // LANES
SLAB_STATES = SLAB_GROUPS * SSM_STATE
HALF_STATES = SLAB_STATES // 2
HALF_CH = LANES // 2
SCAN_PAIR = 2
SCAN_SLABS = 2 * SCAN_PAIR
SCAN_T = 256


def _rms(x, g):
    ms = jnp.mean(x * x, axis=-1, keepdims=True)
    return x * lax.rsqrt(ms + EPS) * g


def _gelu(x):
    c = math.sqrt(2.0 / math.pi)
    return x * (0.5 * (1.0 + jnp.tanh(c * (x + 0.044715 * (x * x * x)))))


def _sigmoid(x):
    return 1.0 / (1.0 + jnp.exp(-x))


def _params(sem):
    return pltpu.CompilerParams(dimension_semantics=sem, vmem_limit_bytes=VMEM_LIMIT)


def _gain_spec(layer, slot):
    return pl.BlockSpec((None, None, 1, D_MODEL), lambda i, j: (layer, slot, 0, 0))


def _token_tile_copy(x_hbm, x_ref, sem, tile):
    tm = x_ref.shape[0]
    rows = pl.ds(pl.multiple_of(tile * tm, tm), tm)
    return pltpu.make_async_copy(x_hbm.at[rows, :], x_ref, sem)


def _wait_token_tile(x_hbm, x_ref, sem):
    i = pl.program_id(0)

    @pl.when(i == 0)
    def _():
        _token_tile_copy(x_hbm, x_ref, sem, i).start()

    _token_tile_copy(x_hbm, x_ref, sem, i).wait()


def _prefetch_next_token_tile(x_hbm, x_ref, sem):
    i, j = pl.program_id(0), pl.program_id(1)

    @pl.when(jnp.logical_and(j == 1, i + 1 < pl.num_programs(0)))
    def _():
        _token_tile_copy(x_hbm, x_ref, sem, i + 1).start()


def _ffn_body(*refs, n_side, sub_rows):
    x_hbm, g_ref, wg_lo, wg_hi, wu_lo, wu_hi, wd_ref = refs[:7]
    side_in = refs[7:7 + n_side]
    o_ref = refs[7 + n_side]
    side_out = refs[8 + n_side:8 + 2 * n_side]
    h_ref, x_ref, sem = refs[8 + 2 * n_side:]

    @pl.when(pl.program_id(1) == 0)
    def _():
        _wait_token_tile(x_hbm, x_ref, sem)
        x = x_ref[...]
        h_ref[...] = _rms(x, g_ref[...]).astype(BF16)
        o_ref[...] = x

    _prefetch_next_token_tile(x_hbm, x_ref, sem)

    for src, dst in zip(side_in, side_out):
        dst[...] = src[...].astype(BF16)

    wg = jnp.concatenate([wg_lo[...].astype(BF16), wg_hi[...].astype(BF16)], axis=0)
    wu = jnp.concatenate([wu_lo[...].astype(BF16), wu_hi[...].astype(BF16)], axis=0)
    wd = wd_ref[...].astype(BF16)
    subs = [slice(r, r + sub_rows) for r in range(0, x_ref.shape[0], sub_rows)]
    acts = []
    for rows in subs:
        h = h_ref[rows, :]
        gate = jnp.dot(h, wg, preferred_element_type=F32)
        up = jnp.dot(h, wu, preferred_element_type=F32)
        acts.append(((gate * _sigmoid(gate)) * (0.5 * up)).astype(BF16))
    for rows, a in zip(subs, acts):
        o_ref[rows, :] += jnp.dot(a, wd, preferred_element_type=F32)


def _ffn(x, norm_g, wg, wu, wd, layer, which, *, side=(), tm=1024, tf=512, sub_rows=None):
    ni, nj = TOKENS // tm, D_FF // tf
    side_specs, side_shapes = [], []
    for arr, n_blocks in side:
        assert n_blocks <= ni * nj and arr.shape[0] % n_blocks == 0
        block = (arr.shape[0] // n_blocks, arr.shape[1])
        assert block[0] % (2 * SUBLANES) == 0
        index = functools.partial(
            lambda i, j, last: (jnp.minimum(i * nj + j, last), 0), last=n_blocks - 1)
        side_specs.append(pl.BlockSpec(block, index))
        side_shapes.append(jax.ShapeDtypeStruct(arr.shape, BF16))
    half_k = lambda r: pl.BlockSpec((None, None, D_MODEL // 2, tf),
                                    lambda i, j: (layer, which, r, j))
    out = pl.pallas_call(
        functools.partial(_ffn_body, n_side=len(side), sub_rows=sub_rows or tm),
        grid=(ni, nj),
        in_specs=[
            pl.BlockSpec(memory_space=pl.ANY),
            _gain_spec(layer, 2 * which),
            half_k(0), half_k(1), half_k(0), half_k(1),
            pl.BlockSpec((None, None, tf, D_MODEL), lambda i, j: (layer, which, j, 0)),
        ] + side_specs,
        out_specs=[pl.BlockSpec((tm, D_MODEL), lambda i, j: (i, 0))] + side_specs,
        out_shape=[jax.ShapeDtypeStruct((TOKENS, D_MODEL), F32)] + side_shapes,
        scratch_shapes=[pltpu.VMEM((tm, D_MODEL), BF16), pltpu.VMEM((tm, D_MODEL), F32),
                        pltpu.SemaphoreType.DMA(())],
        compiler_params=_params(("arbitrary", "arbitrary")),
        name="ffn",
    )(x, norm_g, wg, wg, wu, wu, wd, *[arr for arr, _ in side])
    return out[0], out[1:]


GMLP_STEP_GROUPS = 2
GMLP_CAST_BLOCKS = 128
GMLP_SUBTILE = 256
GMLP_STEP_W = GMLP_STEP_GROUPS * GMLP_GROUP_DIM


def _gmlp_body(x_hbm, g_ref, wu_ref, wv_ref, lng_ref, lnb_ref, ws_ref, bs_ref, wo_ref,
               o_ref, h_ref, gated_ref, x_ref, sem):
    @pl.when(pl.program_id(1) == 0)
    def _():
        _wait_token_tile(x_hbm, x_ref, sem)
        x = x_ref[...]
        h_ref[...] = _rms(x, g_ref[...]).astype(BF16)
        o_ref[...] = x

    _prefetch_next_token_tile(x_hbm, x_ref, sem)

    tm = x_ref.shape[0]
    subs = [slice(r, r + GMLP_SUBTILE) for r in range(0, tm, GMLP_SUBTILE)]
    z = []
    for rows in subs:
        h = h_ref[rows, :]
        zv = _gelu(jnp.dot(h, wv_ref[...], preferred_element_type=F32))
        zu = _gelu(jnp.dot(h, wu_ref[...], preferred_element_type=F32))
        z.append((zu, zv))
    row = lax.broadcasted_iota(jnp.int32, (CHUNK, CHUNK), 0)
    col = lax.broadcasted_iota(jnp.int32, (CHUNK, CHUNK), 1)
    ws = [jnp.where(row >= col, ws_ref[gi], 0.0).astype(BF16)
          for gi in range(GMLP_STEP_GROUPS)]
    for rows, (zu, zv) in zip(subs, z):
        for gi in range(GMLP_STEP_GROUPS):
            lo, hi = gi * GMLP_GROUP_DIM, (gi + 1) * GMLP_GROUP_DIM
            v = zv[:, lo:hi]
            d = v - jnp.mean(v, axis=-1, keepdims=True)
            var = jnp.mean(d * d, axis=-1, keepdims=True)
            vn = ((d * lax.rsqrt(var + EPS)) * lng_ref[:, lo:hi] + lnb_ref[:, lo:hi]).astype(BF16)
            bias = bs_ref[gi]
            for c in range(0, GMLP_SUBTILE, CHUNK):
                sv = jnp.dot(ws[gi], vn[c:c + CHUNK], preferred_element_type=F32) + bias
                out_rows = slice(rows.start + c, rows.start + c + CHUNK)
                gated_ref[out_rows, lo:hi] = (zu[c:c + CHUNK, lo:hi] * sv).astype(BF16)
    for rows in subs:
        o_ref[rows, :] += jnp.dot(gated_ref[rows, :], wo_ref[...], preferred_element_type=F32)


def _gmlp(x, norm_g, w_in, ln_g, ln_b, w_s, b_s, w_out, layer, *, tm=1024):
    nj = GMLP_GROUPS // GMLP_STEP_GROUPS
    return pl.pallas_call(
        _gmlp_body,
        grid=(TOKENS // tm, nj),
        in_specs=[
            pl.BlockSpec(memory_space=pl.ANY),
            _gain_spec(layer, 1),
            pl.BlockSpec((D_MODEL, GMLP_STEP_W), lambda i, j: (0, j)),
            pl.BlockSpec((D_MODEL, GMLP_STEP_W), lambda i, j: (0, j + nj)),
            pl.BlockSpec((1, GMLP_STEP_W), lambda i, j: (0, j)),
            pl.BlockSpec((1, GMLP_STEP_W), lambda i, j: (0, j)),
            pl.BlockSpec((GMLP_STEP_GROUPS, CHUNK, CHUNK), lambda i, j: (j, 0, 0)),
            pl.BlockSpec((GMLP_STEP_GROUPS, CHUNK, 1), lambda i, j: (j, 0, 0)),
            pl.BlockSpec((GMLP_STEP_W, D_MODEL), lambda i, j: (j, 0)),
        ],
        out_specs=pl.BlockSpec((tm, D_MODEL), lambda i, j: (i, 0)),
        out_shape=jax.ShapeDtypeStruct((TOKENS, D_MODEL), F32),
        scratch_shapes=[pltpu.VMEM((tm, D_MODEL), BF16), pltpu.VMEM((tm, GMLP_STEP_W), BF16),
                        pltpu.VMEM((tm, D_MODEL), F32), pltpu.SemaphoreType.DMA(())],
        compiler_params=_params(("arbitrary", "arbitrary")),
        name="gmlp",
    )(x, norm_g, w_in, w_in, ln_g, ln_b, w_s, b_s, w_out)


def _s5_disc_body(lr_ref, li_ref, ldt_ref, br_ref, bi_ref, ar_ref, ai_ref, bbr_ref, bbi_ref):
    lr, li = lr_ref[...], li_ref[...]
    dt = jnp.exp(ldt_ref[...])
    mag = jnp.exp(lr * dt)
    ang = li * dt
    a_r = mag * jnp.cos(ang)
    a_i = mag * jnp.sin(ang)
    den = lr * lr + li * li
    nr = a_r - 1.0
    z_r = (nr * lr + a_i * li) / den
    z_i = (a_i * lr - nr * li) / den
    ar_ref[...] = a_r
    ai_ref[...] = a_i
    bbr_ref[...] = z_r * br_ref[...] - z_i * bi_ref[...]
    bbi_ref[...] = z_r * bi_ref[...] + z_i * br_ref[...]


def _s5_disc(lam_re, lam_im, log_dt, b_re, b_im):
    n = SSM_GROUPS * SSM_STATE
    row = jax.ShapeDtypeStruct((1, n), F32)
    mat = jax.ShapeDtypeStruct((SSM_GROUP_DIM, n), F32)
    to_mat = lambda b: jnp.transpose(b, (2, 0, 1)).reshape(SSM_GROUP_DIM, n)
    ldt = jnp.broadcast_to(log_dt[:, None], (SSM_GROUPS, SSM_STATE)).reshape(1, n)
    return pl.pallas_call(_s5_disc_body, out_shape=(row, row, mat, mat), name="s5_disc")(
        lam_re.reshape(1, n), lam_im.reshape(1, n), ldt, to_mat(b_re), to_mat(b_im))


def _resident(shape):
    zeros = (0,) * len(shape)
    return pl.BlockSpec(shape, lambda i: zeros, pipeline_mode=pl.Buffered(1))


def _s5_in_body(x_ref, g_ref, w_ref, wo_ref, o_ref, wob_ref, wb_ref):
    @pl.when(pl.program_id(0) == 0)
    def _():
        wb_ref[...] = w_ref[...].astype(BF16)

    wob_ref[...] = wo_ref[...].astype(BF16)
    h = _rms(x_ref[...], g_ref[...]).astype(BF16)
    o_ref[...] = jnp.dot(h, wb_ref[...], preferred_element_type=F32)


def _s5_in(x, norm_g, w, w_out, layer, *, tm=512):
    steps = TOKENS // tm
    wo_block = pl.BlockSpec((D_MODEL // steps, 2 * D_MODEL), lambda i: (i, 0))
    return pl.pallas_call(
        _s5_in_body,
        grid=(steps,),
        in_specs=[
            pl.BlockSpec((tm, D_MODEL), lambda i: (i, 0)),
            pl.BlockSpec((None, None, 1, D_MODEL), lambda i: (layer, 1, 0, 0)),
            _resident((D_MODEL, D_MODEL)),
            wo_block,
        ],
        out_specs=[pl.BlockSpec((tm, D_MODEL), lambda i: (i, 0)), wo_block],
        out_shape=[jax.ShapeDtypeStruct((TOKENS, D_MODEL), F32),
                   jax.ShapeDtypeStruct((D_MODEL, 2 * D_MODEL), BF16)],
        scratch_shapes=[pltpu.VMEM((D_MODEL, D_MODEL), BF16)],
        compiler_params=_params(("arbitrary",)),
        name="s5_in",
    )(x, norm_g, w, w_out)


def _s5_scan_body(u_ref, bmat_ref, cmat_ref, ar_ref, ai_ref, d_ref, o_ref,
                  lhs_ref, bu0_ref, bu1_ref, hs0_ref, hs1_ref, yp_ref, carry_ref):
    tile = SUBLANES
    lane = lax.broadcasted_iota(jnp.int32, (tile, LANES), 1)
    low = lane < HALF_CH
    bu_refs = (bu0_ref, bu1_ref)
    hs_refs = (hs0_ref, hs1_ref)

    @pl.when(pl.program_id(1) == 0)
    def _():
        carry_ref[...] = jnp.zeros_like(carry_ref)

    re, im = slice(0, HALF_STATES), slice(HALF_STATES, 2 * HALF_STATES)
    pair_slabs = lambda pair: range(pair * SCAN_PAIR, (pair + 1) * SCAN_PAIR)

    def to_states(pair):
        for k, s in enumerate(pair_slabs(pair)):
            for b in range(BATCH):
                for t0 in range(0, SCAN_T, tile):
                    v = u_ref[b, t0:t0 + tile, s * LANES:(s + 1) * LANES]
                    lhs_ref[s, pl.ds(t0 * tile + b, tile, stride=tile), :] = jnp.where(low, v, 0.0)
                    lhs_ref[s, pl.ds(t0 * tile + BATCH + b, tile, stride=tile), :] = jnp.where(low, 0.0, v)
            bu_refs[pair][k] = jnp.dot(lhs_ref[s].astype(BF16), bmat_ref[s],
                                       preferred_element_type=F32)

    def scan(pair):
        bu_ref, hs_ref = bu_refs[pair], hs_refs[pair]
        state = [(ar_ref[s], ai_ref[s], carry_ref[s, :, re], carry_ref[s, :, im])
                 for s in pair_slabs(pair)]
        for t in range(0, SCAN_T, 2):
            for k in range(SCAN_PAIR):
                a_r, a_i, hr, hi = state[k]
                steps = []
                for rows in (slice(t * tile, (t + 1) * tile), slice((t + 1) * tile, (t + 2) * tile)):
                    hr, hi = ((a_r * hr - a_i * hi) + bu_ref[k, rows, re],
                              (a_r * hi + a_i * hr) + bu_ref[k, rows, im])
                    steps.append((hr, hi))
                both = slice(t * tile, (t + 2) * tile)
                hs_ref[k, both, re] = jnp.concatenate([steps[0][0], steps[1][0]], axis=0).astype(BF16)
                hs_ref[k, both, im] = jnp.concatenate([steps[0][1], steps[1][1]], axis=0).astype(BF16)
                state[k] = (a_r, a_i, hr, hi)
        for k, s in enumerate(pair_slabs(pair)):
            carry_ref[s, :, re] = state[k][2]
            carry_ref[s, :, im] = state[k][3]

    def from_states(pair):
        for k, s in enumerate(pair_slabs(pair)):
            yp_ref[s] = jnp.dot(hs_refs[pair][k], cmat_ref[s], preferred_element_type=F32)
            sl = slice(s * LANES, (s + 1) * LANES)
            dsk = d_ref[:, sl]
            for b in range(BATCH):
                for t0 in range(0, SCAN_T, 2 * tile):
                    ys = []
                    for t1 in (t0, t0 + tile):
                        y0 = yp_ref[s, pl.ds(t1 * tile + b, tile, stride=tile), :]
                        y1 = yp_ref[s, pl.ds(t1 * tile + BATCH + b, tile, stride=tile), :]
                        ys.append(jnp.where(low, y0, y1) + dsk * u_ref[b, t1:t1 + tile, sl])
                    o_ref[b, t0:t0 + 2 * tile, sl] = _gelu(jnp.concatenate(ys, axis=0)).astype(BF16)

    pairs = range(SCAN_SLABS // SCAN_PAIR)
    for pair in pairs:
        to_states(pair)
    for pair in pairs:
        scan(pair)
        from_states(pair)


def _s5_scan(u, bmat, cmat, a_r, a_i, d):
    width = SCAN_SLABS * LANES
    rows = SCAN_T * SUBLANES
    bu_buf = pltpu.VMEM((SCAN_PAIR, rows, 2 * HALF_STATES), F32)
    hs_buf = pltpu.VMEM((SCAN_PAIR, rows, 2 * HALF_STATES), BF16)
    return pl.pallas_call(
        _s5_scan_body,
        grid=(N_SLABS // SCAN_SLABS, SEQ // SCAN_T),
        in_specs=[
            pl.BlockSpec((BATCH, SCAN_T, width), lambda s, c: (0, c, s)),
            pl.BlockSpec((SCAN_SLABS, LANES, 2 * HALF_STATES), lambda s, c: (s, 0, 0)),
            pl.BlockSpec((SCAN_SLABS, 2 * HALF_STATES, LANES), lambda s, c: (s, 0, 0)),
            pl.BlockSpec((SCAN_SLABS, SUBLANES, HALF_STATES), lambda s, c: (s, 0, 0)),
            pl.BlockSpec((SCAN_SLABS, SUBLANES, HALF_STATES), lambda s, c: (s, 0, 0)),
            pl.BlockSpec((1, width), lambda s, c: (0, s)),
        ],
        out_specs=pl.BlockSpec((BATCH, SCAN_T, width), lambda s, c: (0, c, s)),
        out_shape=jax.ShapeDtypeStruct((BATCH, SEQ, D_MODEL), BF16),
        scratch_shapes=[
            pltpu.VMEM((SCAN_SLABS, rows, LANES), F32),
            bu_buf, bu_buf, hs_buf, hs_buf,
            pltpu.VMEM((SCAN_SLABS, rows, LANES), F32),
            pltpu.VMEM((SCAN_SLABS, SUBLANES, 2 * HALF_STATES), F32),
        ],
        compiler_params=_params(("parallel", "arbitrary")),
        name="s5_scan",
    )(u, bmat, cmat, a_r, a_i, d)


def _s5_scan_operands(a_r, a_i, bb_r, bb_i, c_re, c_im):
    halves = SLAB_GROUPS // 2
    eye = jnp.eye(halves, dtype=F32)

    def b_blocks(bb):
        bb = bb.reshape(SSM_GROUP_DIM, N_SLABS, 2, halves, SSM_STATE)
        blk = jnp.einsum('hsqgp,gk->sqghkp', bb, eye)
        return blk.reshape(N_SLABS, LANES, HALF_STATES)

    def c_blocks(c):
        c = c.reshape(N_SLABS, 2, halves, SSM_GROUP_DIM, SSM_STATE)
        blk = jnp.einsum('sqghp,gk->skpqgh', c, eye)
        return blk.reshape(N_SLABS, HALF_STATES, LANES)

    bmat = jnp.concatenate([b_blocks(bb_r), b_blocks(bb_i)], axis=-1).astype(BF16)
    cmat = jnp.concatenate([c_blocks(c_re), -c_blocks(c_im)], axis=1).astype(BF16)

    def a_rows(a):
        a = a.reshape(N_SLABS, 2, 1, HALF_STATES)
        return jnp.broadcast_to(a, (N_SLABS, 2, BATCH, HALF_STATES)).reshape(
            N_SLABS, SUBLANES, HALF_STATES)

    return bmat, cmat, a_rows(a_r), a_rows(a_i)


def _s5_out_body(x_ref, a_ref, w_ref, o_ref, *, tn):
    a = a_ref[...]
    for c in range(0, D_MODEL, tn):
        val = jnp.dot(a, w_ref[:, c:c + tn], preferred_element_type=F32)
        gate = jnp.dot(a, w_ref[:, D_MODEL + c:D_MODEL + c + tn], preferred_element_type=F32)
        o_ref[:, c:c + tn] = x_ref[:, c:c + tn] + val * _sigmoid(gate)


def _s5_out(x, act, w, *, tm=512, tn=512):
    return pl.pallas_call(
        functools.partial(_s5_out_body, tn=tn),
        grid=(TOKENS // tm,),
        in_specs=[
            pl.BlockSpec((tm, D_MODEL), lambda i: (i, 0)),
            pl.BlockSpec((tm, D_MODEL), lambda i: (i, 0)),
            _resident((D_MODEL, 2 * D_MODEL)),
        ],
        out_specs=pl.BlockSpec((tm, D_MODEL), lambda i: (i, 0)),
        out_shape=jax.ShapeDtypeStruct((TOKENS, D_MODEL), F32),
        compiler_params=_params(("parallel",)),
        name="s5_out",
    )(x, act, w)


def _s5(x, norm_g, layer, w_in, lam_re, lam_im, log_dt, b_re, b_im, c_re, c_im, d_skip, w_out):
    a_r, a_i, bb_r, bb_i = _s5_disc(lam_re, lam_im, log_dt, b_re, b_im)
    bmat, cmat, ar_rows, ai_rows = _s5_scan_operands(a_r, a_i, bb_r, bb_i, c_re, c_im)
    u, w_out_bf = _s5_in(x, norm_g, w_in, w_out, layer)
    act = _s5_scan(u.reshape(BATCH, SEQ, D_MODEL), bmat, cmat, ar_rows, ai_rows,
                   d_skip.reshape(1, D_MODEL))
    return _s5_out(x, act.reshape(TOKENS, D_MODEL), w_out_bf)


def _ple_body(x_ref, g_ref, wg_ref, p_ref, wp_ref, fg_ref, o_ref, wgb_ref, wpb_ref, *, final):
    @pl.when(pl.program_id(0) == 0)
    def _():
        wgb_ref[...] = wg_ref[...].astype(BF16)
        wpb_ref[...] = wp_ref[...].astype(BF16)

    x = x_ref[...]
    h = _rms(x, g_ref[...]).astype(BF16)
    gate = _sigmoid(jnp.dot(h, wgb_ref[...], preferred_element_type=F32))
    proj = jnp.dot(p_ref[...].astype(BF16), wpb_ref[...], preferred_element_type=F32)
    y = x + gate * proj
    o_ref[...] = _rms(y, fg_ref[...]) if final else y


def _ple(x, norm_g, wg, p, wp, fg, layer, *, final, tm=512):
    once = pl.Buffered(1)
    return pl.pallas_call(
        functools.partial(_ple_body, final=final),
        grid=(TOKENS // tm,),
        in_specs=[
            pl.BlockSpec((tm, D_MODEL), lambda i: (i, 0)),
            pl.BlockSpec((None, None, 1, D_MODEL), lambda i: (layer, 3, 0, 0)),
            pl.BlockSpec((None, D_MODEL, D_MODEL), lambda i: (layer, 0, 0), pipeline_mode=once),
            pl.BlockSpec((None, tm, PLE_DIM), lambda i: (layer, i, 0)),
            pl.BlockSpec((None, PLE_DIM, D_MODEL), lambda i: (layer, 0, 0), pipeline_mode=once),
            pl.BlockSpec((1, D_MODEL), lambda i: (0, 0)),
        ],
        out_specs=pl.BlockSpec((tm, D_MODEL), lambda i: (i, 0)),
        out_shape=jax.ShapeDtypeStruct((TOKENS, D_MODEL), F32),
        scratch_shapes=[pltpu.VMEM((D_MODEL, D_MODEL), BF16), pltpu.VMEM((PLE_DIM, D_MODEL), BF16)],
        compiler_params=_params(("arbitrary",)),
        name="ple",
    )(x, norm_g, wg, p, wp, fg)


def kernel(x, p, norm_g, final_norm_g, ffn_w_gate, ffn_w_up, ffn_w_down, gmlp_w_in, gmlp_ln_g, gmlp_ln_b, gmlp_w_s, gmlp_b_s, gmlp_w_out, s5_w_in, s5_lam_re, s5_lam_im, s5_log_dt, s5_b_re, s5_b_im, s5_c_re, s5_c_im, s5_d, s5_w_out, ple_w_gate, ple_w_proj):
    x = x.reshape(TOKENS, D_MODEL)
    p = p.reshape(DEPTH, TOKENS, PLE_DIM)
    norm_g = norm_g.reshape(DEPTH, 4, 1, D_MODEL)
    fg = final_norm_g.reshape(1, D_MODEL)
    for i in range(DEPTH):
        j = i // 2
        if i % 2 == 0:
            side = ((gmlp_w_in[j], GMLP_CAST_BLOCKS), (gmlp_w_out[j], GMLP_CAST_BLOCKS))
            x, (w_in, w_out) = _ffn(x, norm_g, ffn_w_gate, ffn_w_up, ffn_w_down, i, 0,
                                    side=side, tf=256, sub_rows=512)
            x = _gmlp(x, norm_g, w_in, gmlp_ln_g[j].reshape(1, GMLP_HALF),
                      gmlp_ln_b[j].reshape(1, GMLP_HALF), gmlp_w_s[j],
                      gmlp_b_s[j].reshape(GMLP_GROUPS, CHUNK, 1), w_out, i)
        else:
            x, _ = _ffn(x, norm_g, ffn_w_gate, ffn_w_up, ffn_w_down, i, 0)
            x = _s5(x, norm_g, i, s5_w_in[j], s5_lam_re[j], s5_lam_im[j], s5_log_dt[j],
                    s5_b_re[j], s5_b_im[j], s5_c_re[j], s5_c_im[j], s5_d[j], s5_w_out[j])
        x, _ = _ffn(x, norm_g, ffn_w_gate, ffn_w_up, ffn_w_down, i, 1)
        x = _ple(x, norm_g, ple_w_gate, p, ple_w_proj, fg, i, final=(i == DEPTH - 1))
    return x.reshape(BATCH, SEQ, D_MODEL)
```

```python
import functools
import math

import jax
import jax.numpy as jnp
from jax import lax
from jax.experimental import pallas as pl
from jax.experimental.pallas import tpu as pltpu

F32 = jnp.float32
BF16 = jnp.bfloat16

D_MODEL = 2048
BATCH = 4
SEQ = 2048
TOKENS = BATCH * SEQ
DEPTH = 2
D_FF = 5632
PLE_DIM = 256
CHUNK = 128
GMLP_HALF = 3 * D_MODEL
GMLP_GROUPS = 16
GMLP_GROUP_DIM = GMLP_HALF // GMLP_GROUPS
SSM_GROUP_DIM = 16
SSM_GROUPS = D_MODEL // SSM_GROUP_DIM
SSM_STATE = 64
EPS = 1e-6

LANES = 128
SUBLANES = 8
VMEM_LIMIT = 60 * 1024 * 1024

SLAB_GROUPS = LANES // SSM_GROUP_DIM
N_SLABS = D_MODEL // LANES
SLAB_STATES = SLAB_GROUPS * SSM_STATE
HALF_STATES = SLAB_STATES // 2
HALF_CH = LANES // 2
SCAN_PAIR = 2
SCAN_SLABS = 2 * SCAN_PAIR
SCAN_T = 256


def _rms(x, g):
    ms = jnp.mean(x * x, axis=-1, keepdims=True)
    return x * lax.rsqrt(ms + EPS) * g


def _gelu(x):
    c = math.sqrt(2.0 / math.pi)
    return x * (0.5 * (1.0 + jnp.tanh(c * (x + 0.044715 * (x * x * x)))))


def _sigmoid(x):
    return 1.0 / (1.0 + jnp.exp(-x))


def _params(sem):
    return pltpu.CompilerParams(dimension_semantics=sem, vmem_limit_bytes=VMEM_LIMIT)


def _gain_spec(layer, slot):
    return pl.BlockSpec((None, None, 1, D_MODEL), lambda i, j: (layer, slot, 0, 0))


def _token_tile_copy(x_hbm, x_ref, sem, tile):
    tm = x_ref.shape[0]
    rows = pl.ds(pl.multiple_of(tile * tm, tm), tm)
    return pltpu.make_async_copy(x_hbm.at[rows, :], x_ref, sem)


def _wait_token_tile(x_hbm, x_ref, sem):
    i = pl.program_id(0)

    @pl.when(i == 0)
    def _():
        _token_tile_copy(x_hbm, x_ref, sem, i).start()

    _token_tile_copy(x_hbm, x_ref, sem, i).wait()


def _prefetch_next_token_tile(x_hbm, x_ref, sem):
    i, j = pl.program_id(0), pl.program_id(1)

    @pl.when(jnp.logical_and(j == 1, i + 1 < pl.num_programs(0)))
    def _():
        _token_tile_copy(x_hbm, x_ref, sem, i + 1).start()


def _ffn_body(*refs, n_side, sub_rows, k_parts):
    x_hbm, g_ref = refs[:2]
    wg_parts, wu_parts = refs[2:2 + k_parts], refs[2 + k_parts:2 + 2 * k_parts]
    wd_ref = refs[2 + 2 * k_parts]
    rest = refs[3 + 2 * k_parts:]
    side_in, o_ref, side_out = rest[:n_side], rest[n_side], rest[n_side + 1:2 * n_side + 1]
    h_ref, x_ref, sem = rest[2 * n_side + 1:]

    @pl.when(pl.program_id(1) == 0)
    def _():
        _wait_token_tile(x_hbm, x_ref, sem)
        x = x_ref[...]
        h_ref[...] = _rms(x, g_ref[...]).astype(BF16)
        o_ref[...] = x

    _prefetch_next_token_tile(x_hbm, x_ref, sem)

    for src, dst in zip(side_in, side_out):
        dst[...] = src[...].astype(BF16)

    wg = jnp.concatenate([w[...].astype(BF16) for w in wg_parts], axis=0)
    wu = jnp.concatenate([w[...].astype(BF16) for w in wu_parts], axis=0)
    wd = wd_ref[...].astype(BF16)
    subs = [slice(r, r + sub_rows) for r in range(0, x_ref.shape[0], sub_rows)]
    acts = []
    for rows in subs:
        h = h_ref[rows, :]
        gate = jnp.dot(h, wg, preferred_element_type=F32)
        up = jnp.dot(h, wu, preferred_element_type=F32)
        acts.append(((gate * _sigmoid(gate)) * (0.5 * up)).astype(BF16))
    for rows, a in zip(subs, acts):
        o_ref[rows, :] += jnp.dot(a, wd, preferred_element_type=F32)


def _ffn(x, norm_g, wg, wu, wd, layer, which, *, side=(), tm=1024, tf=512, sub_rows=None,
         k_parts=1):
    ni, nj = TOKENS // tm, D_FF // tf
    side_specs, side_shapes = [], []
    for arr, n_blocks in side:
        assert n_blocks <= ni * nj and arr.shape[0] % n_blocks == 0
        block = (arr.shape[0] // n_blocks, arr.shape[1])
        assert block[0] % (2 * SUBLANES) == 0
        index = functools.partial(
            lambda i, j, last: (jnp.minimum(i * nj + j, last), 0), last=n_blocks - 1)
        side_specs.append(pl.BlockSpec(block, index))
        side_shapes.append(jax.ShapeDtypeStruct(arr.shape, BF16))
    k_slices = [pl.BlockSpec((None, None, D_MODEL // k_parts, tf),
                             functools.partial(lambda i, j, r: (layer, which, r, j), r=r))
                for r in range(k_parts)]
    out = pl.pallas_call(
        functools.partial(_ffn_body, n_side=len(side), sub_rows=sub_rows or tm,
                          k_parts=k_parts),
        grid=(ni, nj),
        in_specs=[
            pl.BlockSpec(memory_space=pl.ANY),
            _gain_spec(layer, 2 * which),
            *k_slices, *k_slices,
            pl.BlockSpec((None, None, tf, D_MODEL), lambda i, j: (layer, which, j, 0)),
        ] + side_specs,
        out_specs=[pl.BlockSpec((tm, D_MODEL), lambda i, j: (i, 0))] + side_specs,
        out_shape=[jax.ShapeDtypeStruct((TOKENS, D_MODEL), F32)] + side_shapes,
        scratch_shapes=[pltpu.VMEM((tm, D_MODEL), BF16), pltpu.VMEM((tm, D_MODEL), F32),
                        pltpu.SemaphoreType.DMA(())],
        compiler_params=_params(("arbitrary", "arbitrary")),
        name="ffn",
    )(x, norm_g, *[wg] * k_parts, *[wu] * k_parts, wd, *[arr for arr, _ in side])
    return out[0], out[1:]


GMLP_STEP_GROUPS = 2
GMLP_CAST_BLOCKS = 128
GMLP_SUBTILE = 256
GMLP_STEP_W = GMLP_STEP_GROUPS * GMLP_GROUP_DIM


def _gmlp_body(x_hbm, g_ref, wu_ref, wv_ref, lng_ref, lnb_ref, ws_ref, bs_ref, wo_ref,
               o_ref, h_ref, gated_ref, x_ref, sem):
    @pl.when(pl.program_id(1) == 0)
    def _():
        _wait_token_tile(x_hbm, x_ref, sem)
        x = x_ref[...]
        h_ref[...] = _rms(x, g_ref[...]).astype(BF16)
        o_ref[...] = x

    _prefetch_next_token_tile(x_hbm, x_ref, sem)

    tm = x_ref.shape[0]
    subs = [slice(r, r + GMLP_SUBTILE) for r in range(0, tm, GMLP_SUBTILE)]
    z = []
    for rows in subs:
        h = h_ref[rows, :]
        zv = _gelu(jnp.dot(h, wv_ref[...], preferred_element_type=F32))
        zu = _gelu(jnp.dot(h, wu_ref[...], preferred_element_type=F32))
        z.append((zu, zv))
    row = lax.broadcasted_iota(jnp.int32, (CHUNK, CHUNK), 0)
    col = lax.broadcasted_iota(jnp.int32, (CHUNK, CHUNK), 1)
    ws = [jnp.where(row >= col, ws_ref[gi], 0.0).astype(BF16)
          for gi in range(GMLP_STEP_GROUPS)]
    for rows, (zu, zv) in zip(subs, z):
        for gi in range(GMLP_STEP_GROUPS):
            lo, hi = gi * GMLP_GROUP_DIM, (gi + 1) * GMLP_GROUP_DIM
            v = zv[:, lo:hi]
            d = v - jnp.mean(v, axis=-1, keepdims=True)
            var = jnp.mean(d * d, axis=-1, keepdims=True)
            vn = ((d * lax.rsqrt(var + EPS)) * lng_ref[:, lo:hi] + lnb_ref[:, lo:hi]).astype(BF16)
            bias = bs_ref[gi]
            for c in range(0, GMLP_SUBTILE, CHUNK):
                sv = jnp.dot(ws[gi], vn[c:c + CHUNK], preferred_element_type=F32) + bias
                out_rows = slice(rows.start + c, rows.start + c + CHUNK)
                gated_ref[out_rows, lo:hi] = (zu[c:c + CHUNK, lo:hi] * sv).astype(BF16)
    for rows in subs:
        o_ref[rows, :] += jnp.dot(gated_ref[rows, :], wo_ref[...], preferred_element_type=F32)


def _gmlp(x, norm_g, w_in, ln_g, ln_b, w_s, b_s, w_out, layer, *, tm=1024):
    nj = GMLP_GROUPS // GMLP_STEP_GROUPS
    return pl.pallas_call(
        _gmlp_body,
        grid=(TOKENS // tm, nj),
        in_specs=[
            pl.BlockSpec(memory_space=pl.ANY),
            _gain_spec(layer, 1),
            pl.BlockSpec((D_MODEL, GMLP_STEP_W), lambda i, j: (0, j)),
            pl.BlockSpec((D_MODEL, GMLP_STEP_W), lambda i, j: (0, j + nj)),
            pl.BlockSpec((1, GMLP_STEP_W), lambda i, j: (0, j)),
            pl.BlockSpec((1, GMLP_STEP_W), lambda i, j: (0, j)),
            pl.BlockSpec((GMLP_STEP_GROUPS, CHUNK, CHUNK), lambda i, j: (j, 0, 0)),
            pl.BlockSpec((GMLP_STEP_GROUPS, CHUNK, 1), lambda i, j: (j, 0, 0)),
            pl.BlockSpec((GMLP_STEP_W, D_MODEL), lambda i, j: (j, 0)),
        ],
        out_specs=pl.BlockSpec((tm, D_MODEL), lambda i, j: (i, 0)),
        out_shape=jax.ShapeDtypeStruct((TOKENS, D_MODEL), F32),
        scratch_shapes=[pltpu.VMEM((tm, D_MODEL), BF16), pltpu.VMEM((tm, GMLP_STEP_W), BF16),
                        pltpu.VMEM((tm, D_MODEL), F32), pltpu.SemaphoreType.DMA(())],
        compiler_params=_params(("arbitrary", "arbitrary")),
        name="gmlp",
    )(x, norm_g, w_in, w_in, ln_g, ln_b, w_s, b_s, w_out)


def _s5_disc_body(lr_ref, li_ref, ldt_ref, br_ref, bi_ref, ar_ref, ai_ref, bbr_ref, bbi_ref):
    lr, li = lr_ref[...], li_ref[...]
    dt = jnp.exp(ldt_ref[...])
    mag = jnp.exp(lr * dt)
    ang = li * dt
    a_r = mag * jnp.cos(ang)
    a_i = mag * jnp.sin(ang)
    den = lr * lr + li * li
    nr = a_r - 1.0
    z_r = (nr * lr + a_i * li) / den
    z_i = (a_i * lr - nr * li) / den
    ar_ref[...] = a_r
    ai_ref[...] = a_i
    bbr_ref[...] = z_r * br_ref[...] - z_i * bi_ref[...]
    bbi_ref[...] = z_r * bi_ref[...] + z_i * br_ref[...]


def _s5_disc(lam_re, lam_im, log_dt, b_re, b_im):
    n = SSM_GROUPS * SSM_STATE
    row = jax.ShapeDtypeStruct((1, n), F32)
    mat = jax.ShapeDtypeStruct((SSM_GROUP_DIM, n), F32)
    to_mat = lambda b: jnp.transpose(b, (2, 0, 1)).reshape(SSM_GROUP_DIM, n)
    ldt = jnp.broadcast_to(log_dt[:, None], (SSM_GROUPS, SSM_STATE)).reshape(1, n)
    return pl.pallas_call(_s5_disc_body, out_shape=(row, row, mat, mat), name="s5_disc")(
        lam_re.reshape(1, n), lam_im.reshape(1, n), ldt, to_mat(b_re), to_mat(b_im))


def _resident(shape):
    zeros = (0,) * len(shape)
    return pl.BlockSpec(shape, lambda i: zeros, pipeline_mode=pl.Buffered(1))


def _s5_in_body(x_ref, g_ref, w_ref, wo_ref, o_ref, wob_ref, wb_ref):
    @pl.when(pl.program_id(0) == 0)
    def _():
        wb_ref[...] = w_ref[...].astype(BF16)

    wob_ref[...] = wo_ref[...].astype(BF16)
    h = _rms(x_ref[...], g_ref[...]).astype(BF16)
    o_ref[...] = jnp.dot(h, wb_ref[...], preferred_element_type=F32)


def _s5_in(x, norm_g, w, w_out, layer, *, tm=512):
    steps = TOKENS // tm
    wo_block = pl.BlockSpec((D_MODEL // steps, 2 * D_MODEL), lambda i: (i, 0))
    return pl.pallas_call(
        _s5_in_body,
        grid=(steps,),
        in_specs=[
            pl.BlockSpec((tm, D_MODEL), lambda i: (i, 0)),
            pl.BlockSpec((None, None, 1, D_MODEL), lambda i: (layer, 1, 0, 0)),
            _resident((D_MODEL, D_MODEL)),
            wo_block,
        ],
        out_specs=[pl.BlockSpec((tm, D_MODEL), lambda i: (i, 0)), wo_block],
        out_shape=[jax.ShapeDtypeStruct((TOKENS, D_MODEL), F32),
                   jax.ShapeDtypeStruct((D_MODEL, 2 * D_MODEL), BF16)],
        scratch_shapes=[pltpu.VMEM((D_MODEL, D_MODEL), BF16)],
        compiler_params=_params(("arbitrary",)),
        name="s5_in",
    )(x, norm_g, w, w_out)


def _s5_scan_body(u_ref, bmat_ref, cmat_ref, ar_ref, ai_ref, d_ref, o_ref,
                  lhs_ref, bu0_ref, bu1_ref, hs0_ref, hs1_ref, yp_ref, carry_ref):
    tile = SUBLANES
    lane = lax.broadcasted_iota(jnp.int32, (tile, LANES), 1)
    low = lane < HALF_CH
    bu_refs = (bu0_ref, bu1_ref)
    hs_refs = (hs0_ref, hs1_ref)

    @pl.when(pl.program_id(1) == 0)
    def _():
        carry_ref[...] = jnp.zeros_like(carry_ref)

    re, im = slice(0, HALF_STATES), slice(HALF_STATES, 2 * HALF_STATES)
    pair_slabs = lambda pair: range(pair * SCAN_PAIR, (pair + 1) * SCAN_PAIR)

    def to_states(pair):
        for k, s in enumerate(pair_slabs(pair)):
            for b in range(BATCH):
                for t0 in range(0, SCAN_T, tile):
                    v = u_ref[b, t0:t0 + tile, s * LANES:(s + 1) * LANES]
                    lhs_ref[s, pl.ds(t0 * tile + b, tile, stride=tile), :] = jnp.where(low, v, 0.0)
                    lhs_ref[s, pl.ds(t0 * tile + BATCH + b, tile, stride=tile), :] = jnp.where(low, 0.0, v)
            bu_refs[pair][k] = jnp.dot(lhs_ref[s].astype(BF16), bmat_ref[s],
                                       preferred_element_type=F32)

    def scan(pair):
        bu_ref, hs_ref = bu_refs[pair], hs_refs[pair]
        state = [(ar_ref[s], ai_ref[s], carry_ref[s, :, re], carry_ref[s, :, im])
                 for s in pair_slabs(pair)]
        for t in range(0, SCAN_T, 2):
            for k in range(SCAN_PAIR):
                a_r, a_i, hr, hi = state[k]
                steps = []
                for rows in (slice(t * tile, (t + 1) * tile), slice((t + 1) * tile, (t + 2) * tile)):
                    hr, hi = ((a_r * hr - a_i * hi) + bu_ref[k, rows, re],
                              (a_r * hi + a_i * hr) + bu_ref[k, rows, im])
                    steps.append((hr, hi))
                both = slice(t * tile, (t + 2) * tile)
                hs_ref[k, both, re] = jnp.concatenate([steps[0][0], steps[1][0]], axis=0).astype(BF16)
                hs_ref[k, both, im] = jnp.concatenate([steps[0][1], steps[1][1]], axis=0).astype(BF16)
                state[k] = (a_r, a_i, hr, hi)
        for k, s in enumerate(pair_slabs(pair)):
            carry_ref[s, :, re] = state[k][2]
            carry_ref[s, :, im] = state[k][3]

    def from_states(pair):
        for k, s in enumerate(pair_slabs(pair)):
            yp_ref[s] = jnp.dot(hs_refs[pair][k], cmat_ref[s], preferred_element_type=F32)
            sl = slice(s * LANES, (s + 1) * LANES)
            dsk = d_ref[:, sl]
            for b in range(BATCH):
                for t0 in range(0, SCAN_T, 2 * tile):
                    ys = []
                    for t1 in (t0, t0 + tile):
                        y0 = yp_ref[s, pl.ds(t1 * tile + b, tile, stride=tile), :]
                        y1 = yp_ref[s, pl.ds(t1 * tile + BATCH + b, tile, stride=tile), :]
                        ys.append(jnp.where(low, y0, y1) + dsk * u_ref[b, t1:t1 + tile, sl])
                    o_ref[b, t0:t0 + 2 * tile, sl] = _gelu(jnp.concatenate(ys, axis=0)).astype(BF16)

    pairs = range(SCAN_SLABS // SCAN_PAIR)
    for pair in pairs:
        to_states(pair)
    for pair in pairs:
        scan(pair)
        from_states(pair)


def _s5_scan(u, bmat, cmat, a_r, a_i, d):
    width = SCAN_SLABS * LANES
    rows = SCAN_T * SUBLANES
    bu_buf = pltpu.VMEM((SCAN_PAIR, rows, 2 * HALF_STATES), F32)
    hs_buf = pltpu.VMEM((SCAN_PAIR, rows, 2 * HALF_STATES), BF16)
    return pl.pallas_call(
        _s5_scan_body,
        grid=(N_SLABS // SCAN_SLABS, SEQ // SCAN_T),
        in_specs=[
            pl.BlockSpec((BATCH, SCAN_T, width), lambda s, c: (0, c, s)),
            pl.BlockSpec((SCAN_SLABS, LANES, 2 * HALF_STATES), lambda s, c: (s, 0, 0)),
            pl.BlockSpec((SCAN_SLABS, 2 * HALF_STATES, LANES), lambda s, c: (s, 0, 0)),
            pl.BlockSpec((SCAN_SLABS, SUBLANES, HALF_STATES), lambda s, c: (s, 0, 0)),
            pl.BlockSpec((SCAN_SLABS, SUBLANES, HALF_STATES), lambda s, c: (s, 0, 0)),
            pl.BlockSpec((1, width), lambda s, c: (0, s)),
        ],
        out_specs=pl.BlockSpec((BATCH, SCAN_T, width), lambda s, c: (0, c, s)),
        out_shape=jax.ShapeDtypeStruct((BATCH, SEQ, D_MODEL), BF16),
        scratch_shapes=[
            pltpu.VMEM((SCAN_SLABS, rows, LANES), F32),
            bu_buf, bu_buf, hs_buf, hs_buf,
            pltpu.VMEM((SCAN_SLABS, rows, LANES), F32),
            pltpu.VMEM((SCAN_SLABS, SUBLANES, 2 * HALF_STATES), F32),
        ],
        compiler_params=_params(("parallel", "arbitrary")),
        name="s5_scan",
    )(u, bmat, cmat, a_r, a_i, d)


def _s5_scan_operands(a_r, a_i, bb_r, bb_i, c_re, c_im):
    halves = SLAB_GROUPS // 2
    eye = jnp.eye(halves, dtype=F32)

    def b_blocks(bb):
        bb = bb.reshape(SSM_GROUP_DIM, N_SLABS, 2, halves, SSM_STATE)
        blk = jnp.einsum('hsqgp,gk->sqghkp', bb, eye)
        return blk.reshape(N_SLABS, LANES, HALF_STATES)

    def c_blocks(c):
        c = c.reshape(N_SLABS, 2, halves, SSM_GROUP_DIM, SSM_STATE)
        blk = jnp.einsum('sqghp,gk->skpqgh', c, eye)
        return blk.reshape(N_SLABS, HALF_STATES, LANES)

    bmat = jnp.concatenate([b_blocks(bb_r), b_blocks(bb_i)], axis=-1).astype(BF16)
    cmat = jnp.concatenate([c_blocks(c_re), -c_blocks(c_im)], axis=1).astype(BF16)

    def a_rows(a):
        a = a.reshape(N_SLABS, 2, 1, HALF_STATES)
        return jnp.broadcast_to(a, (N_SLABS, 2, BATCH, HALF_STATES)).reshape(
            N_SLABS, SUBLANES, HALF_STATES)

    return bmat, cmat, a_rows(a_r), a_rows(a_i)


def _s5_out_body(x_ref, a_ref, w_ref, o_ref, *, tn):
    a = a_ref[...]
    for c in range(0, D_MODEL, tn):
        val = jnp.dot(a, w_ref[:, c:c + tn], preferred_element_type=F32)
        gate = jnp.dot(a, w_ref[:, D_MODEL + c:D_MODEL + c + tn], preferred_element_type=F32)
        o_ref[:, c:c + tn] = x_ref[:, c:c + tn] + val * _sigmoid(gate)


def _s5_out(x, act, w, *, tm=512, tn=512):
    return pl.pallas_call(
        functools.partial(_s5_out_body, tn=tn),
        grid=(TOKENS // tm,),
        in_specs=[
            pl.BlockSpec((tm, D_MODEL), lambda i: (i, 0)),
            pl.BlockSpec((tm, D_MODEL), lambda i: (i, 0)),
            _resident((D_MODEL, 2 * D_MODEL)),
        ],
        out_specs=pl.BlockSpec((tm, D_MODEL), lambda i: (i, 0)),
        out_shape=jax.ShapeDtypeStruct((TOKENS, D_MODEL), F32),
        compiler_params=_params(("parallel",)),
        name="s5_out",
    )(x, act, w)


def _s5(x, norm_g, layer, w_in, lam_re, lam_im, log_dt, b_re, b_im, c_re, c_im, d_skip, w_out):
    a_r, a_i, bb_r, bb_i = _s5_disc(lam_re, lam_im, log_dt, b_re, b_im)
    bmat, cmat, ar_rows, ai_rows = _s5_scan_operands(a_r, a_i, bb_r, bb_i, c_re, c_im)
    u, w_out_bf = _s5_in(x, norm_g, w_in, w_out, layer)
    act = _s5_scan(u.reshape(BATCH, SEQ, D_MODEL), bmat, cmat, ar_rows, ai_rows,
                   d_skip.reshape(1, D_MODEL))
    return _s5_out(x, act.reshape(TOKENS, D_MODEL), w_out_bf)


def _ple_body(x_ref, g_ref, wg_ref, p_ref, wp_ref, fg_ref, o_ref, wgb_ref, wpb_ref, *, final):
    @pl.when(pl.program_id(0) == 0)
    def _():
        wgb_ref[...] = wg_ref[...].astype(BF16)
        wpb_ref[...] = wp_ref[...].astype(BF16)

    x = x_ref[...]
    h = _rms(x, g_ref[...]).astype(BF16)
    gate = _sigmoid(jnp.dot(h, wgb_ref[...], preferred_element_type=F32))
    proj = jnp.dot(p_ref[...].astype(BF16), wpb_ref[...], preferred_element_type=F32)
    y = x + gate * proj
    o_ref[...] = _rms(y, fg_ref[...]) if final else y


def _ple(x, norm_g, wg, p, wp, fg, layer, *, final, tm=512):
    once = pl.Buffered(1)
    return pl.pallas_call(
        functools.partial(_ple_body, final=final),
        grid=(TOKENS // tm,),
        in_specs=[
            pl.BlockSpec((tm, D_MODEL), lambda i: (i, 0)),
            pl.BlockSpec((None, None, 1, D_MODEL), lambda i: (layer, 3, 0, 0)),
            pl.BlockSpec((None, D_MODEL, D_MODEL), lambda i: (layer, 0, 0), pipeline_mode=once),
            pl.BlockSpec((None, tm, PLE_DIM), lambda i: (layer, i, 0)),
            pl.BlockSpec((None, PLE_DIM, D_MODEL), lambda i: (layer, 0, 0), pipeline_mode=once),
            pl.BlockSpec((1, D_MODEL), lambda i: (0, 0)),
        ],
        out_specs=pl.BlockSpec((tm, D_MODEL), lambda i: (i, 0)),
        out_shape=jax.ShapeDtypeStruct((TOKENS, D_MODEL), F32),
        scratch_shapes=[pltpu.VMEM((D_MODEL, D_MODEL), BF16), pltpu.VMEM((PLE_DIM, D_MODEL), BF16)],
        compiler_params=_params(("arbitrary",)),
        name="ple",
    )(x, norm_g, wg, p, wp, fg)


def kernel(x, p, norm_g, final_norm_g, ffn_w_gate, ffn_w_up, ffn_w_down, gmlp_w_in, gmlp_ln_g, gmlp_ln_b, gmlp_w_s, gmlp_b_s, gmlp_w_out, s5_w_in, s5_lam_re, s5_lam_im, s5_log_dt, s5_b_re, s5_b_im, s5_c_re, s5_c_im, s5_d, s5_w_out, ple_w_gate, ple_w_proj):
    x = x.reshape(TOKENS, D_MODEL)
    p = p.reshape(DEPTH, TOKENS, PLE_DIM)
    norm_g = norm_g.reshape(DEPTH, 4, 1, D_MODEL)
    fg = final_norm_g.reshape(1, D_MODEL)
    for i in range(DEPTH):
        j = i // 2
        if i % 2 == 0:
            side = ((gmlp_w_in[j], GMLP_CAST_BLOCKS), (gmlp_w_out[j], GMLP_CAST_BLOCKS))
            x, (w_in, w_out) = _ffn(x, norm_g, ffn_w_gate, ffn_w_up, ffn_w_down, i, 0,
                                    side=side, tf=256, sub_rows=512, k_parts=4)
            x = _gmlp(x, norm_g, w_in, gmlp_ln_g[j].reshape(1, GMLP_HALF),
                      gmlp_ln_b[j].reshape(1, GMLP_HALF), gmlp_w_s[j],
                      gmlp_b_s[j].reshape(GMLP_GROUPS, CHUNK, 1), w_out, i)
        else:
            x, _ = _ffn(x, norm_g, ffn_w_gate, ffn_w_up, ffn_w_down, i, 0)
            x = _s5(x, norm_g, i, s5_w_in[j], s5_lam_re[j], s5_lam_im[j], s5_log_dt[j],
                    s5_b_re[j], s5_b_im[j], s5_c_re[j], s5_c_im[j], s5_d[j], s5_w_out[j])
        x, _ = _ffn(x, norm_g, ffn_w_gate, ffn_w_up, ffn_w_down, i, 1)
        x = _ple(x, norm_g, ple_w_gate, p, ple_w_proj, fg, i, final=(i == DEPTH - 1))
    return x.reshape(BATCH, SEQ, D_MODEL)
```

```python
import functools
import math

import jax
import jax.numpy as jnp
from jax import lax
from jax.experimental import pallas as pl
from jax.experimental.pallas import tpu as pltpu

F32 = jnp.float32
BF16 = jnp.bfloat16

D_MODEL = 2048
BATCH = 4
SEQ = 2048
TOKENS = BATCH * SEQ
DEPTH = 2
D_FF = 5632
PLE_DIM = 256
CHUNK = 128
GMLP_HALF = 3 * D_MODEL
GMLP_GROUPS = 16
GMLP_GROUP_DIM = GMLP_HALF // GMLP_GROUPS
SSM_GROUP_DIM = 16
SSM_GROUPS = D_MODEL // SSM_GROUP_DIM
SSM_STATE = 64
EPS = 1e-6

LANES = 128
SUBLANES = 8
VMEM_LIMIT = 60 * 1024 * 1024

SLAB_GROUPS = LANES // SSM_GROUP_DIM
N_SLABS = D_MODEL // LANES
SLAB_STATES = SLAB_GROUPS * SSM_STATE
HALF_STATES = SLAB_STATES // 2
HALF_CH = LANES // 2
SCAN_PAIR = 2
SCAN_SLABS = 2 * SCAN_PAIR
SCAN_T = 256


def _rms(x, g):
    ms = jnp.mean(x * x, axis=-1, keepdims=True)
    return x * lax.rsqrt(ms + EPS) * g


def _gelu(x):
    c = math.sqrt(2.0 / math.pi)
    return x * (0.5 * (1.0 + jnp.tanh(c * (x + 0.044715 * (x * x * x)))))


def _sigmoid(x):
    return 1.0 / (1.0 + jnp.exp(-x))


def _params(sem):
    return pltpu.CompilerParams(dimension_semantics=sem, vmem_limit_bytes=VMEM_LIMIT)


def _gain_spec(layer, slot):
    return pl.BlockSpec((None, None, 1, D_MODEL), lambda i, j: (layer, slot, 0, 0))


def _token_tile_copy(x_hbm, x_ref, sem, tile):
    tm = x_ref.shape[0]
    rows = pl.ds(pl.multiple_of(tile * tm, tm), tm)
    return pltpu.make_async_copy(x_hbm.at[rows, :], x_ref, sem)


def _wait_token_tile(x_hbm, x_ref, sem):
    i = pl.program_id(0)

    @pl.when(i == 0)
    def _():
        _token_tile_copy(x_hbm, x_ref, sem, i).start()

    _token_tile_copy(x_hbm, x_ref, sem, i).wait()


def _prefetch_next_token_tile(x_hbm, x_ref, sem):
    i, j = pl.program_id(0), pl.program_id(1)

    @pl.when(jnp.logical_and(j == 1, i + 1 < pl.num_programs(0)))
    def _():
        _token_tile_copy(x_hbm, x_ref, sem, i + 1).start(priority=1)


def _ffn_body(*refs, n_side, sub_rows):
    x_hbm, g_ref, wg_lo, wg_hi, wu_lo, wu_hi, wd_ref = refs[:7]
    side_in = refs[7:7 + n_side]
    o_ref = refs[7 + n_side]
    side_out = refs[8 + n_side:8 + 2 * n_side]
    h_ref, x_ref, sem = refs[8 + 2 * n_side:]

    @pl.when(pl.program_id(1) == 0)
    def _():
        _wait_token_tile(x_hbm, x_ref, sem)
        x = x_ref[...]
        h_ref[...] = _rms(x, g_ref[...]).astype(BF16)
        o_ref[...] = x

    _prefetch_next_token_tile(x_hbm, x_ref, sem)

    for src, dst in zip(side_in, side_out):
        dst[...] = src[...].astype(BF16)

    wg = jnp.concatenate([wg_lo[...].astype(BF16), wg_hi[...].astype(BF16)], axis=0)
    wu = jnp.concatenate([wu_lo[...].astype(BF16), wu_hi[...].astype(BF16)], axis=0)
    wd = wd_ref[...].astype(BF16)
    subs = [slice(r, r + sub_rows) for r in range(0, x_ref.shape[0], sub_rows)]
    acts = []
    for rows in subs:
        h = h_ref[rows, :]
        gate = jnp.dot(h, wg, preferred_element_type=F32)
        up = jnp.dot(h, wu, preferred_element_type=F32)
        acts.append(((gate * _sigmoid(gate)) * (0.5 * up)).astype(BF16))
    for rows, a in zip(subs, acts):
        o_ref[rows, :] += jnp.dot(a, wd, preferred_element_type=F32)


def _ffn(x, norm_g, wg, wu, wd, layer, which, *, side=(), tm=1024, tf=512, sub_rows=None):
    ni, nj = TOKENS // tm, D_FF // tf
    side_specs, side_shapes = [], []
    for arr, n_blocks in side:
        assert n_blocks <= ni * nj and arr.shape[0] % n_blocks == 0
        block = (arr.shape[0] // n_blocks, arr.shape[1])
        assert block[0] % (2 * SUBLANES) == 0
        index = functools.partial(
            lambda i, j, last: (jnp.minimum(i * nj + j, last), 0), last=n_blocks - 1)
        side_specs.append(pl.BlockSpec(block, index))
        side_shapes.append(jax.ShapeDtypeStruct(arr.shape, BF16))
    half_k = lambda r: pl.BlockSpec((None, None, D_MODEL // 2, tf),
                                    lambda i, j: (layer, which, r, j))
    out = pl.pallas_call(
        functools.partial(_ffn_body, n_side=len(side), sub_rows=sub_rows or tm),
        grid=(ni, nj),
        in_specs=[
            pl.BlockSpec(memory_space=pl.ANY),
            _gain_spec(layer, 2 * which),
            half_k(0), half_k(1), half_k(0), half_k(1),
            pl.BlockSpec((None, None, tf, D_MODEL), lambda i, j: (layer, which, j, 0)),
        ] + side_specs,
        out_specs=[pl.BlockSpec((tm, D_MODEL), lambda i, j: (i, 0))] + side_specs,
        out_shape=[jax.ShapeDtypeStruct((TOKENS, D_MODEL), F32)] + side_shapes,
        scratch_shapes=[pltpu.VMEM((tm, D_MODEL), BF16), pltpu.VMEM((tm, D_MODEL), F32),
                        pltpu.SemaphoreType.DMA(())],
        compiler_params=_params(("arbitrary", "arbitrary")),
        name="ffn",
    )(x, norm_g, wg, wg, wu, wu, wd, *[arr for arr, _ in side])
    return out[0], out[1:]


GMLP_STEP_GROUPS = 2
GMLP_CAST_BLOCKS = 128
GMLP_SUBTILE = 256
GMLP_STEP_W = GMLP_STEP_GROUPS * GMLP_GROUP_DIM


def _gmlp_body(x_hbm, g_ref, wu_ref, wv_ref, lng_ref, lnb_ref, ws_ref, bs_ref, wo_ref,
               o_ref, h_ref, gated_ref, x_ref, sem):
    @pl.when(pl.program_id(1) == 0)
    def _():
        _wait_token_tile(x_hbm, x_ref, sem)
        x = x_ref[...]
        h_ref[...] = _rms(x, g_ref[...]).astype(BF16)
        o_ref[...] = x

    _prefetch_next_token_tile(x_hbm, x_ref, sem)

    tm = x_ref.shape[0]
    subs = [slice(r, r + GMLP_SUBTILE) for r in range(0, tm, GMLP_SUBTILE)]
    z = []
    for rows in subs:
        h = h_ref[rows, :]
        zv = _gelu(jnp.dot(h, wv_ref[...], preferred_element_type=F32))
        zu = _gelu(jnp.dot(h, wu_ref[...], preferred_element_type=F32))
        z.append((zu, zv))
    row = lax.broadcasted_iota(jnp.int32, (CHUNK, CHUNK), 0)
    col = lax.broadcasted_iota(jnp.int32, (CHUNK, CHUNK), 1)
    ws = [jnp.where(row >= col, ws_ref[gi], 0.0).astype(BF16)
          for gi in range(GMLP_STEP_GROUPS)]
    for rows, (zu, zv) in zip(subs, z):
        for gi in range(GMLP_STEP_GROUPS):
            lo, hi = gi * GMLP_GROUP_DIM, (gi + 1) * GMLP_GROUP_DIM
            v = zv[:, lo:hi]
            d = v - jnp.mean(v, axis=-1, keepdims=True)
            var = jnp.mean(d * d, axis=-1, keepdims=True)
            vn = ((d * lax.rsqrt(var + EPS)) * lng_ref[:, lo:hi] + lnb_ref[:, lo:hi]).astype(BF16)
            bias = bs_ref[gi]
            for c in range(0, GMLP_SUBTILE, CHUNK):
                sv = jnp.dot(ws[gi], vn[c:c + CHUNK], preferred_element_type=F32) + bias
                out_rows = slice(rows.start + c, rows.start + c + CHUNK)
                gated_ref[out_rows, lo:hi] = (zu[c:c + CHUNK, lo:hi] * sv).astype(BF16)
    for rows in subs:
        o_ref[rows, :] += jnp.dot(gated_ref[rows, :], wo_ref[...], preferred_element_type=F32)


def _gmlp(x, norm_g, w_in, ln_g, ln_b, w_s, b_s, w_out, layer, *, tm=1024):
    nj = GMLP_GROUPS // GMLP_STEP_GROUPS
    return pl.pallas_call(
        _gmlp_body,
        grid=(TOKENS // tm, nj),
        in_specs=[
            pl.BlockSpec(memory_space=pl.ANY),
            _gain_spec(layer, 1),
            pl.BlockSpec((D_MODEL, GMLP_STEP_W), lambda i, j: (0, j)),
            pl.BlockSpec((D_MODEL, GMLP_STEP_W), lambda i, j: (0, j + nj)),
            pl.BlockSpec((1, GMLP_STEP_W), lambda i, j: (0, j)),
            pl.BlockSpec((1, GMLP_STEP_W), lambda i, j: (0, j)),
            pl.BlockSpec((GMLP_STEP_GROUPS, CHUNK, CHUNK), lambda i, j: (j, 0, 0)),
            pl.BlockSpec((GMLP_STEP_GROUPS, CHUNK, 1), lambda i, j: (j, 0, 0)),
            pl.BlockSpec((GMLP_STEP_W, D_MODEL), lambda i, j: (j, 0)),
        ],
        out_specs=pl.BlockSpec((tm, D_MODEL), lambda i, j: (i, 0)),
        out_shape=jax.ShapeDtypeStruct((TOKENS, D_MODEL), F32),
        scratch_shapes=[pltpu.VMEM((tm, D_MODEL), BF16), pltpu.VMEM((tm, GMLP_STEP_W), BF16),
                        pltpu.VMEM((tm, D_MODEL), F32), pltpu.SemaphoreType.DMA(())],
        compiler_params=_params(("arbitrary", "arbitrary")),
        name="gmlp",
    )(x, norm_g, w_in, w_in, ln_g, ln_b, w_s, b_s, w_out)


def _s5_disc_body(lr_ref, li_ref, ldt_ref, br_ref, bi_ref, ar_ref, ai_ref, bbr_ref, bbi_ref):
    lr, li = lr_ref[...], li_ref[...]
    dt = jnp.exp(ldt_ref[...])
    mag = jnp.exp(lr * dt)
    ang = li * dt
    a_r = mag * jnp.cos(ang)
    a_i = mag * jnp.sin(ang)
    den = lr * lr + li * li
    nr = a_r - 1.0
    z_r = (nr * lr + a_i * li) / den
    z_i = (a_i * lr - nr * li) / den
    ar_ref[...] = a_r
    ai_ref[...] = a_i
    bbr_ref[...] = z_r * br_ref[...] - z_i * bi_ref[...]
    bbi_ref[...] = z_r * bi_ref[...] + z_i * br_ref[...]


def _s5_disc(lam_re, lam_im, log_dt, b_re, b_im):
    n = SSM_GROUPS * SSM_STATE
    row = jax.ShapeDtypeStruct((1, n), F32)
    mat = jax.ShapeDtypeStruct((SSM_GROUP_DIM, n), F32)
    to_mat = lambda b: jnp.transpose(b, (2, 0, 1)).reshape(SSM_GROUP_DIM, n)
    ldt = jnp.broadcast_to(log_dt[:, None], (SSM_GROUPS, SSM_STATE)).reshape(1, n)
    return pl.pallas_call(_s5_disc_body, out_shape=(row, row, mat, mat), name="s5_disc")(
        lam_re.reshape(1, n), lam_im.reshape(1, n), ldt, to_mat(b_re), to_mat(b_im))


def _resident(shape):
    zeros = (0,) * len(shape)
    return pl.BlockSpec(shape, lambda i: zeros, pipeline_mode=pl.Buffered(1))


def _s5_in_body(x_ref, g_ref, w_ref, wo_ref, o_ref, wob_ref, wb_ref):
    @pl.when(pl.program_id(0) == 0)
    def _():
        wb_ref[...] = w_ref[...].astype(BF16)

    wob_ref[...] = wo_ref[...].astype(BF16)
    h = _rms(x_ref[...], g_ref[...]).astype(BF16)
    o_ref[...] = jnp.dot(h, wb_ref[...], preferred_element_type=F32)


def _s5_in(x, norm_g, w, w_out, layer, *, tm=512):
    steps = TOKENS // tm
    wo_block = pl.BlockSpec((D_MODEL // steps, 2 * D_MODEL), lambda i: (i, 0))
    return pl.pallas_call(
        _s5_in_body,
        grid=(steps,),
        in_specs=[
            pl.BlockSpec((tm, D_MODEL), lambda i: (i, 0)),
            pl.BlockSpec((None, None, 1, D_MODEL), lambda i: (layer, 1, 0, 0)),
            _resident((D_MODEL, D_MODEL)),
            wo_block,
        ],
        out_specs=[pl.BlockSpec((tm, D_MODEL), lambda i: (i, 0)), wo_block],
        out_shape=[jax.ShapeDtypeStruct((TOKENS, D_MODEL), F32),
                   jax.ShapeDtypeStruct((D_MODEL, 2 * D_MODEL), BF16)],
        scratch_shapes=[pltpu.VMEM((D_MODEL, D_MODEL), BF16)],
        compiler_params=_params(("arbitrary",)),
        name="s5_in",
    )(x, norm_g, w, w_out)


def _s5_scan_body(u_ref, bmat_ref, cmat_ref, ar_ref, ai_ref, d_ref, o_ref,
                  lhs_ref, bu0_ref, bu1_ref, hs0_ref, hs1_ref, yp_ref, carry_ref):
    tile = SUBLANES
    lane = lax.broadcasted_iota(jnp.int32, (tile, LANES), 1)
    low = lane < HALF_CH
    bu_refs = (bu0_ref, bu1_ref)
    hs_refs = (hs0_ref, hs1_ref)

    @pl.when(pl.program_id(1) == 0)
    def _():
        carry_ref[...] = jnp.zeros_like(carry_ref)

    re, im = slice(0, HALF_STATES), slice(HALF_STATES, 2 * HALF_STATES)
    pair_slabs = lambda pair: range(pair * SCAN_PAIR, (pair + 1) * SCAN_PAIR)

    def to_states(pair):
        for k, s in enumerate(pair_slabs(pair)):
            for b in range(BATCH):
                for t0 in range(0, SCAN_T, tile):
                    v = u_ref[b, t0:t0 + tile, s * LANES:(s + 1) * LANES]
                    lhs_ref[s, pl.ds(t0 * tile + b, tile, stride=tile), :] = jnp.where(low, v, 0.0)
                    lhs_ref[s, pl.ds(t0 * tile + BATCH + b, tile, stride=tile), :] = jnp.where(low, 0.0, v)
            bu_refs[pair][k] = jnp.dot(lhs_ref[s].astype(BF16), bmat_ref[s],
                                       preferred_element_type=F32)

    def scan(pair):
        bu_ref, hs_ref = bu_refs[pair], hs_refs[pair]
        state = [(ar_ref[s], ai_ref[s], carry_ref[s, :, re], carry_ref[s, :, im])
                 for s in pair_slabs(pair)]
        for t in range(0, SCAN_T, 2):
            for k in range(SCAN_PAIR):
                a_r, a_i, hr, hi = state[k]
                steps = []
                for rows in (slice(t * tile, (t + 1) * tile), slice((t + 1) * tile, (t + 2) * tile)):
                    hr, hi = ((a_r * hr - a_i * hi) + bu_ref[k, rows, re],
                              (a_r * hi + a_i * hr) + bu_ref[k, rows, im])
                    steps.append((hr, hi))
                both = slice(t * tile, (t + 2) * tile)
                hs_ref[k, both, re] = jnp.concatenate([steps[0][0], steps[1][0]], axis=0).astype(BF16)
                hs_ref[k, both, im] = jnp.concatenate([steps[0][1], steps[1][1]], axis=0).astype(BF16)
                state[k] = (a_r, a_i, hr, hi)
        for k, s in enumerate(pair_slabs(pair)):
            carry_ref[s, :, re] = state[k][2]
            carry_ref[s, :, im] = state[k][3]

    def from_states(pair):
        for k, s in enumerate(pair_slabs(pair)):
            yp_ref[s] = jnp.dot(hs_refs[pair][k], cmat_ref[s], preferred_element_type=F32)
            sl = slice(s * LANES, (s + 1) * LANES)
            dsk = d_ref[:, sl]
            for b in range(BATCH):
                for t0 in range(0, SCAN_T, 2 * tile):
                    ys = []
                    for t1 in (t0, t0 + tile):
                        y0 = yp_ref[s, pl.ds(t1 * tile + b, tile, stride=tile), :]
                        y1 = yp_ref[s, pl.ds(t1 * tile + BATCH + b, tile, stride=tile), :]
                        ys.append(jnp.where(low, y0, y1) + dsk * u_ref[b, t1:t1 + tile, sl])
                    o_ref[b, t0:t0 + 2 * tile, sl] = _gelu(jnp.concatenate(ys, axis=0)).astype(BF16)

    pairs = range(SCAN_SLABS // SCAN_PAIR)
    for pair in pairs:
        to_states(pair)
    for pair in pairs:
        scan(pair)
        from_states(pair)


def _s5_scan(u, bmat, cmat, a_r, a_i, d):
    width = SCAN_SLABS * LANES
    rows = SCAN_T * SUBLANES
    bu_buf = pltpu.VMEM((SCAN_PAIR, rows, 2 * HALF_STATES), F32)
    hs_buf = pltpu.VMEM((SCAN_PAIR, rows, 2 * HALF_STATES), BF16)
    return pl.pallas_call(
        _s5_scan_body,
        grid=(N_SLABS // SCAN_SLABS, SEQ // SCAN_T),
        in_specs=[
            pl.BlockSpec((BATCH, SCAN_T, width), lambda s, c: (0, c, s)),
            pl.BlockSpec((SCAN_SLABS, LANES, 2 * HALF_STATES), lambda s, c: (s, 0, 0)),
            pl.BlockSpec((SCAN_SLABS, 2 * HALF_STATES, LANES), lambda s, c: (s, 0, 0)),
            pl.BlockSpec((SCAN_SLABS, SUBLANES, HALF_STATES), lambda s, c: (s, 0, 0)),
            pl.BlockSpec((SCAN_SLABS, SUBLANES, HALF_STATES), lambda s, c: (s, 0, 0)),
            pl.BlockSpec((1, width), lambda s, c: (0, s)),
        ],
        out_specs=pl.BlockSpec((BATCH, SCAN_T, width), lambda s, c: (0, c, s)),
        out_shape=jax.ShapeDtypeStruct((BATCH, SEQ, D_MODEL), BF16),
        scratch_shapes=[
            pltpu.VMEM((SCAN_SLABS, rows, LANES), F32),
            bu_buf, bu_buf, hs_buf, hs_buf,
            pltpu.VMEM((SCAN_SLABS, rows, LANES), F32),
            pltpu.VMEM((SCAN_SLABS, SUBLANES, 2 * HALF_STATES), F32),
        ],
        compiler_params=_params(("parallel", "arbitrary")),
        name="s5_scan",
    )(u, bmat, cmat, a_r, a_i, d)


def _s5_scan_operands(a_r, a_i, bb_r, bb_i, c_re, c_im):
    halves = SLAB_GROUPS // 2
    eye = jnp.eye(halves, dtype=F32)

    def b_blocks(bb):
        bb = bb.reshape(SSM_GROUP_DIM, N_SLABS, 2, halves, SSM_STATE)
        blk = jnp.einsum('hsqgp,gk->sqghkp', bb, eye)
        return blk.reshape(N_SLABS, LANES, HALF_STATES)

    def c_blocks(c):
        c = c.reshape(N_SLABS, 2, halves, SSM_GROUP_DIM, SSM_STATE)
        blk = jnp.einsum('sqghp,gk->skpqgh', c, eye)
        return blk.reshape(N_SLABS, HALF_STATES, LANES)

    bmat = jnp.concatenate([b_blocks(bb_r), b_blocks(bb_i)], axis=-1).astype(BF16)
    cmat = jnp.concatenate([c_blocks(c_re), -c_blocks(c_im)], axis=1).astype(BF16)

    def a_rows(a):
        a = a.reshape(N_SLABS, 2, 1, HALF_STATES)
        return jnp.broadcast_to(a, (N_SLABS, 2, BATCH, HALF_STATES)).reshape(
            N_SLABS, SUBLANES, HALF_STATES)

    return bmat, cmat, a_rows(a_r), a_rows(a_i)


def _s5_out_body(x_ref, a_ref, w_ref, o_ref, *, tn):
    a = a_ref[...]
    for c in range(0, D_MODEL, tn):
        val = jnp.dot(a, w_ref[:, c:c + tn], preferred_element_type=F32)
        gate = jnp.dot(a, w_ref[:, D_MODEL + c:D_MODEL + c + tn], preferred_element_type=F32)
        o_ref[:, c:c + tn] = x_ref[:, c:c + tn] + val * _sigmoid(gate)


def _s5_out(x, act, w, *, tm=512, tn=512):
    return pl.pallas_call(
        functools.partial(_s5_out_body, tn=tn),
        grid=(TOKENS // tm,),
        in_specs=[
            pl.BlockSpec((tm, D_MODEL), lambda i: (i, 0)),
            pl.BlockSpec((tm, D_MODEL), lambda i: (i, 0)),
            _resident((D_MODEL, 2 * D_MODEL)),
        ],
        out_specs=pl.BlockSpec((tm, D_MODEL), lambda i: (i, 0)),
        out_shape=jax.ShapeDtypeStruct((TOKENS, D_MODEL), F32),
        compiler_params=_params(("parallel",)),
        name="s5_out",
    )(x, act, w)


def _s5(x, norm_g, layer, w_in, lam_re, lam_im, log_dt, b_re, b_im, c_re, c_im, d_skip, w_out):
    a_r, a_i, bb_r, bb_i = _s5_disc(lam_re, lam_im, log_dt, b_re, b_im)
    bmat, cmat, ar_rows, ai_rows = _s5_scan_operands(a_r, a_i, bb_r, bb_i, c_re, c_im)
    u, w_out_bf = _s5_in(x, norm_g, w_in, w_out, layer)
    act = _s5_scan(u.reshape(BATCH, SEQ, D_MODEL), bmat, cmat, ar_rows, ai_rows,
                   d_skip.reshape(1, D_MODEL))
    return _s5_out(x, act.reshape(TOKENS, D_MODEL), w_out_bf)


def _ple_body(x_ref, g_ref, wg_ref, p_ref, wp_ref, fg_ref, o_ref, wgb_ref, wpb_ref, *, final):
    @pl.when(pl.program_id(0) == 0)
    def _():
        wgb_ref[...] = wg_ref[...].astype(BF16)
        wpb_ref[...] = wp_ref[...].astype(BF16)

    x = x_ref[...]
    h = _rms(x, g_ref[...]).astype(BF16)
    gate = _sigmoid(jnp.dot(h, wgb_ref[...], preferred_element_type=F32))
    proj = jnp.dot(p_ref[...].astype(BF16), wpb_ref[...], preferred_element_type=F32)
    y = x + gate * proj
    o_ref[...] = _rms(y, fg_ref[...]) if final else y


def _ple(x, norm_g, wg, p, wp, fg, layer, *, final, tm=512):
    once = pl.Buffered(1)
    return pl.pallas_call(
        functools.partial(_ple_body, final=final),
        grid=(TOKENS // tm,),
        in_specs=[
            pl.BlockSpec((tm, D_MODEL), lambda i: (i, 0)),
            pl.BlockSpec((None, None, 1, D_MODEL), lambda i: (layer, 3, 0, 0)),
            pl.BlockSpec((None, D_MODEL, D_MODEL), lambda i: (layer, 0, 0), pipeline_mode=once),
            pl.BlockSpec((None, tm, PLE_DIM), lambda i: (layer, i, 0)),
            pl.BlockSpec((None, PLE_DIM, D_MODEL), lambda i: (layer, 0, 0), pipeline_mode=once),
            pl.BlockSpec((1, D_MODEL), lambda i: (0, 0)),
        ],
        out_specs=pl.BlockSpec((tm, D_MODEL), lambda i: (i, 0)),
        out_shape=jax.ShapeDtypeStruct((TOKENS, D_MODEL), F32),
        scratch_shapes=[pltpu.VMEM((D_MODEL, D_MODEL), BF16), pltpu.VMEM((PLE_DIM, D_MODEL), BF16)],
        compiler_params=_params(("arbitrary",)),
        name="ple",
    )(x, norm_g, wg, p, wp, fg)


def kernel(x, p, norm_g, final_norm_g, ffn_w_gate, ffn_w_up, ffn_w_down, gmlp_w_in, gmlp_ln_g, gmlp_ln_b, gmlp_w_s, gmlp_b_s, gmlp_w_out, s5_w_in, s5_lam_re, s5_lam_im, s5_log_dt, s5_b_re, s5_b_im, s5_c_re, s5_c_im, s5_d, s5_w_out, ple_w_gate, ple_w_proj):
    x = x.reshape(TOKENS, D_MODEL)
    p = p.reshape(DEPTH, TOKENS, PLE_DIM)
    norm_g = norm_g.reshape(DEPTH, 4, 1, D_MODEL)
    fg = final_norm_g.reshape(1, D_MODEL)
    for i in range(DEPTH):
        j = i // 2
        if i % 2 == 0:
            side = ((gmlp_w_in[j], GMLP_CAST_BLOCKS), (gmlp_w_out[j], GMLP_CAST_BLOCKS))
            x, (w_in, w_out) = _ffn(x, norm_g, ffn_w_gate, ffn_w_up, ffn_w_down, i, 0,
                                    side=side, tf=256, sub_rows=512)
            x = _gmlp(x, norm_g, w_in, gmlp_ln_g[j].reshape(1, GMLP_HALF),
                      gmlp_ln_b[j].reshape(1, GMLP_HALF), gmlp_w_s[j],
                      gmlp_b_s[j].reshape(GMLP_GROUPS, CHUNK, 1), w_out, i)
        else:
            x, _ = _ffn(x, norm_g, ffn_w_gate, ffn_w_up, ffn_w_down, i, 0)
            x = _s5(x, norm_g, i, s5_w_in[j], s5_lam_re[j], s5_lam_im[j], s5_log_dt[j],
                    s5_b_re[j], s5_b_im[j], s5_c_re[j], s5_c_im[j], s5_d[j], s5_w_out[j])
        x, _ = _ffn(x, norm_g, ffn_w_gate, ffn_w_up, ffn_w_down, i, 1)
        x = _ple(x, norm_g, ple_w_gate, p, ple_w_proj, fg, i, final=(i == DEPTH - 1))
    return x.reshape(BATCH, SEQ, D_MODEL)
```
